```python
import math
import jax, jax.numpy as jnp
from jax import lax
import numpy as np

D_MODEL = 1024
BATCH = 8
SEQ = 4096
DEPTH = 2

GRID_W = 64
CTX_LEN = 256
ATT_HEADS = 8
ATT_DH = 64
ATT_DV = 2 * ATT_DH
ATT_QBLOCK = 128
GM_WIDTH = 1024
GM_GROUPS = 8
GM_CHUNK = 128
HG_HEADS = 8
HG_DK = 128
HG_DV = 128
HG_CHUNK = 64
N_BRANCH = 3
D_FF = 2816
CONV_W = 3
ROPE_BASE = 10000.0
EPS = 1e-6

ATT_QW = ATT_HEADS * 2 * ATT_DH
ATT_VW = ATT_HEADS * ATT_DV
HG_KW = HG_HEADS * HG_DK
HG_VW = HG_HEADS * HG_DV
IN_SPLITS = (ATT_QW, ATT_QW, ATT_VW, GM_WIDTH, GM_WIDTH, HG_KW, HG_KW, HG_KW, HG_VW, HG_VW, N_BRANCH * D_MODEL)
IN_W = sum(IN_SPLITS)

kernel_name = "hybrid_diffattn_gmlp_hgrn2_dit_block"


def rmsnorm(x, g):
    xf = x.astype(jnp.float32)
    y = xf * lax.rsqrt(jnp.mean(xf * xf, axis=-1, keepdims=True) + EPS)
    return (y * g.astype(jnp.float32)).astype(x.dtype)


def layernorm(x, g, b):
    xf = x.astype(jnp.float32)
    mu = jnp.mean(xf, axis=-1, keepdims=True)
    var = jnp.mean(jnp.square(xf - mu), axis=-1, keepdims=True)
    y = (xf - mu) * lax.rsqrt(var + EPS)
    return (y * g.astype(jnp.float32) + b.astype(jnp.float32)).astype(x.dtype)


def modulate(x, shift, scale):
    return x * (1 + scale) + shift


def split_cols(p):
    out, start = [], 0
    for w in IN_SPLITS:
        out.append(p[..., start:start + w])
        start += w
    return out


def rope_1d(x, pos):
    half = x.shape[-1] // 2
    inv = ROPE_BASE ** (-jnp.arange(half, dtype=jnp.float32) / half)
    ang = pos.astype(jnp.float32)[:, None] * inv[None, :]
    cos = jnp.cos(ang)[None, :, None, None, :].astype(x.dtype)
    sin = jnp.sin(ang)[None, :, None, None, :].astype(x.dtype)
    x1, x2 = x[..., :half], x[..., half:]
    return jnp.concatenate([x1 * cos - x2 * sin, x2 * cos + x1 * sin], axis=-1)


def axial_rope(x, rows, cols):
    r = ATT_DH // 2
    return jnp.concatenate([rope_1d(x[..., :r], rows), rope_1d(x[..., r:], cols)], axis=-1)


def diff_attention(q, k, v, lam):
    B, T = q.shape[0], q.shape[1]
    nb = T // ATT_QBLOCK
    qb = q.reshape(B, nb, ATT_QBLOCK, ATT_HEADS, 2, ATT_DH).transpose(1, 0, 2, 3, 4, 5)
    scale = ATT_DH ** -0.5

    def block(qi):
        s = jnp.einsum('bqhid,bkhid->bhiqk', qi, k).astype(jnp.float32) * scale
        p = jax.nn.softmax(s, axis=-1)
        w = (p[:, :, 0] - lam * p[:, :, 1]).astype(v.dtype)
        return jnp.einsum('bhqk,bkhv->bqhv', w, v)

    o = lax.map(block, qb)
    return o.transpose(1, 0, 2, 3, 4).reshape(B, T, ATT_HEADS, ATT_DV)


def spatial_gating(u, v, ln_g, ln_b, w_s, b_s):
    B, T, _ = v.shape
    n = T // GM_CHUNK
    dg = GM_WIDTH // GM_GROUPS
    vn = layernorm(v, ln_g, ln_b).reshape(B, n, GM_CHUNK, GM_GROUPS, dg)
    s = jnp.einsum('gts,bnsgc->bntgc', w_s, vn) + b_s.T[None, None, :, :, None]
    return u * s.reshape(B, T, GM_WIDTH)


def hgrn2_gates(z, lb):
    B, T, _ = z.shape
    zf = z.astype(jnp.float32)
    logf = jnp.logaddexp(jnp.log(lb), jnp.log1p(-lb) + jax.nn.log_sigmoid(zf))
    k = ((1 - lb) * jax.nn.sigmoid(-zf)).astype(z.dtype)
    return logf.reshape(B, T, HG_HEADS, HG_DK), k.reshape(B, T, HG_HEADS, HG_DK)


def hgrn2_scan(k, v, logf, s0, q=None):
    B, T = k.shape[0], k.shape[1]
    nc = T // HG_CHUNK
    with_output = q is not None

    def to_chunks(a):
        return a.reshape(B, nc, HG_CHUNK, a.shape[2], a.shape[3]).transpose(1, 0, 3, 2, 4)

    tri = jnp.tril(jnp.ones((HG_CHUNK, HG_CHUNK), dtype=bool))
    xs = (to_chunks(k), to_chunks(v), to_chunks(logf)) + ((to_chunks(q),) if with_output else ())

    def step(S, inp):
        k_, v_, lf = inp[0], inp[1], inp[2]
        A = jnp.cumsum(lf, axis=2)
        A_last = A[:, :, -1:, :]
        S_new = jnp.exp(A_last[:, :, 0, :])[..., None] * S + jnp.einsum(
            'bhsk,bhsv->bhkv', k_ * jnp.exp(A_last - A), v_)
        if not with_output:
            return S_new, None
        q_ = inp[3]
        o_inter = jnp.einsum('bhtk,bhkv->bhtv', q_ * jnp.exp(A), S)
        diff = A[:, :, :, None, :] - A[:, :, None, :, :]
        decay = jnp.exp(jnp.where(tri[None, None, :, :, None], diff, -jnp.inf))
        scores = jnp.einsum('bhtk,bhsk,bhtsk->bhts', q_, k_, decay)
        return S_new, o_inter + jnp.einsum('bhts,bhsv->bhtv', scores, v_)

    S, o = lax.scan(step, s0, xs)
    if with_output:
        o = o.transpose(1, 0, 3, 2, 4).reshape(B, T, HG_HEADS, HG_DV)
    return S, o


def flip(a):
    return jnp.flip(a, axis=1)


def branch_merge(y_att, y_gm, y_hg, gates, w_br_att, w_br_gm, w_br_hg, w_out):
    g_att, g_gm, g_hg = jnp.split(jax.nn.sigmoid(gates), N_BRANCH, axis=-1)
    y = g_att * (y_att @ w_br_att) + g_gm * (y_gm @ w_br_gm) + g_hg * (y_hg @ w_br_hg)
    return y @ w_out


def token_mixers(h, hc, rows, cols, lam_init, lb, with_ctx_out, w_in, lam_q1, lam_k1, lam_q2, lam_k2,
                 att_subln_g, gm_ln_g, gm_ln_b, gm_ws, gm_bs, hg_norm_g, w_br_att, w_br_gm, w_br_hg, w_out):
    B, T, _ = h.shape
    L = hc.shape[1]
    aq, ak, av, gu, gv, hq, hff, hfb, hi, hg, gates = split_cols(h @ w_in)
    caq, cak, cav, cgu, cgv, chq, chff, chfb, chi, chg, cgates = split_cols(hc @ w_in)

    lam = (jnp.exp(jnp.sum(lam_q1.astype(jnp.float32) * lam_k1.astype(jnp.float32)))
           - jnp.exp(jnp.sum(lam_q2.astype(jnp.float32) * lam_k2.astype(jnp.float32))) + lam_init)
    q = axial_rope(aq.reshape(B, T, ATT_HEADS, 2, ATT_DH), rows, cols)
    k = axial_rope(ak.reshape(B, T, ATT_HEADS, 2, ATT_DH), rows, cols)
    kc = cak.reshape(B, L, ATT_HEADS, 2, ATT_DH)
    vc = cav.reshape(B, L, ATT_HEADS, ATT_DV)
    keys = jnp.concatenate([k, kc], axis=1)
    vals = jnp.concatenate([av.reshape(B, T, ATT_HEADS, ATT_DV), vc], axis=1)

    def att_out(o):
        return (rmsnorm(o, att_subln_g) * (1 - lam_init)).reshape(o.shape[0], o.shape[1], ATT_VW)

    y_att = att_out(diff_attention(q, keys, vals, lam))

    y_gm = spatial_gating(jax.nn.gelu(gu), jax.nn.gelu(gv), gm_ln_g, gm_ln_b, gm_ws, gm_bs)

    lf_f, k_f = hgrn2_gates(hff, lb[0])
    lf_b, k_b = hgrn2_gates(hfb, lb[1])
    clf_f, ck_f = hgrn2_gates(chff, lb[0])
    clf_b, ck_b = hgrn2_gates(chfb, lb[1])
    qh = jax.nn.silu(hq).reshape(B, T, HG_HEADS, HG_DK)
    iv = hi.reshape(B, T, HG_HEADS, HG_DV)
    civ = chi.reshape(B, L, HG_HEADS, HG_DV)
    s0 = jnp.zeros((B, HG_HEADS, HG_DK, HG_DV), jnp.float32)
    cq = jax.nn.silu(chq).reshape(B, L, HG_HEADS, HG_DK) if with_ctx_out else None
    s_f, oc_f = hgrn2_scan(ck_f, civ, clf_f, s0, q=cq)
    s_b, oc_b = hgrn2_scan(flip(ck_b), flip(civ), flip(clf_b), s0, q=None if cq is None else flip(cq))
    _, o_f = hgrn2_scan(k_f, iv, lf_f, s_f, q=qh)
    _, o_b = hgrn2_scan(flip(k_b), flip(iv), flip(lf_b), s_b, q=flip(qh))

    def hg_out(o, g):
        Bo, To = o.shape[0], o.shape[1]
        o = rmsnorm(o.astype(h.dtype), hg_norm_g) * jax.nn.silu(g.reshape(Bo, To, HG_HEADS, HG_DV))
        return o.reshape(Bo, To, HG_VW)

    y_hg = hg_out(o_f + flip(o_b), hg)
    y = branch_merge(y_att, y_gm, y_hg, gates, w_br_att, w_br_gm, w_br_hg, w_out)

    if not with_ctx_out:
        return y, None
    yc_att = att_out(diff_attention(cak.reshape(B, L, ATT_HEADS, 2, ATT_DH) * 0 + caq.reshape(B, L, ATT_HEADS, 2, ATT_DH), kc, vc, lam))
    yc_gm = spatial_gating(jax.nn.gelu(cgu), jax.nn.gelu(cgv), gm_ln_g, gm_ln_b, gm_ws, gm_bs)
    yc_hg = hg_out(oc_f + flip(oc_b), chg)
    yc = branch_merge(yc_att, yc_gm, yc_hg, cgates, w_br_att, w_br_gm, w_br_hg, w_out)
    return y, yc


def conv_ffn(h, w_up, conv_w, conv_b, w_down):
    T = h.shape[1]
    u = h @ w_up
    pad = CONV_W // 2
    up = jnp.pad(u, ((0, 0), (pad, pad), (0, 0)))
    u = sum(up[:, j:j + T] * conv_w[j] for j in range(CONV_W)) + conv_b
    a, b = jnp.split(u, 2, axis=-1)
    return (jax.nn.silu(a) * b) @ w_down


def setup_inputs(seed: int = 0) -> dict:
    key = jax.random.key(seed)
    ks = jax.random.split(key, 32)
    D = D_MODEL

    def nrm(k, shape, s):
        return jax.random.normal(k, shape, jnp.float32) * s

    return {
        "x": nrm(ks[0], (BATCH, SEQ, D), 1.0),
        "c": nrm(ks[1], (BATCH, D), 1.0),
        "ctx": nrm(ks[2], (BATCH, CTX_LEN, D), 1.0),
        "c_ctx": nrm(ks[3], (D,), 1.0),
        "w_ada": nrm(ks[4], (DEPTH, D, 6 * D), 0.5 * D ** -0.5),
        "b_ada": nrm(ks[5], (DEPTH, 6 * D), 0.01),
        "g_pre_mix": 1.0 + nrm(ks[6], (DEPTH, D), 0.02),
        "g_post_mix": 1.0 + nrm(ks[7], (DEPTH, D), 0.02),
        "g_pre_ffn": 1.0 + nrm(ks[8], (DEPTH, D), 0.02),
        "g_post_ffn": 1.0 + nrm(ks[9], (DEPTH, D), 0.02),
        "w_in": nrm(ks[10], (DEPTH, D, IN_W), D ** -0.5),
        "lam_q1": nrm(ks[11], (DEPTH, ATT_DH), 0.1),
        "lam_k1": nrm(ks[12], (DEPTH, ATT_DH), 0.1),
        "lam_q2": nrm(ks[13], (DEPTH, ATT_DH), 0.1),
        "lam_k2": nrm(ks[14], (DEPTH, ATT_DH), 0.1),
        "att_subln_g": 1.0 + nrm(ks[15], (DEPTH, ATT_DV), 0.02),
        "gm_ln_g": 1.0 + nrm(ks[16], (DEPTH, GM_WIDTH), 0.02),
        "gm_ln_b": nrm(ks[17], (DEPTH, GM_WIDTH), 0.01),
        "gm_ws": nrm(ks[18], (DEPTH, GM_GROUPS, GM_CHUNK, GM_CHUNK), GM_CHUNK ** -0.5),
        "gm_bs": 1.0 + nrm(ks[19], (DEPTH, GM_GROUPS, GM_CHUNK), 0.01),
        "hg_lb": nrm(ks[20], (DEPTH, 2, HG_KW), 0.5),
        "hg_norm_g": 1.0 + nrm(ks[21], (DEPTH, HG_DV), 0.02),
        "w_br_att": nrm(ks[22], (DEPTH, ATT_VW, D), ATT_VW ** -0.5),
        "w_br_gm": nrm(ks[23], (DEPTH, GM_WIDTH, D), GM_WIDTH ** -0.5),
        "w_br_hg": nrm(ks[24], (DEPTH, HG_VW, D), HG_VW ** -0.5),
        "w_out": nrm(ks[25], (DEPTH, D, D), D ** -0.5),
        "w_up": nrm(ks[26], (DEPTH, D, 2 * D_FF), D ** -0.5),
        "conv_w": nrm(ks[27], (DEPTH, CONV_W, 2 * D_FF), 0.5),
        "conv_b": nrm(ks[28], (DEPTH, 2 * D_FF), 0.01),
        "w_down": nrm(ks[29], (DEPTH, D_FF, D), D_FF ** -0.5),
    }


def reference(x, c, ctx, c_ctx, w_ada, b_ada, g_pre_mix, g_post_mix, g_pre_ffn, g_post_ffn, w_in,
              lam_q1, lam_k1, lam_q2, lam_k2, att_subln_g, gm_ln_g, gm_ln_b, gm_ws, gm_bs, hg_lb,
              hg_norm_g, w_br_att, w_br_gm, w_br_hg, w_out, w_up, conv_w, conv_b, w_down):
    T = x.shape[1]
    ROWS = T // GRID_W
    rows = jnp.repeat(jnp.arange(ROWS, dtype=jnp.int32), GRID_W)
    cols = jnp.tile(jnp.arange(GRID_W, dtype=jnp.int32), ROWS)
    lb_all = jnp.cumsum(jax.nn.softmax(hg_lb.astype(jnp.float32), axis=0), axis=0)
    lb_all = lb_all - lb_all[0]
    xc = ctx
    for l in range(DEPTH):
        last = l == DEPTH - 1
        lam_init = 0.8 - 0.6 * math.exp(-0.3 * l)
        mod = (jax.nn.silu(c) @ w_ada[l] + b_ada[l])[:, None, :]
        modc = (jax.nn.silu(c_ctx) @ w_ada[l] + b_ada[l])[None, None, :]
        sh1, sc1, gt1, sh2, sc2, gt2 = jnp.split(mod, 6, axis=-1)
        csh1, csc1, cgt1, csh2, csc2, cgt2 = jnp.split(modc, 6, axis=-1)

        h = modulate(rmsnorm(x, g_pre_mix[l]), sh1, sc1)
        hc = modulate(rmsnorm(xc, g_pre_mix[l]), csh1, csc1)
        y, yc = token_mixers(h, hc, rows, cols, lam_init, lb_all[l], not last, w_in[l],
                             lam_q1[l], lam_k1[l], lam_q2[l], lam_k2[l], att_subln_g[l],
                             gm_ln_g[l], gm_ln_b[l], gm_ws[l], gm_bs[l], hg_norm_g[l],
                             w_br_att[l], w_br_gm[l], w_br_hg[l], w_out[l])
        x = x + gt1 * rmsnorm(y, g_post_mix[l])
        h = modulate(rmsnorm(x, g_pre_ffn[l]), sh2, sc2)
        x = x + gt2 * rmsnorm(conv_ffn(h, w_up[l], conv_w[l], conv_b[l], w_down[l]), g_post_ffn[l])
        if not last:
            xc = xc + cgt1 * rmsnorm(yc, g_post_mix[l])
            hc = modulate(rmsnorm(xc, g_pre_ffn[l]), csh2, csc2)
            xc = xc + cgt2 * rmsnorm(conv_ffn(hc, w_up[l], conv_w[l], conv_b[l], w_down[l]), g_post_ffn[l])
    return x
```

```python
import functools
import math

import jax
import jax.numpy as jnp
from jax import lax
from jax.experimental import pallas as pl
from jax.experimental.pallas import tpu as pltpu

F32 = jnp.float32
BF16 = jnp.bfloat16

D_MODEL = 1024
DEPTH = 2
GRID_W = 64
ATT_HEADS = 8
ATT_DH = 64
ATT_DV = 2 * ATT_DH
GM_WIDTH = 1024
GM_GROUPS = 8
GM_CHUNK = 128
HG_HEADS = 8
HG_DK = 128
HG_DV = 128
HG_CHUNK = 64
HG_SUB = 8
N_BRANCH = 3
D_FF = 2816
FF_CHUNK = 256
CONV_W = 3
ROPE_BASE = 10000.0
EPS = 1e-6

ATT_QW = ATT_HEADS * 2 * ATT_DH
ATT_VW = ATT_HEADS * ATT_DV
HG_KW = HG_HEADS * HG_DK
HG_VW = HG_HEADS * HG_DV
IN_SPLITS = (ATT_QW, ATT_QW, ATT_VW, GM_WIDTH, GM_WIDTH, HG_KW, HG_KW, HG_KW, HG_VW, HG_VW,
             N_BRANCH * D_MODEL)

LANES = 128
BF16_ROWS = 16
MXU_N = 256
VMEM_LIMIT = 56 * 1024 * 1024

NEG_INF = float("-inf")


def _params(*sem):
    return pltpu.CompilerParams(dimension_semantics=sem, vmem_limit_bytes=VMEM_LIMIT)


def _resident(shape):
    nd = len(shape)
    return pl.BlockSpec(shape, lambda *_: (0,) * nd, pipeline_mode=pl.Buffered(1))


def _dot(a, b):
    return jnp.dot(a, b, preferred_element_type=F32)


def _dot_nt(a, b):
    return lax.dot_general(a, b, (((1,), (1,)), ((), ())), preferred_element_type=F32)


def _rms(x, g):
    return x * lax.rsqrt(jnp.mean(x * x, axis=-1, keepdims=True) + EPS) * g


def _silu(x):
    return x * jax.nn.sigmoid(x)


def _gelu_tanh(x):
    return 0.5 * x * (1.0 + jnp.tanh(math.sqrt(2.0 / math.pi) * (x + 0.044715 * (x * x * x))))


def _ada_kernel(c_ref, w_ref, b_ref, o_ref):
    c = c_ref[...]
    o_ref[0] = _dot(_silu(c).astype(BF16), w_ref[0].astype(BF16)) + b_ref[0]


def _ada(c_all, w_ada, b_ada):
    rows = c_all.shape[0]
    n = w_ada.shape[-1]
    tn = 1536
    return pl.pallas_call(
        _ada_kernel,
        grid=(DEPTH, n // tn),
        in_specs=[
            pl.BlockSpec((rows, D_MODEL), lambda l, j: (0, 0)),
            pl.BlockSpec((1, D_MODEL, tn), lambda l, j: (l, 0, j)),
            pl.BlockSpec((1, 1, tn), lambda l, j: (l, 0, j)),
        ],
        out_specs=pl.BlockSpec((1, rows, tn), lambda l, j: (l, 0, j)),
        out_shape=jax.ShapeDtypeStruct((DEPTH, rows, n), F32),
        compiler_params=_params("parallel", "parallel"),
        name="ada",
    )(c_all, w_ada, b_ada.reshape(DEPTH, 1, n))


def _prenorm_kernel(x_ref, g_ref, sh_ref, sc_ref, h_ref):
    y = _rms(x_ref[0], g_ref[...])
    h_ref[0] = (y * (1.0 + sc_ref[0]) + sh_ref[0]).astype(BF16)


def _prenorm(x, g, sh, sc, tm):
    B, T, D = x.shape
    row = pl.BlockSpec((1, tm, D), lambda b, i: (b, i, 0))
    vec = pl.BlockSpec((1, 1, D), lambda b, i: (b, 0, 0))
    return pl.pallas_call(
        _prenorm_kernel,
        grid=(B, T // tm),
        in_specs=[row, pl.BlockSpec((1, D), lambda b, i: (0, 0)), vec, vec],
        out_specs=row,
        out_shape=jax.ShapeDtypeStruct((B, T, D), BF16),
        compiler_params=_params("parallel", "parallel"),
        name="prenorm",
    )(x, g.reshape(1, D), sh, sc)


def _att_proj_kernel(h_ref, w_ref, *rest, rope):
    if rope:
        cos_ref, sa_ref, sb_ref, q_ref, k_ref, v_ref = rest
        cos, sa, sb = cos_ref[...], sa_ref[...], sb_ref[...]
    else:
        q_ref, k_ref, v_ref = rest
    h = h_ref[0]

    def rot(a):
        half = ATT_DH // 4
        return a * cos + pltpu.roll(a, LANES - half, 1) * sa + pltpu.roll(a, half, 1) * sb

    for j in range(ATT_QW // MXU_N):
        cols = slice(j * MXU_N, (j + 1) * MXU_N)
        aq = _dot(h, w_ref[:, j * MXU_N:(j + 1) * MXU_N])
        ak = _dot(h, w_ref[:, ATT_QW + j * MXU_N:ATT_QW + (j + 1) * MXU_N])
        av = _dot(h, w_ref[:, 2 * ATT_QW + j * MXU_N:2 * ATT_QW + (j + 1) * MXU_N])
        if rope:
            aq = jnp.concatenate([rot(aq[:, :LANES]), rot(aq[:, LANES:])], axis=1)
            ak = jnp.concatenate([rot(ak[:, :LANES]), rot(ak[:, LANES:])], axis=1)
        q_ref[0, :, cols] = (aq * (ATT_DH ** -0.5)).astype(BF16)
        k_ref[0, :, cols] = ak.astype(BF16)
        v_ref[0, :, cols] = av.astype(BF16)


def _att_proj(h, w_qkv, tables, tm):
    B, T, D = h.shape
    rope = tables is not None
    row = pl.BlockSpec((1, tm, D), lambda b, i: (b, i, 0))
    out = pl.BlockSpec((1, tm, ATT_QW), lambda b, i: (b, i, 0))
    in_specs = [row, _resident(w_qkv.shape)]
    args = [h, w_qkv]
    if rope:
        in_specs += [pl.BlockSpec((tm, LANES), lambda b, i: (i, 0))] * 3
        args += list(tables)
    return pl.pallas_call(
        functools.partial(_att_proj_kernel, rope=rope),
        grid=(B, T // tm),
        in_specs=in_specs,
        out_specs=[out, out, out],
        out_shape=[jax.ShapeDtypeStruct((B, T, ATT_QW), BF16)] * 3,
        compiler_params=_params("parallel", "parallel"),
        name="att_proj",
    )(*args)


def _attn_kernel(lam_ref, q_ref, g_ref, *rest, n_kv, post_scale):
    kv = rest[:2 * n_kv]
    o_ref = rest[2 * n_kv]
    q = q_ref[0]
    lane = lax.broadcasted_iota(jnp.int32, q.shape, 1)
    zero = jnp.zeros_like(q)
    outs = []
    for i in range(2):
        qi = jnp.where(lane < ATT_DH, q, zero) if i == 0 else jnp.where(lane >= ATT_DH, q, zero)
        ss = [_dot_nt(qi, kv[2 * j][0]) for j in range(n_kv)]
        m = functools.reduce(jnp.maximum, [jnp.max(s, axis=-1, keepdims=True) for s in ss])
        ps = [jnp.exp(s - m) for s in ss]
        l = functools.reduce(jnp.add, [jnp.sum(p, axis=-1, keepdims=True) for p in ps])
        o = functools.reduce(jnp.add, [_dot(ps[j].astype(BF16), kv[2 * j + 1][0]) for j in range(n_kv)])
        outs.append(o / l)
    o = outs[0] - lam_ref[0] * outs[1]
    o_ref[0] = (_rms(o, g_ref[...]) * post_scale).astype(BF16)


def _attention(q, kvs, lam, g, post_scale, tq):
    B, T, _ = q.shape
    in_specs = [
        pl.BlockSpec(memory_space=pltpu.SMEM),
        pl.BlockSpec((1, tq, LANES), lambda b, h, i: (b, i, h)),
        pl.BlockSpec((1, LANES), lambda b, h, i: (0, 0)),
    ]
    args = [lam.reshape(1), q, g.reshape(1, ATT_DV)]
    for k, v in kvs:
        spec = pl.BlockSpec((1, k.shape[1], LANES), lambda b, h, i: (b, 0, h))
        in_specs += [spec, spec]
        args += [k, v]
    return pl.pallas_call(
        functools.partial(_attn_kernel, n_kv=len(kvs), post_scale=post_scale),
        grid=(B, ATT_HEADS, T // tq),
        in_specs=in_specs,
        out_specs=pl.BlockSpec((1, tq, LANES), lambda b, h, i: (b, i, h)),
        out_shape=jax.ShapeDtypeStruct((B, T, ATT_VW), BF16),
        compiler_params=_params("parallel", "parallel", "parallel"),
        name="attention",
    )(*args)


def _gmlp_kernel(h_ref, w_ref, lng_ref, lnb_ref, ws_ref, bias_ref, o_ref, *, tm):
    h = h_ref[0]
    u = _gelu_tanh(_dot(h, w_ref[:, :GM_WIDTH]))
    v = _gelu_tanh(_dot(h, w_ref[:, GM_WIDTH:]))
    mu = jnp.mean(v, axis=-1, keepdims=True)
    vc = v - mu
    var = jnp.mean(vc * vc, axis=-1, keepdims=True)
    vn = (vc * lax.rsqrt(var + EPS) * lng_ref[...] + lnb_ref[...]).astype(BF16)
    dg = GM_WIDTH // GM_GROUPS
    for r in range(tm // GM_CHUNK):
        rows = slice(r * GM_CHUNK, (r + 1) * GM_CHUNK)
        for g in range(GM_GROUPS):
            cols = slice(g * dg, (g + 1) * dg)
            s = _dot(ws_ref[g], vn[rows, cols]) + bias_ref[:, cols]
            o_ref[0, rows, cols] = (u[rows, cols] * s).astype(BF16)


def _gmlp(h, w_uv, ln_g, ln_b, ws, bias_full, tm):
    B, T, D = h.shape
    return pl.pallas_call(
        functools.partial(_gmlp_kernel, tm=tm),
        grid=(B, T // tm),
        in_specs=[
            pl.BlockSpec((1, tm, D), lambda b, i: (b, i, 0)),
            _resident(w_uv.shape),
            _resident((1, GM_WIDTH)),
            _resident((1, GM_WIDTH)),
            _resident(ws.shape),
            _resident(bias_full.shape),
        ],
        out_specs=pl.BlockSpec((1, tm, GM_WIDTH), lambda b, i: (b, i, 0)),
        out_shape=jax.ShapeDtypeStruct((B, T, GM_WIDTH), BF16),
        compiler_params=_params("parallel", "parallel"),
        name="gmlp",
    )(h, w_uv, ln_g.reshape(1, GM_WIDTH), ln_b.reshape(1, GM_WIDTH), ws, bias_full)


def _hgrn_proj_kernel(h_ref, w_ref, lb_ref, q_ref, kf_ref, lff_ref, kb_ref, lfb_ref, v_ref, g_ref):
    h = h_ref[0]

    def gates(z, d, cols):
        log_lb = lb_ref[3 * d + 0:3 * d + 1, cols]
        log_1m = lb_ref[3 * d + 1:3 * d + 2, cols]
        one_m = lb_ref[3 * d + 2:3 * d + 3, cols]
        log_sig = jnp.minimum(z, 0.0) - jnp.log1p(jnp.exp(-jnp.abs(z)))
        b = log_1m + log_sig
        logf = jnp.maximum(log_lb, b) + jnp.log1p(jnp.exp(-jnp.abs(log_lb - b)))
        return logf, one_m * jax.nn.sigmoid(-z)

    for j in range(HG_KW // MXU_N):
        cols = slice(j * MXU_N, (j + 1) * MXU_N)

        def proj(seg):
            return _dot(h, w_ref[:, seg * HG_KW + j * MXU_N:seg * HG_KW + (j + 1) * MXU_N])

        q_ref[0, :, cols] = _silu(proj(0)).astype(BF16)
        lf, k = gates(proj(1), 0, cols)
        lff_ref[0, :, cols] = lf
        kf_ref[0, :, cols] = k.astype(BF16)
        lf, k = gates(proj(2), 1, cols)
        lfb_ref[0, :, cols] = lf
        kb_ref[0, :, cols] = k.astype(BF16)
        v_ref[0, :, cols] = proj(3).astype(BF16)
        g_ref[0, :, cols] = _silu(proj(4)).astype(BF16)


def _hgrn_proj(h, w_hg, lb_tab, tm):
    B, T, D = h.shape
    out = pl.BlockSpec((1, tm, HG_KW), lambda b, i: (b, i, 0))
    bf = jax.ShapeDtypeStruct((B, T, HG_KW), BF16)
    f32 = jax.ShapeDtypeStruct((B, T, HG_KW), F32)
    return pl.pallas_call(
        _hgrn_proj_kernel,
        grid=(B, T // tm),
        in_specs=[pl.BlockSpec((1, tm, D), lambda b, i: (b, i, 0)), _resident(w_hg.shape),
                  _resident(lb_tab.shape)],
        out_specs=[out] * 7,
        out_shape=[bf, bf, f32, bf, f32, bf, bf],
        compiler_params=_params("parallel", "parallel"),
        name="hgrn_proj",
    )(h, w_hg, lb_tab)


def _split3(x):
    a = x.astype(BF16)
    r = x - a.astype(F32)
    b = r.astype(BF16)
    c = (r - b.astype(F32)).astype(BF16)
    return a, b, c


def _hgrn_chunk(q, k, lf, v, st, rev, consts):
    C = HG_CHUNK
    tri, row_id, col_id, r_in = consts
    qf = q.astype(F32)
    kf = k.astype(F32)
    a1, a2, a3 = _split3(lf)
    A = _dot(tri, a1) + _dot(tri, a2) + _dot(tri, a3)
    a_tot = A[0:1] if rev else A[C - 1:C]

    o = _dot_nt((qf * jnp.exp(A)).astype(BF16), st.astype(BF16))
    k_end = (kf * jnp.exp(a_tot - A)).astype(BF16)
    st_new = st * jnp.exp(a_tot) + _dot(v.astype(F32).T.astype(BF16), k_end)

    scores = jnp.zeros((C, C), F32)
    seg = 2 * HG_SUB
    while seg <= C:
        half = seg // 2
        pieces = []
        for s0 in range(0, C, seg):
            ref_row = s0 + half if rev else s0 + half - 1
            pieces.append(jnp.broadcast_to(A[ref_row:ref_row + 1], (seg, HG_DK)))
        ref = pieces[0] if len(pieces) == 1 else jnp.concatenate(pieces, axis=0)
        upper = (r_in & (seg - 1)) >= half
        q_side = (r_in & (seg - 1)) < half if rev else upper
        qm = (qf * jnp.exp(jnp.where(q_side, A - ref, NEG_INF))).astype(BF16)
        km = (kf * jnp.exp(jnp.where(q_side, NEG_INF, ref - A))).astype(BF16)
        p = _dot_nt(qm, km)
        if seg < C:
            p = jnp.where((row_id & -seg) == (col_id & -seg), p, 0.0)
        scores = scores + p
        seg *= 2

    nsub = C // HG_SUB
    A3 = A.reshape(nsub, HG_SUB, HG_DK)
    k3 = kf.reshape(nsub, HG_SUB, HG_DK)
    t_in = r_in & (HG_SUB - 1)
    sub0 = row_id & -HG_SUB
    for s in range(HG_SUB):
        a_s = jnp.broadcast_to(A3[:, s:s + 1, :], A3.shape).reshape(C, HG_DK)
        k_s = jnp.broadcast_to(k3[:, s:s + 1, :], k3.shape).reshape(C, HG_DK)
        live = (t_in <= s) if rev else (t_in >= s)
        x = qf * k_s * jnp.exp(jnp.where(live, A - a_s, NEG_INF))
        col = jnp.sum(x, axis=-1, keepdims=True)
        scores = jnp.where(col_id == sub0 + s, col, scores)
    o = o + _dot(scores.astype(BF16), v)
    return o, st_new


def _hgrn_scan_kernel(g_ref, q_ref, kf_ref, lff_ref, kb_ref, lfb_ref, v_ref, og_ref,
                      cq_ref, ckf_ref, clff_ref, ckb_ref, clfb_ref, cv_ref, cog_ref,
                      y_ref, yc_ref, of_ref, ob_ref, cof_ref, cob_ref, st_ref, *, T, L):
    C = HG_CHUNK
    row_id = lax.broadcasted_iota(jnp.int32, (C, C), 0)
    col_id = lax.broadcasted_iota(jnp.int32, (C, C), 1)
    tri_f = (col_id <= row_id).astype(BF16)
    tri_b = (col_id >= row_id).astype(BF16)
    r_in = lax.broadcasted_iota(jnp.int32, (C, HG_DK), 0)
    consts_f = (tri_f, row_id, col_id, r_in)
    consts_b = (tri_b, row_id, col_id, r_in)

    st_ref[...] = jnp.zeros_like(st_ref)

    def scan(q_r, kf_r, lff_r, kb_r, lfb_r, v_r, of_r, ob_r, n):
        def body(c, carry):
            fs = pl.ds(pl.multiple_of(c * C, C), C)
            o, st = _hgrn_chunk(q_r[0, fs, :], kf_r[0, fs, :], lff_r[0, fs, :], v_r[0, fs, :],
                                st_ref[0], False, consts_f)
            of_r[fs, :] = o
            st_ref[0] = st
            bs = pl.ds(pl.multiple_of((n - 1 - c) * C, C), C)
            o, st = _hgrn_chunk(q_r[0, bs, :], kb_r[0, bs, :], lfb_r[0, bs, :], v_r[0, bs, :],
                                st_ref[1], True, consts_b)
            ob_r[bs, :] = o
            st_ref[1] = st
            return carry

        lax.fori_loop(0, n, body, 0)

    def finish(of_r, ob_r, og_r, y_r):
        y_r[0] = (_rms(of_r[...] + ob_r[...], g_ref[...]) * og_r[0].astype(F32)).astype(BF16)

    scan(cq_ref, ckf_ref, clff_ref, ckb_ref, clfb_ref, cv_ref, cof_ref, cob_ref, L // C)
    finish(cof_ref, cob_ref, cog_ref, yc_ref)
    scan(q_ref, kf_ref, lff_ref, kb_ref, lfb_ref, v_ref, of_ref, ob_ref, T // C)
    finish(of_ref, ob_ref, og_ref, y_ref)


def _hgrn_scan(lat, ctx, g):
    B, T, _ = lat[0].shape
    L = ctx[0].shape[1]
    lat_spec = pl.BlockSpec((1, T, LANES), lambda b, h: (b, 0, h))
    ctx_spec = pl.BlockSpec((1, L, LANES), lambda b, h: (b, 0, h))
    return pl.pallas_call(
        functools.partial(_hgrn_scan_kernel, T=T, L=L),
        grid=(B, HG_HEADS),
        in_specs=[pl.BlockSpec((1, HG_DV), lambda b, h: (0, 0))] + [lat_spec] * 7 + [ctx_spec] * 7,
        out_specs=[lat_spec, ctx_spec],
        out_shape=[jax.ShapeDtypeStruct((B, T, HG_VW), BF16), jax.ShapeDtypeStruct((B, L, HG_VW), BF16)],
        scratch_shapes=[
            pltpu.VMEM((T, HG_DV), F32), pltpu.VMEM((T, HG_DV), F32),
            pltpu.VMEM((L, HG_DV), F32), pltpu.VMEM((L, HG_DV), F32),
            pltpu.VMEM((2, HG_DV, HG_DK), F32),
        ],
        compiler_params=_params("parallel", "parallel"),
        name="hgrn_scan",
    )(g.reshape(1, HG_DV), *lat, *ctx)


def _merge_kernel(x_ref, h_ref, ya_ref, yg_ref, yh_ref, wgate_ref, wa_ref, wg_ref, wh_ref, wo_ref,
                  gpost_ref, gt_ref, gpre_ref, sh_ref, sc_ref, x1_ref, h2_ref):
    h = h_ref[0]
    D = D_MODEL
    y = jax.nn.sigmoid(_dot(h, wgate_ref[:, :D])) * _dot(ya_ref[0], wa_ref[...])
    y = y + jax.nn.sigmoid(_dot(h, wgate_ref[:, D:2 * D])) * _dot(yg_ref[0], wg_ref[...])
    y = y + jax.nn.sigmoid(_dot(h, wgate_ref[:, 2 * D:])) * _dot(yh_ref[0], wh_ref[...])
    z = _dot(y.astype(BF16), wo_ref[...])
    x1 = x_ref[0] + gt_ref[0] * _rms(z, gpost_ref[...])
    x1_ref[0] = x1
    h2_ref[0] = (_rms(x1, gpre_ref[...]) * (1.0 + sc_ref[0]) + sh_ref[0]).astype(BF16)


def _merge(x, h, ya, yg, yh, wgate, wa, wg, wh, wo, g_post, gt, g_pre, sh, sc, tm):
    B, T, D = x.shape
    row = pl.BlockSpec((1, tm, D), lambda b, i: (b, i, 0))
    vec = pl.BlockSpec((1, 1, D), lambda b, i: (b, 0, 0))
    par = pl.BlockSpec((1, D), lambda b, i: (0, 0))
    return pl.pallas_call(
        _merge_kernel,
        grid=(B, T // tm),
        in_specs=[row] * 5 + [_resident(wgate.shape)] + [_resident(wa.shape)] * 4 + [par, vec, par, vec, vec],
        out_specs=[row, row],
        out_shape=[jax.ShapeDtypeStruct((B, T, D), F32), jax.ShapeDtypeStruct((B, T, D), BF16)],
        compiler_params=_params("parallel", "parallel"),
        name="merge",
    )(x, h, ya, yg, yh, wgate, wa, wg, wh, wo, g_post.reshape(1, D), gt, g_pre.reshape(1, D), sh, sc)


def _ffn_kernel(x_ref, h_ref, hp_ref, hn_ref, wup_ref, cw_ref, cb_ref, wdn_ref, gpost_ref, gt_ref,
                o_ref, acc_ref, *, tm):
    i = pl.program_id(1)
    last = pl.num_programs(1) - 1
    halo = BF16_ROWS
    hp = jnp.where(i > 0, hp_ref[0], jnp.zeros_like(hp_ref[0]))
    hn = jnp.where(i < last, hn_ref[0], jnp.zeros_like(hn_ref[0]))
    h_ext = jnp.concatenate([hp, h_ref[0], hn], axis=0)
    rows = tm + 2 * halo
    acc_ref[...] = jnp.zeros_like(acc_ref)

    def conv(u, part, n):
        cw = cw_ref[part, n]
        u_prev = pltpu.roll(u, 1, 0)
        u_next = pltpu.roll(u, rows - 1, 0)
        return (u_prev * cw[0:1] + u * cw[1:2] + u_next * cw[2:3] + cb_ref[part, n])[halo:halo + tm]

    def body(n, carry):
        a = conv(_dot(h_ext, wup_ref[0, n]), 0, n)
        b = conv(_dot(h_ext, wup_ref[1, n]), 1, n)
        acc_ref[...] += _dot((_silu(a) * b).astype(BF16), wdn_ref[n])
        return carry

    lax.fori_loop(0, D_FF // FF_CHUNK, body, 0)
    o_ref[0] = x_ref[0] + gt_ref[0] * _rms(acc_ref[...], gpost_ref[...])


def _ffn(x, h, wup, cw, cb, wdn, g_post, gt, tm):
    B, T, D = x.shape
    nh = tm // BF16_ROWS
    row = pl.BlockSpec((1, tm, D), lambda b, i: (b, i, 0))
    prev = pl.BlockSpec((1, BF16_ROWS, D), lambda b, i: (b, jnp.maximum(i * nh - 1, 0), 0))
    nxt = pl.BlockSpec((1, BF16_ROWS, D), lambda b, i: (b, jnp.minimum((i + 1) * nh, T // BF16_ROWS - 1), 0))
    return pl.pallas_call(
        functools.partial(_ffn_kernel, tm=tm),
        grid=(B, T // tm),
        in_specs=[row, row, prev, nxt, _resident(wup.shape), _resident(cw.shape), _resident(cb.shape),
                  _resident(wdn.shape), pl.BlockSpec((1, D), lambda b, i: (0, 0)),
                  pl.BlockSpec((1, 1, D), lambda b, i: (b, 0, 0))],
        out_specs=row,
        out_shape=jax.ShapeDtypeStruct((B, T, D), F32),
        scratch_shapes=[pltpu.VMEM((tm, D), F32)],
        compiler_params=_params("parallel", "parallel"),
        name="ffn",
    )(x, h, h, h, wup, cw, cb, wdn, g_post.reshape(1, D), gt)


def _rope_tables(T):
    half = ATT_DH // 4
    t = jnp.arange(T, dtype=jnp.int32)
    rows = (t // GRID_W).astype(F32)
    cols = (t % GRID_W).astype(F32)
    inv = ROPE_BASE ** (-jnp.arange(half, dtype=F32) / half)
    ang_r = rows[:, None] * inv[None, :]
    ang_c = cols[:, None] * inv[None, :]
    z = jnp.zeros_like(ang_r)
    cos64 = jnp.concatenate([jnp.cos(ang_r)] * 2 + [jnp.cos(ang_c)] * 2, axis=1)
    sa64 = jnp.concatenate([-jnp.sin(ang_r), z, -jnp.sin(ang_c), z], axis=1)
    sb64 = jnp.concatenate([z, jnp.sin(ang_r), z, jnp.sin(ang_c)], axis=1)
    return tuple(jnp.concatenate([a, a], axis=1) for a in (cos64, sa64, sb64))


def _col_offsets():
    offs, start = [], 0
    for w in IN_SPLITS:
        offs.append(start)
        start += w
    return offs


def kernel(x, c, ctx, c_ctx, w_ada, b_ada, g_pre_mix, g_post_mix, g_pre_ffn, g_post_ffn, w_in, lam_q1, lam_k1, lam_q2, lam_k2, att_subln_g, gm_ln_g, gm_ln_b, gm_ws, gm_bs, hg_lb, hg_norm_g, w_br_att, w_br_gm, w_br_hg, w_out, w_up, conv_w, conv_b, w_down):
    B, T, D = x.shape
    L = ctx.shape[1]
    TM = 512
    TQ = 256

    rows = 16
    c_all = jnp.concatenate([c, c_ctx[None, :], jnp.zeros((rows - B - 1, D), F32)], axis=0)
    mod = _ada(c_all, w_ada, b_ada)

    lb_all = jnp.cumsum(jax.nn.softmax(hg_lb.astype(F32), axis=0), axis=0)
    lb_all = lb_all - lb_all[0]
    tables = _rope_tables(T)
    offs = _col_offsets()
    nff = D_FF // FF_CHUNK

    xc = ctx
    for l in range(DEPTH):
        last = l == DEPTH - 1
        lam_init = 0.8 - 0.6 * math.exp(-0.3 * l)
        lam = (jnp.exp(jnp.sum(lam_q1[l] * lam_k1[l])) - jnp.exp(jnp.sum(lam_q2[l] * lam_k2[l])) + lam_init)

        def mods(k, ctx_row):
            m = mod[l, :, k * D:(k + 1) * D]
            if ctx_row:
                return jnp.broadcast_to(m[B:B + 1], (B, D)).reshape(B, 1, D)
            return m[:B].reshape(B, 1, D)

        w_l = w_in[l].astype(BF16)
        w_qkv = w_l[:, offs[0]:offs[3]]
        w_uv = w_l[:, offs[3]:offs[5]]
        w_hg = w_l[:, offs[5]:offs[10]]
        w_gate = w_l[:, offs[10]:]
        lb = lb_all[l]
        lb_tab = jnp.stack([jnp.log(lb[0]), jnp.log1p(-lb[0]), 1.0 - lb[0],
                            jnp.log(lb[1]), jnp.log1p(-lb[1]), 1.0 - lb[1],
                            jnp.zeros_like(lb[0]), jnp.zeros_like(lb[0])], axis=0)
        ws = gm_ws[l].astype(BF16)
        dg = GM_WIDTH // GM_GROUPS
        bias_full = jnp.broadcast_to(gm_bs[l].T[:, :, None], (GM_CHUNK, GM_GROUPS, dg)).reshape(GM_CHUNK, GM_WIDTH)
        wa, wg, wh, wo = (w[l].astype(BF16) for w in (w_br_att, w_br_gm, w_br_hg, w_out))
        wup = w_up[l].astype(BF16).reshape(D, 2, nff, FF_CHUNK).transpose(1, 2, 0, 3)
        cw = conv_w[l].reshape(CONV_W, 2, nff, FF_CHUNK).transpose(1, 2, 0, 3)
        cb = conv_b[l].reshape(2, nff, 1, FF_CHUNK)
        wdn = w_down[l].astype(BF16).reshape(nff, FF_CHUNK, D)

        hc = _prenorm(xc, g_pre_mix[l], mods(0, True), mods(1, True), L)
        h = _prenorm(x, g_pre_mix[l], mods(0, False), mods(1, False), TM)
        cq, ck, cv = _att_proj(hc, w_qkv, None, L)
        q, k, v = _att_proj(h, w_qkv, tables, TM)
        c_hg = _hgrn_proj(hc, w_hg, lb_tab, L)
        l_hg = _hgrn_proj(h, w_hg, lb_tab, TM)

        y_att = _attention(q, [(k, v), (ck, cv)], lam, att_subln_g[l], 1.0 - lam_init, TQ)
        y_gm = _gmlp(h, w_uv, gm_ln_g[l], gm_ln_b[l], ws, bias_full, TM)
        y_hg, yc_hg = _hgrn_scan(l_hg, c_hg, hg_norm_g[l])

        x, h2 = _merge(x, h, y_att, y_gm, y_hg, w_gate, wa, wg, wh, wo, g_post_mix[l], mods(2, False),
                       g_pre_ffn[l], mods(3, False), mods(4, False), TM)
        x = _ffn(x, h2, wup, cw, cb, wdn, g_post_ffn[l], mods(5, False), TM)

        if not last:
            yc_att = _attention(cq, [(ck, cv)], lam, att_subln_g[l], 1.0 - lam_init, L)
            yc_gm = _gmlp(hc, w_uv, gm_ln_g[l], gm_ln_b[l], ws, bias_full, L)
            xc, hc2 = _merge(xc, hc, yc_att, yc_gm, yc_hg, w_gate, wa, wg, wh, wo, g_post_mix[l],
                             mods(2, True), g_pre_ffn[l], mods(3, True), mods(4, True), L)
            xc = _ffn(xc, hc2, wup, cw, cb, wdn, g_post_ffn[l], mods(5, True), L)
    return x
```

```python
import functools
import math

import jax
import jax.numpy as jnp
from jax import lax
from jax.experimental import pallas as pl
from jax.experimental.pallas import tpu as pltpu

F32 = jnp.float32
BF16 = jnp.bfloat16

D_MODEL = 1024
DEPTH = 2
GRID_W = 64
ATT_HEADS = 8
ATT_DH = 64
ATT_DV = 2 * ATT_DH
GM_WIDTH = 1024
GM_GROUPS = 8
GM_CHUNK = 128
HG_HEADS = 8
HG_DK = 128
HG_DV = 128
HG_CHUNK = 64
HG_SUB = 8
N_BRANCH = 3
D_FF = 2816
FF_CHUNK = 256
CONV_W = 3
ROPE_BASE = 10000.0
EPS = 1e-6

ATT_QW = ATT_HEADS * 2 * ATT_DH
ATT_VW = ATT_HEADS * ATT_DV
HG_KW = HG_HEADS * HG_DK
HG_VW = HG_HEADS * HG_DV
IN_SPLITS = (ATT_QW, ATT_QW, ATT_VW, GM_WIDTH, GM_WIDTH, HG_KW, HG_KW, HG_KW, HG_VW, HG_VW,
             N_BRANCH * D_MODEL)

LANES = 128
BF16_ROWS = 16
MXU_N = 256
VMEM_LIMIT = 56 * 1024 * 1024

NEG_INF = float("-inf")
ATT_Q_SCALE = ATT_DH ** -0.5 * math.log2(math.e)
ATT_KEY_CHUNK = 512


def _params(*sem):
    return pltpu.CompilerParams(dimension_semantics=sem, vmem_limit_bytes=VMEM_LIMIT)


def _resident(shape):
    nd = len(shape)
    return pl.BlockSpec(shape, lambda *_: (0,) * nd, pipeline_mode=pl.Buffered(1))


def _dot(a, b):
    return jnp.dot(a, b, preferred_element_type=F32)


def _dot_nt(a, b):
    return lax.dot_general(a, b, (((1,), (1,)), ((), ())), preferred_element_type=F32)


def _rms(x, g):
    return x * lax.rsqrt(jnp.mean(x * x, axis=-1, keepdims=True) + EPS) * g


def _silu(x):
    return x * jax.nn.sigmoid(x)


def _gelu_tanh(x):
    return 0.5 * x * (1.0 + jnp.tanh(math.sqrt(2.0 / math.pi) * (x + 0.044715 * (x * x * x))))


def _ada_kernel(c_ref, w_ref, b_ref, o_ref):
    c = c_ref[...]
    o_ref[0] = _dot(_silu(c).astype(BF16), w_ref[0].astype(BF16)) + b_ref[0]


def _ada(c_all, w_ada, b_ada):
    rows = c_all.shape[0]
    n = w_ada.shape[-1]
    tn = 1536
    return pl.pallas_call(
        _ada_kernel,
        grid=(DEPTH, n // tn),
        in_specs=[
            pl.BlockSpec((rows, D_MODEL), lambda l, j: (0, 0)),
            pl.BlockSpec((1, D_MODEL, tn), lambda l, j: (l, 0, j)),
            pl.BlockSpec((1, 1, tn), lambda l, j: (l, 0, j)),
        ],
        out_specs=pl.BlockSpec((1, rows, tn), lambda l, j: (l, 0, j)),
        out_shape=jax.ShapeDtypeStruct((DEPTH, rows, n), F32),
        compiler_params=_params("parallel", "parallel"),
        name="ada",
    )(c_all, w_ada, b_ada.reshape(DEPTH, 1, n))


def _prenorm_kernel(x_ref, g_ref, sh_ref, sc_ref, h_ref):
    y = _rms(x_ref[0], g_ref[...])
    h_ref[0] = (y * (1.0 + sc_ref[0]) + sh_ref[0]).astype(BF16)


def _prenorm(x, g, sh, sc, tm):
    B, T, D = x.shape
    row = pl.BlockSpec((1, tm, D), lambda b, i: (b, i, 0))
    vec = pl.BlockSpec((1, 1, D), lambda b, i: (b, 0, 0))
    return pl.pallas_call(
        _prenorm_kernel,
        grid=(B, T // tm),
        in_specs=[row, pl.BlockSpec((1, D), lambda b, i: (0, 0)), vec, vec],
        out_specs=row,
        out_shape=jax.ShapeDtypeStruct((B, T, D), BF16),
        compiler_params=_params("parallel", "parallel"),
        name="prenorm",
    )(x, g.reshape(1, D), sh, sc)


def _att_proj_kernel(h_ref, w_ref, *rest, rope):
    if rope:
        cos_ref, sa_ref, sb_ref, q_ref, k_ref, v_ref = rest
        cos, sa, sb = cos_ref[...], sa_ref[...], sb_ref[...]
    else:
        q_ref, k_ref, v_ref = rest
    h = h_ref[0]

    def rot(a):
        half = ATT_DH // 4
        return a * cos + pltpu.roll(a, LANES - half, 1) * sa + pltpu.roll(a, half, 1) * sb

    for j in range(ATT_QW // MXU_N):
        cols = slice(j * MXU_N, (j + 1) * MXU_N)
        aq = _dot(h, w_ref[:, j * MXU_N:(j + 1) * MXU_N])
        ak = _dot(h, w_ref[:, ATT_QW + j * MXU_N:ATT_QW + (j + 1) * MXU_N])
        av = _dot(h, w_ref[:, 2 * ATT_QW + j * MXU_N:2 * ATT_QW + (j + 1) * MXU_N])
        if rope:
            aq = jnp.concatenate([rot(aq[:, :LANES]), rot(aq[:, LANES:])], axis=1)
            ak = jnp.concatenate([rot(ak[:, :LANES]), rot(ak[:, LANES:])], axis=1)
        q_ref[0, :, cols] = (aq * ATT_Q_SCALE).astype(BF16)
        k_ref[0, :, cols] = ak.astype(BF16)
        v_ref[0, :, cols] = av.astype(BF16)


def _att_proj(h, w_qkv, tables, tm):
    B, T, D = h.shape
    rope = tables is not None
    row = pl.BlockSpec((1, tm, D), lambda b, i: (b, i, 0))
    out = pl.BlockSpec((1, tm, ATT_QW), lambda b, i: (b, i, 0))
    in_specs = [row, _resident(w_qkv.shape)]
    args = [h, w_qkv]
    if rope:
        in_specs += [pl.BlockSpec((tm, LANES), lambda b, i: (i, 0))] * 3
        args += list(tables)
    return pl.pallas_call(
        functools.partial(_att_proj_kernel, rope=rope),
        grid=(B, T // tm),
        in_specs=in_specs,
        out_specs=[out, out, out],
        out_shape=[jax.ShapeDtypeStruct((B, T, ATT_QW), BF16)] * 3,
        compiler_params=_params("parallel", "parallel"),
        name="att_proj",
    )(*args)


def _attn_kernel(lam_ref, q_ref, g_ref, *rest, n_kv, post_scale):
    kv = rest[:2 * n_kv]
    o_ref = rest[2 * n_kv]
    vext = rest[2 * n_kv + 1:]

    @pl.when(pl.program_id(2) == 0)
    def _():
        for j in range(n_kv):
            v = kv[2 * j + 1][0]
            vext[j][:, :LANES] = v
            vext[j][:, LANES:] = jnp.ones_like(v)

    q = q_ref[0]
    lane = lax.broadcasted_iota(jnp.int32, q.shape, 1)
    zero = jnp.zeros_like(q)
    outs = []
    for i in range(2):
        qi = jnp.where(lane < ATT_DH, q, zero) if i == 0 else jnp.where(lane >= ATT_DH, q, zero)
        m = acc = None
        for j in range(n_kv):
            n_keys = kv[2 * j].shape[1]
            kc = min(n_keys, ATT_KEY_CHUNK)
            for c0 in range(0, n_keys, kc):
                s = _dot_nt(qi, kv[2 * j][0, c0:c0 + kc, :])
                mc = jnp.max(s, axis=-1, keepdims=True)
                m_new = mc if m is None else jnp.maximum(m, mc)
                pv = _dot(jnp.exp2(s - m_new).astype(BF16), vext[j][c0:c0 + kc, :])
                acc = pv if m is None else acc * jnp.exp2(m - m_new) + pv
                m = m_new
        outs.append(acc[:, :LANES] / acc[:, LANES:])
    o = outs[0] - lam_ref[0] * outs[1]
    o_ref[0] = (_rms(o, g_ref[...]) * post_scale).astype(BF16)


def _attention(q, kvs, lam, g, post_scale, tq):
    B, T, _ = q.shape
    in_specs = [
        pl.BlockSpec(memory_space=pltpu.SMEM),
        pl.BlockSpec((1, tq, LANES), lambda b, h, i: (b, i, h)),
        pl.BlockSpec((1, LANES), lambda b, h, i: (0, 0)),
    ]
    args = [lam.reshape(1), q, g.reshape(1, ATT_DV)]
    scratch = []
    for k, v in kvs:
        spec = pl.BlockSpec((1, k.shape[1], LANES), lambda b, h, i: (b, 0, h))
        in_specs += [spec, spec]
        args += [k, v]
        scratch.append(pltpu.VMEM((k.shape[1], 2 * LANES), BF16))
    return pl.pallas_call(
        functools.partial(_attn_kernel, n_kv=len(kvs), post_scale=post_scale),
        grid=(B, ATT_HEADS, T // tq),
        in_specs=in_specs,
        out_specs=pl.BlockSpec((1, tq, LANES), lambda b, h, i: (b, i, h)),
        out_shape=jax.ShapeDtypeStruct((B, T, ATT_VW), BF16),
        scratch_shapes=scratch,
        compiler_params=_params("parallel", "parallel", "arbitrary"),
        name="attention",
    )(*args)


def _gmlp_kernel(h_ref, w_ref, lng_ref, lnb_ref, ws_ref, bias_ref, o_ref, *, tm):
    h = h_ref[0]
    u = _gelu_tanh(_dot(h, w_ref[:, :GM_WIDTH]))
    v = _gelu_tanh(_dot(h, w_ref[:, GM_WIDTH:]))
    mu = jnp.mean(v, axis=-1, keepdims=True)
    vc = v - mu
    var = jnp.mean(vc * vc, axis=-1, keepdims=True)
    vn = (vc * lax.rsqrt(var + EPS) * lng_ref[...] + lnb_ref[...]).astype(BF16)
    dg = GM_WIDTH // GM_GROUPS
    for r in range(tm // GM_CHUNK):
        rows = slice(r * GM_CHUNK, (r + 1) * GM_CHUNK)
        for g in range(GM_GROUPS):
            cols = slice(g * dg, (g + 1) * dg)
            s = _dot(ws_ref[g], vn[rows, cols]) + bias_ref[:, cols]
            o_ref[0, rows, cols] = (u[rows, cols] * s).astype(BF16)


def _gmlp(h, w_uv, ln_g, ln_b, ws, bias_full, tm):
    B, T, D = h.shape
    return pl.pallas_call(
        functools.partial(_gmlp_kernel, tm=tm),
        grid=(B, T // tm),
        in_specs=[
            pl.BlockSpec((1, tm, D), lambda b, i: (b, i, 0)),
            _resident(w_uv.shape),
            _resident((1, GM_WIDTH)),
            _resident((1, GM_WIDTH)),
            _resident(ws.shape),
            _resident(bias_full.shape),
        ],
        out_specs=pl.BlockSpec((1, tm, GM_WIDTH), lambda b, i: (b, i, 0)),
        out_shape=jax.ShapeDtypeStruct((B, T, GM_WIDTH), BF16),
        compiler_params=_params("parallel", "parallel"),
        name="gmlp",
    )(h, w_uv, ln_g.reshape(1, GM_WIDTH), ln_b.reshape(1, GM_WIDTH), ws, bias_full)


def _hgrn_proj_kernel(h_ref, w_ref, lb_ref, q_ref, kf_ref, lff_ref, kb_ref, lfb_ref, v_ref, g_ref):
    h = h_ref[0]

    def gates(z, d, cols):
        log_lb = lb_ref[3 * d + 0:3 * d + 1, cols]
        log_1m = lb_ref[3 * d + 1:3 * d + 2, cols]
        one_m = lb_ref[3 * d + 2:3 * d + 3, cols]
        log_sig = jnp.minimum(z, 0.0) - jnp.log1p(jnp.exp(-jnp.abs(z)))
        b = log_1m + log_sig
        logf = jnp.maximum(log_lb, b) + jnp.log1p(jnp.exp(-jnp.abs(log_lb - b)))
        return logf, one_m * jax.nn.sigmoid(-z)

    for j in range(HG_KW // MXU_N):
        cols = slice(j * MXU_N, (j + 1) * MXU_N)

        def proj(seg):
            return _dot(h, w_ref[:, seg * HG_KW + j * MXU_N:seg * HG_KW + (j + 1) * MXU_N])

        q_ref[0, :, cols] = _silu(proj(0)).astype(BF16)
        lf, k = gates(proj(1), 0, cols)
        lff_ref[0, :, cols] = lf
        kf_ref[0, :, cols] = k.astype(BF16)
        lf, k = gates(proj(2), 1, cols)
        lfb_ref[0, :, cols] = lf
        kb_ref[0, :, cols] = k.astype(BF16)
        v_ref[0, :, cols] = proj(3).astype(BF16)
        g_ref[0, :, cols] = _silu(proj(4)).astype(BF16)


def _hgrn_proj(h, w_hg, lb_tab, tm):
    B, T, D = h.shape
    out = pl.BlockSpec((1, tm, HG_KW), lambda b, i: (b, i, 0))
    bf = jax.ShapeDtypeStruct((B, T, HG_KW), BF16)
    f32 = jax.ShapeDtypeStruct((B, T, HG_KW), F32)
    return pl.pallas_call(
        _hgrn_proj_kernel,
        grid=(B, T // tm),
        in_specs=[pl.BlockSpec((1, tm, D), lambda b, i: (b, i, 0)), _resident(w_hg.shape),
                  _resident(lb_tab.shape)],
        out_specs=[out] * 7,
        out_shape=[bf, bf, f32, bf, f32, bf, bf],
        compiler_params=_params("parallel", "parallel"),
        name="hgrn_proj",
    )(h, w_hg, lb_tab)


def _split3(x):
    a = x.astype(BF16)
    r = x - a.astype(F32)
    b = r.astype(BF16)
    c = (r - b.astype(F32)).astype(BF16)
    return a, b, c


def _hgrn_chunk(q, k, lf, v, st, rev, consts):
    C = HG_CHUNK
    tri, row_id, col_id, r_in = consts
    qf = q.astype(F32)
    kf = k.astype(F32)
    a1, a2, a3 = _split3(lf)
    A = _dot(tri, a1) + _dot(tri, a2) + _dot(tri, a3)
    a_tot = A[0:1] if rev else A[C - 1:C]

    o = _dot_nt((qf * jnp.exp(A)).astype(BF16), st.astype(BF16))
    k_end = (kf * jnp.exp(a_tot - A)).astype(BF16)
    st_new = st * jnp.exp(a_tot) + _dot(v.astype(F32).T.astype(BF16), k_end)

    scores = jnp.zeros((C, C), F32)
    seg = 2 * HG_SUB
    while seg <= C:
        half = seg // 2
        pieces = []
        for s0 in range(0, C, seg):
            ref_row = s0 + half if rev else s0 + half - 1
            pieces.append(jnp.broadcast_to(A[ref_row:ref_row + 1], (seg, HG_DK)))
        ref = pieces[0] if len(pieces) == 1 else jnp.concatenate(pieces, axis=0)
        upper = (r_in & (seg - 1)) >= half
        q_side = (r_in & (seg - 1)) < half if rev else upper
        qm = (qf * jnp.exp(jnp.where(q_side, A - ref, NEG_INF))).astype(BF16)
        km = (kf * jnp.exp(jnp.where(q_side, NEG_INF, ref - A))).astype(BF16)
        p = _dot_nt(qm, km)
        if seg < C:
            p = jnp.where((row_id & -seg) == (col_id & -seg), p, 0.0)
        scores = scores + p
        seg *= 2

    nsub = C // HG_SUB
    A3 = A.reshape(nsub, HG_SUB, HG_DK)
    k3 = kf.reshape(nsub, HG_SUB, HG_DK)
    t_in = r_in & (HG_SUB - 1)
    sub0 = row_id & -HG_SUB
    for s in range(HG_SUB):
        a_s = jnp.broadcast_to(A3[:, s:s + 1, :], A3.shape).reshape(C, HG_DK)
        k_s = jnp.broadcast_to(k3[:, s:s + 1, :], k3.shape).reshape(C, HG_DK)
        live = (t_in <= s) if rev else (t_in >= s)
        x = qf * k_s * jnp.exp(jnp.where(live, A - a_s, NEG_INF))
        col = jnp.sum(x, axis=-1, keepdims=True)
        scores = jnp.where(col_id == sub0 + s, col, scores)
    o = o + _dot(scores.astype(BF16), v)
    return o, st_new


def _hgrn_scan_kernel(g_ref, q_ref, kf_ref, lff_ref, kb_ref, lfb_ref, v_ref, og_ref,
                      cq_ref, ckf_ref, clff_ref, ckb_ref, clfb_ref, cv_ref, cog_ref,
                      y_ref, yc_ref, of_ref, ob_ref, cof_ref, cob_ref, st_ref, *, T, L):
    C = HG_CHUNK
    row_id = lax.broadcasted_iota(jnp.int32, (C, C), 0)
    col_id = lax.broadcasted_iota(jnp.int32, (C, C), 1)
    tri_f = (col_id <= row_id).astype(BF16)
    tri_b = (col_id >= row_id).astype(BF16)
    r_in = lax.broadcasted_iota(jnp.int32, (C, HG_DK), 0)
    consts_f = (tri_f, row_id, col_id, r_in)
    consts_b = (tri_b, row_id, col_id, r_in)

    st_ref[...] = jnp.zeros_like(st_ref)

    def scan(q_r, kf_r, lff_r, kb_r, lfb_r, v_r, of_r, ob_r, n):
        def body(c, carry):
            fs = pl.ds(pl.multiple_of(c * C, C), C)
            o, st = _hgrn_chunk(q_r[0, fs, :], kf_r[0, fs, :], lff_r[0, fs, :], v_r[0, fs, :],
                                st_ref[0], False, consts_f)
            of_r[fs, :] = o
            st_ref[0] = st
            bs = pl.ds(pl.multiple_of((n - 1 - c) * C, C), C)
            o, st = _hgrn_chunk(q_r[0, bs, :], kb_r[0, bs, :], lfb_r[0, bs, :], v_r[0, bs, :],
                                st_ref[1], True, consts_b)
            ob_r[bs, :] = o
            st_ref[1] = st
            return carry

        lax.fori_loop(0, n, body, 0)

    def finish(of_r, ob_r, og_r, y_r):
        y_r[0] = (_rms(of_r[...] + ob_r[...], g_ref[...]) * og_r[0].astype(F32)).astype(BF16)

    scan(cq_ref, ckf_ref, clff_ref, ckb_ref, clfb_ref, cv_ref, cof_ref, cob_ref, L // C)
    finish(cof_ref, cob_ref, cog_ref, yc_ref)
    scan(q_ref, kf_ref, lff_ref, kb_ref, lfb_ref, v_ref, of_ref, ob_ref, T // C)
    finish(of_ref, ob_ref, og_ref, y_ref)


def _hgrn_scan(lat, ctx, g):
    B, T, _ = lat[0].shape
    L = ctx[0].shape[1]
    lat_spec = pl.BlockSpec((1, T, LANES), lambda b, h: (b, 0, h))
    ctx_spec = pl.BlockSpec((1, L, LANES), lambda b, h: (b, 0, h))
    return pl.pallas_call(
        functools.partial(_hgrn_scan_kernel, T=T, L=L),
        grid=(B, HG_HEADS),
        in_specs=[pl.BlockSpec((1, HG_DV), lambda b, h: (0, 0))] + [lat_spec] * 7 + [ctx_spec] * 7,
        out_specs=[lat_spec, ctx_spec],
        out_shape=[jax.ShapeDtypeStruct((B, T, HG_VW), BF16), jax.ShapeDtypeStruct((B, L, HG_VW), BF16)],
        scratch_shapes=[
            pltpu.VMEM((T, HG_DV), F32), pltpu.VMEM((T, HG_DV), F32),
            pltpu.VMEM((L, HG_DV), F32), pltpu.VMEM((L, HG_DV), F32),
            pltpu.VMEM((2, HG_DV, HG_DK), F32),
        ],
        compiler_params=_params("parallel", "parallel"),
        name="hgrn_scan",
    )(g.reshape(1, HG_DV), *lat, *ctx)


def _merge_kernel(x_ref, h_ref, ya_ref, yg_ref, yh_ref, wgate_ref, wa_ref, wg_ref, wh_ref, wo_ref,
                  gpost_ref, gt_ref, gpre_ref, sh_ref, sc_ref, x1_ref, h2_ref):
    h = h_ref[0]
    D = D_MODEL
    y = jax.nn.sigmoid(_dot(h, wgate_ref[:, :D])) * _dot(ya_ref[0], wa_ref[...])
    y = y + jax.nn.sigmoid(_dot(h, wgate_ref[:, D:2 * D])) * _dot(yg_ref[0], wg_ref[...])
    y = y + jax.nn.sigmoid(_dot(h, wgate_ref[:, 2 * D:])) * _dot(yh_ref[0], wh_ref[...])
    z = _dot(y.astype(BF16), wo_ref[...])
    x1 = x_ref[0] + gt_ref[0] * _rms(z, gpost_ref[...])
    x1_ref[0] = x1
    h2_ref[0] = (_rms(x1, gpre_ref[...]) * (1.0 + sc_ref[0]) + sh_ref[0]).astype(BF16)


def _merge(x, h, ya, yg, yh, wgate, wa, wg, wh, wo, g_post, gt, g_pre, sh, sc, tm):
    B, T, D = x.shape
    row = pl.BlockSpec((1, tm, D), lambda b, i: (b, i, 0))
    vec = pl.BlockSpec((1, 1, D), lambda b, i: (b, 0, 0))
    par = pl.BlockSpec((1, D), lambda b, i: (0, 0))
    return pl.pallas_call(
        _merge_kernel,
        grid=(B, T // tm),
        in_specs=[row] * 5 + [_resident(wgate.shape)] + [_resident(wa.shape)] * 4 + [par, vec, par, vec, vec],
        out_specs=[row, row],
        out_shape=[jax.ShapeDtypeStruct((B, T, D), F32), jax.ShapeDtypeStruct((B, T, D), BF16)],
        compiler_params=_params("parallel", "parallel"),
        name="merge",
    )(x, h, ya, yg, yh, wgate, wa, wg, wh, wo, g_post.reshape(1, D), gt, g_pre.reshape(1, D), sh, sc)


def _ffn_kernel(x_ref, h_ref, hp_ref, hn_ref, wup_ref, cw_ref, cb_ref, wdn_ref, gpost_ref, gt_ref,
                o_ref, acc_ref, *, tm):
    i = pl.program_id(1)
    last = pl.num_programs(1) - 1
    halo = BF16_ROWS
    hp = jnp.where(i > 0, hp_ref[0], jnp.zeros_like(hp_ref[0]))
    hn = jnp.where(i < last, hn_ref[0], jnp.zeros_like(hn_ref[0]))
    h_ext = jnp.concatenate([hp, h_ref[0], hn], axis=0)
    rows = tm + 2 * halo
    acc_ref[...] = jnp.zeros_like(acc_ref)

    def conv(u, part, n):
        cw = cw_ref[part, n]
        u_prev = pltpu.roll(u, 1, 0)
        u_next = pltpu.roll(u, rows - 1, 0)
        return (u_prev * cw[0:1] + u * cw[1:2] + u_next * cw[2:3] + cb_ref[part, n])[halo:halo + tm]

    def body(n, carry):
        a = conv(_dot(h_ext, wup_ref[0, n]), 0, n)
        b = conv(_dot(h_ext, wup_ref[1, n]), 1, n)
        acc_ref[...] += _dot((_silu(a) * b).astype(BF16), wdn_ref[n])
        return carry

    lax.fori_loop(0, D_FF // FF_CHUNK, body, 0)
    o_ref[0] = x_ref[0] + gt_ref[0] * _rms(acc_ref[...], gpost_ref[...])


def _ffn(x, h, wup, cw, cb, wdn, g_post, gt, tm):
    B, T, D = x.shape
    nh = tm // BF16_ROWS
    row = pl.BlockSpec((1, tm, D), lambda b, i: (b, i, 0))
    prev = pl.BlockSpec((1, BF16_ROWS, D), lambda b, i: (b, jnp.maximum(i * nh - 1, 0), 0))
    nxt = pl.BlockSpec((1, BF16_ROWS, D), lambda b, i: (b, jnp.minimum((i + 1) * nh, T // BF16_ROWS - 1), 0))
    return pl.pallas_call(
        functools.partial(_ffn_kernel, tm=tm),
        grid=(B, T // tm),
        in_specs=[row, row, prev, nxt, _resident(wup.shape), _resident(cw.shape), _resident(cb.shape),
                  _resident(wdn.shape), pl.BlockSpec((1, D), lambda b, i: (0, 0)),
                  pl.BlockSpec((1, 1, D), lambda b, i: (b, 0, 0))],
        out_specs=row,
        out_shape=jax.ShapeDtypeStruct((B, T, D), F32),
        scratch_shapes=[pltpu.VMEM((tm, D), F32)],
        compiler_params=_params("parallel", "parallel"),
        name="ffn",
    )(x, h, h, h, wup, cw, cb, wdn, g_post.reshape(1, D), gt)


def _rope_tables(T):
    half = ATT_DH // 4
    t = jnp.arange(T, dtype=jnp.int32)
    rows = (t // GRID_W).astype(F32)
    cols = (t % GRID_W).astype(F32)
    inv = ROPE_BASE ** (-jnp.arange(half, dtype=F32) / half)
    ang_r = rows[:, None] * inv[None, :]
    ang_c = cols[:, None] * inv[None, :]
    z = jnp.zeros_like(ang_r)
    cos64 = jnp.concatenate([jnp.cos(ang_r)] * 2 + [jnp.cos(ang_c)] * 2, axis=1)
    sa64 = jnp.concatenate([-jnp.sin(ang_r), z, -jnp.sin(ang_c), z], axis=1)
    sb64 = jnp.concatenate([z, jnp.sin(ang_r), z, jnp.sin(ang_c)], axis=1)
    return tuple(jnp.concatenate([a, a], axis=1) for a in (cos64, sa64, sb64))


def _col_offsets():
    offs, start = [], 0
    for w in IN_SPLITS:
        offs.append(start)
        start += w
    return offs


def kernel(x, c, ctx, c_ctx, w_ada, b_ada, g_pre_mix, g_post_mix, g_pre_ffn, g_post_ffn, w_in, lam_q1, lam_k1, lam_q2, lam_k2, att_subln_g, gm_ln_g, gm_ln_b, gm_ws, gm_bs, hg_lb, hg_norm_g, w_br_att, w_br_gm, w_br_hg, w_out, w_up, conv_w, conv_b, w_down):
    B, T, D = x.shape
    L = ctx.shape[1]
    TM = 512
    TQ = 512

    rows = 16
    c_all = jnp.concatenate([c, c_ctx[None, :], jnp.zeros((rows - B - 1, D), F32)], axis=0)
    mod = _ada(c_all, w_ada, b_ada)

    lb_all = jnp.cumsum(jax.nn.softmax(hg_lb.astype(F32), axis=0), axis=0)
    lb_all = lb_all - lb_all[0]
    tables = _rope_tables(T)
    offs = _col_offsets()
    nff = D_FF // FF_CHUNK

    xc = ctx
    for l in range(DEPTH):
        last = l == DEPTH - 1
        lam_init = 0.8 - 0.6 * math.exp(-0.3 * l)
        lam = (jnp.exp(jnp.sum(lam_q1[l] * lam_k1[l])) - jnp.exp(jnp.sum(lam_q2[l] * lam_k2[l])) + lam_init)

        def mods(k, ctx_row):
            m = mod[l, :, k * D:(k + 1) * D]
            if ctx_row:
                return jnp.broadcast_to(m[B:B + 1], (B, D)).reshape(B, 1, D)
            return m[:B].reshape(B, 1, D)

        w_l = w_in[l].astype(BF16)
        w_qkv = w_l[:, offs[0]:offs[3]]
        w_uv = w_l[:, offs[3]:offs[5]]
        w_hg = w_l[:, offs[5]:offs[10]]
        w_gate = w_l[:, offs[10]:]
        lb = lb_all[l]
        lb_tab = jnp.stack([jnp.log(lb[0]), jnp.log1p(-lb[0]), 1.0 - lb[0],
                            jnp.log(lb[1]), jnp.log1p(-lb[1]), 1.0 - lb[1],
                            jnp.zeros_like(lb[0]), jnp.zeros_like(lb[0])], axis=0)
        ws = gm_ws[l].astype(BF16)
        dg = GM_WIDTH // GM_GROUPS
        bias_full = jnp.broadcast_to(gm_bs[l].T[:, :, None], (GM_CHUNK, GM_GROUPS, dg)).reshape(GM_CHUNK, GM_WIDTH)
        wa, wg, wh, wo = (w[l].astype(BF16) for w in (w_br_att, w_br_gm, w_br_hg, w_out))
        wup = w_up[l].astype(BF16).reshape(D, 2, nff, FF_CHUNK).transpose(1, 2, 0, 3)
        cw = conv_w[l].reshape(CONV_W, 2, nff, FF_CHUNK).transpose(1, 2, 0, 3)
        cb = conv_b[l].reshape(2, nff, 1, FF_CHUNK)
        wdn = w_down[l].astype(BF16).reshape(nff, FF_CHUNK, D)

        hc = _prenorm(xc, g_pre_mix[l], mods(0, True), mods(1, True), L)
        h = _prenorm(x, g_pre_mix[l], mods(0, False), mods(1, False), TM)
        cq, ck, cv = _att_proj(hc, w_qkv, None, L)
        q, k, v = _att_proj(h, w_qkv, tables, TM)
        c_hg = _hgrn_proj(hc, w_hg, lb_tab, L)
        l_hg = _hgrn_proj(h, w_hg, lb_tab, TM)

        y_att = _attention(q, [(k, v), (ck, cv)], lam, att_subln_g[l], 1.0 - lam_init, TQ)
        y_gm = _gmlp(h, w_uv, gm_ln_g[l], gm_ln_b[l], ws, bias_full, TM)
        y_hg, yc_hg = _hgrn_scan(l_hg, c_hg, hg_norm_g[l])

        x, h2 = _merge(x, h, y_att, y_gm, y_hg, w_gate, wa, wg, wh, wo, g_post_mix[l], mods(2, False),
                       g_pre_ffn[l], mods(3, False), mods(4, False), TM)
        x = _ffn(x, h2, wup, cw, cb, wdn, g_post_ffn[l], mods(5, False), TM)

        if not last:
            yc_att = _attention(cq, [(ck, cv)], lam, att_subln_g[l], 1.0 - lam_init, L)
            yc_gm = _gmlp(hc, w_uv, gm_ln_g[l], gm_ln_b[l], ws, bias_full, L)
            xc, hc2 = _merge(xc, hc, yc_att, yc_gm, yc_hg, w_gate, wa, wg, wh, wo, g_post_mix[l],
                             mods(2, True), g_pre_ffn[l], mods(3, True), mods(4, True), L)
            xc = _ffn(xc, hc2, wup, cw, cb, wdn, g_post_ffn[l], mods(5, True), L)
    return x
```

```python
import functools
import math

import jax
import jax.numpy as jnp
import numpy as np
from jax import lax
from jax.experimental import pallas as pl
from jax.experimental.pallas import tpu as pltpu

F32 = jnp.float32
BF16 = jnp.bfloat16

D_MODEL = 1024
DEPTH = 2
GRID_W = 64
ATT_HEADS = 8
ATT_DH = 64
ATT_DV = 2 * ATT_DH
GM_WIDTH = 1024
GM_GROUPS = 8
GM_CHUNK = 128
HG_HEADS = 8
HG_DK = 128
HG_DV = 128
HG_CHUNK = 64
HG_LEVELS = 6
HG_GROUP = 4
N_BRANCH = 3
D_FF = 2816
FF_CHUNK = 256
CONV_W = 3
ROPE_BASE = 10000.0
EPS = 1e-6

ATT_QW = ATT_HEADS * 2 * ATT_DH
ATT_VW = ATT_HEADS * ATT_DV
HG_KW = HG_HEADS * HG_DK
HG_VW = HG_HEADS * HG_DV
IN_SPLITS = (ATT_QW, ATT_QW, ATT_VW, GM_WIDTH, GM_WIDTH, HG_KW, HG_KW, HG_KW, HG_VW, HG_VW,
             N_BRANCH * D_MODEL)

LANES = 128
BF16_ROWS = 16
MXU_N = 256
VMEM_LIMIT = 56 * 1024 * 1024

NEG_INF = float("-inf")
ATT_Q_SCALE = ATT_DH ** -0.5 * math.log2(math.e)
ATT_KEY_CHUNK = 512


def _params(*sem):
    return pltpu.CompilerParams(dimension_semantics=sem, vmem_limit_bytes=VMEM_LIMIT)


def _resident(shape):
    nd = len(shape)
    return pl.BlockSpec(shape, lambda *_: (0,) * nd, pipeline_mode=pl.Buffered(1))


def _dot(a, b):
    return jnp.dot(a, b, preferred_element_type=F32)


def _dot_nt(a, b):
    return lax.dot_general(a, b, (((1,), (1,)), ((), ())), preferred_element_type=F32)


def _rms(x, g):
    return x * lax.rsqrt(jnp.mean(x * x, axis=-1, keepdims=True) + EPS) * g


def _silu(x):
    return x * jax.nn.sigmoid(x)


def _gelu_tanh(x):
    return 0.5 * x * (1.0 + jnp.tanh(math.sqrt(2.0 / math.pi) * (x + 0.044715 * (x * x * x))))


def _ada_kernel(c_ref, w_ref, b_ref, o_ref):
    c = c_ref[...]
    o_ref[0] = _dot(_silu(c).astype(BF16), w_ref[0].astype(BF16)) + b_ref[0]


def _ada(c_all, w_ada, b_ada):
    rows = c_all.shape[0]
    n = w_ada.shape[-1]
    tn = 1536
    return pl.pallas_call(
        _ada_kernel,
        grid=(DEPTH, n // tn),
        in_specs=[
            pl.BlockSpec((rows, D_MODEL), lambda l, j: (0, 0)),
            pl.BlockSpec((1, D_MODEL, tn), lambda l, j: (l, 0, j)),
            pl.BlockSpec((1, 1, tn), lambda l, j: (l, 0, j)),
        ],
        out_specs=pl.BlockSpec((1, rows, tn), lambda l, j: (l, 0, j)),
        out_shape=jax.ShapeDtypeStruct((DEPTH, rows, n), F32),
        compiler_params=_params("parallel", "parallel"),
        name="ada",
    )(c_all, w_ada, b_ada.reshape(DEPTH, 1, n))


def _prenorm_kernel(x_ref, g_ref, sh_ref, sc_ref, h_ref):
    y = _rms(x_ref[0], g_ref[...])
    h_ref[0] = (y * (1.0 + sc_ref[0]) + sh_ref[0]).astype(BF16)


def _prenorm(x, g, sh, sc, tm):
    B, T, D = x.shape
    row = pl.BlockSpec((1, tm, D), lambda b, i: (b, i, 0))
    vec = pl.BlockSpec((1, 1, D), lambda b, i: (b, 0, 0))
    return pl.pallas_call(
        _prenorm_kernel,
        grid=(B, T // tm),
        in_specs=[row, pl.BlockSpec((1, D), lambda b, i: (0, 0)), vec, vec],
        out_specs=row,
        out_shape=jax.ShapeDtypeStruct((B, T, D), BF16),
        compiler_params=_params("parallel", "parallel"),
        name="prenorm",
    )(x, g.reshape(1, D), sh, sc)


def _att_proj_kernel(h_ref, w_ref, *rest, rope):
    if rope:
        cos_ref, sa_ref, sb_ref, q_ref, k_ref, v_ref = rest
        cos, sa, sb = cos_ref[...], sa_ref[...], sb_ref[...]
    else:
        q_ref, k_ref, v_ref = rest
    h = h_ref[0]

    def rot(a):
        half = ATT_DH // 4
        return a * cos + pltpu.roll(a, LANES - half, 1) * sa + pltpu.roll(a, half, 1) * sb

    for j in range(ATT_QW // MXU_N):
        cols = slice(j * MXU_N, (j + 1) * MXU_N)
        aq = _dot(h, w_ref[:, j * MXU_N:(j + 1) * MXU_N])
        ak = _dot(h, w_ref[:, ATT_QW + j * MXU_N:ATT_QW + (j + 1) * MXU_N])
        av = _dot(h, w_ref[:, 2 * ATT_QW + j * MXU_N:2 * ATT_QW + (j + 1) * MXU_N])
        if rope:
            aq = jnp.concatenate([rot(aq[:, :LANES]), rot(aq[:, LANES:])], axis=1)
            ak = jnp.concatenate([rot(ak[:, :LANES]), rot(ak[:, LANES:])], axis=1)
        q_ref[0, :, cols] = (aq * ATT_Q_SCALE).astype(BF16)
        k_ref[0, :, cols] = ak.astype(BF16)
        v_ref[0, :, cols] = av.astype(BF16)


def _att_proj(h, w_qkv, tables, tm):
    B, T, D = h.shape
    rope = tables is not None
    row = pl.BlockSpec((1, tm, D), lambda b, i: (b, i, 0))
    out = pl.BlockSpec((1, tm, ATT_QW), lambda b, i: (b, i, 0))
    in_specs = [row, _resident(w_qkv.shape)]
    args = [h, w_qkv]
    if rope:
        in_specs += [pl.BlockSpec((tm, LANES), lambda b, i: (i, 0))] * 3
        args += list(tables)
    return pl.pallas_call(
        functools.partial(_att_proj_kernel, rope=rope),
        grid=(B, T // tm),
        in_specs=in_specs,
        out_specs=[out, out, out],
        out_shape=[jax.ShapeDtypeStruct((B, T, ATT_QW), BF16)] * 3,
        compiler_params=_params("parallel", "parallel"),
        name="att_proj",
    )(*args)


def _attn_kernel(lam_ref, q_ref, g_ref, *rest, n_kv, post_scale):
    kv = rest[:2 * n_kv]
    o_ref = rest[2 * n_kv]
    vext = rest[2 * n_kv + 1:]

    @pl.when(pl.program_id(2) == 0)
    def _():
        for j in range(n_kv):
            v = kv[2 * j + 1][0]
            vext[j][:, :LANES] = v
            vext[j][:, LANES:] = jnp.ones_like(v)

    q = q_ref[0]
    lane = lax.broadcasted_iota(jnp.int32, q.shape, 1)
    zero = jnp.zeros_like(q)
    outs = []
    for i in range(2):
        qi = jnp.where(lane < ATT_DH, q, zero) if i == 0 else jnp.where(lane >= ATT_DH, q, zero)
        m = acc = None
        for j in range(n_kv):
            n_keys = kv[2 * j].shape[1]
            kc = min(n_keys, ATT_KEY_CHUNK)
            for c0 in range(0, n_keys, kc):
                s = _dot_nt(qi, kv[2 * j][0, c0:c0 + kc, :])
                mc = jnp.max(s, axis=-1, keepdims=True)
                m_new = mc if m is None else jnp.maximum(m, mc)
                pv = _dot(jnp.exp2(s - m_new).astype(BF16), vext[j][c0:c0 + kc, :])
                acc = pv if m is None else acc * jnp.exp2(m - m_new) + pv
                m = m_new
        outs.append(acc[:, :LANES] / acc[:, LANES:])
    o = outs[0] - lam_ref[0] * outs[1]
    o_ref[0] = (_rms(o, g_ref[...]) * post_scale).astype(BF16)


def _attention(q, kvs, lam, g, post_scale, tq):
    B, T, _ = q.shape
    in_specs = [
        pl.BlockSpec(memory_space=pltpu.SMEM),
        pl.BlockSpec((1, tq, LANES), lambda b, h, i: (b, i, h)),
        pl.BlockSpec((1, LANES), lambda b, h, i: (0, 0)),
    ]
    args = [lam.reshape(1), q, g.reshape(1, ATT_DV)]
    scratch = []
    for k, v in kvs:
        spec = pl.BlockSpec((1, k.shape[1], LANES), lambda b, h, i: (b, 0, h))
        in_specs += [spec, spec]
        args += [k, v]
        scratch.append(pltpu.VMEM((k.shape[1], 2 * LANES), BF16))
    return pl.pallas_call(
        functools.partial(_attn_kernel, n_kv=len(kvs), post_scale=post_scale),
        grid=(B, ATT_HEADS, T // tq),
        in_specs=in_specs,
        out_specs=pl.BlockSpec((1, tq, LANES), lambda b, h, i: (b, i, h)),
        out_shape=jax.ShapeDtypeStruct((B, T, ATT_VW), BF16),
        scratch_shapes=scratch,
        compiler_params=_params("parallel", "parallel", "arbitrary"),
        name="attention",
    )(*args)


def _gmlp_kernel(h_ref, w_ref, lng_ref, lnb_ref, ws_ref, bias_ref, o_ref, *, tm):
    h = h_ref[0]
    u = _gelu_tanh(_dot(h, w_ref[:, :GM_WIDTH]))
    v = _gelu_tanh(_dot(h, w_ref[:, GM_WIDTH:]))
    mu = jnp.mean(v, axis=-1, keepdims=True)
    vc = v - mu
    var = jnp.mean(vc * vc, axis=-1, keepdims=True)
    vn = (vc * lax.rsqrt(var + EPS) * lng_ref[...] + lnb_ref[...]).astype(BF16)
    dg = GM_WIDTH // GM_GROUPS
    for r in range(tm // GM_CHUNK):
        rows = slice(r * GM_CHUNK, (r + 1) * GM_CHUNK)
        for g in range(GM_GROUPS):
            cols = slice(g * dg, (g + 1) * dg)
            s = _dot(ws_ref[g], vn[rows, cols]) + bias_ref[:, cols]
            o_ref[0, rows, cols] = (u[rows, cols] * s).astype(BF16)


def _gmlp(h, w_uv, ln_g, ln_b, ws, bias_full, tm):
    B, T, D = h.shape
    return pl.pallas_call(
        functools.partial(_gmlp_kernel, tm=tm),
        grid=(B, T // tm),
        in_specs=[
            pl.BlockSpec((1, tm, D), lambda b, i: (b, i, 0)),
            _resident(w_uv.shape),
            _resident((1, GM_WIDTH)),
            _resident((1, GM_WIDTH)),
            _resident(ws.shape),
            _resident(bias_full.shape),
        ],
        out_specs=pl.BlockSpec((1, tm, GM_WIDTH), lambda b, i: (b, i, 0)),
        out_shape=jax.ShapeDtypeStruct((B, T, GM_WIDTH), BF16),
        compiler_params=_params("parallel", "parallel"),
        name="gmlp",
    )(h, w_uv, ln_g.reshape(1, GM_WIDTH), ln_b.reshape(1, GM_WIDTH), ws, bias_full)


def _hgrn_proj_kernel(h_ref, w_ref, lb_ref, q_ref, kf_ref, lff_ref, kb_ref, lfb_ref, v_ref, g_ref):
    h = h_ref[0]

    def gates(z, d, cols):
        log_lb = lb_ref[3 * d + 0:3 * d + 1, cols]
        log_1m = lb_ref[3 * d + 1:3 * d + 2, cols]
        one_m = lb_ref[3 * d + 2:3 * d + 3, cols]
        log_sig = jnp.minimum(z, 0.0) - jnp.log1p(jnp.exp(-jnp.abs(z)))
        b = log_1m + log_sig
        logf = jnp.maximum(log_lb, b) + jnp.log1p(jnp.exp(-jnp.abs(log_lb - b)))
        return logf, one_m * jax.nn.sigmoid(-z)

    for j in range(HG_KW // MXU_N):
        cols = slice(j * MXU_N, (j + 1) * MXU_N)

        def proj(seg):
            return _dot(h, w_ref[:, seg * HG_KW + j * MXU_N:seg * HG_KW + (j + 1) * MXU_N])

        q_ref[0, :, cols] = _silu(proj(0)).astype(BF16)
        lf, k = gates(proj(1), 0, cols)
        lff_ref[0, :, cols] = lf
        kf_ref[0, :, cols] = k.astype(BF16)
        lf, k = gates(proj(2), 1, cols)
        lfb_ref[0, :, cols] = lf
        kb_ref[0, :, cols] = k.astype(BF16)
        v_ref[0, :, cols] = proj(3).astype(BF16)
        g_ref[0, :, cols] = _silu(proj(4)).astype(BF16)


def _hgrn_proj(h, w_hg, lb_tab, tm):
    B, T, D = h.shape
    out = pl.BlockSpec((1, tm, HG_KW), lambda b, i: (b, i, 0))
    bf = jax.ShapeDtypeStruct((B, T, HG_KW), BF16)
    f32 = jax.ShapeDtypeStruct((B, T, HG_KW), F32)
    return pl.pallas_call(
        _hgrn_proj_kernel,
        grid=(B, T // tm),
        in_specs=[pl.BlockSpec((1, tm, D), lambda b, i: (b, i, 0)), _resident(w_hg.shape),
                  _resident(lb_tab.shape)],
        out_specs=[out] * 7,
        out_shape=[bf, bf, f32, bf, f32, bf, bf],
        compiler_params=_params("parallel", "parallel"),
        name="hgrn_proj",
    )(h, w_hg, lb_tab)


def _split2(x):
    a = x.astype(BF16)
    return a, (x - a.astype(F32)).astype(BF16)


def _hgrn_decay_matrix():
    C = HG_CHUNK
    t = np.arange(C)[:, None]
    u = np.arange(C)[None, :]
    out = []
    for rev in (False, True):
        blocks = [u >= t, u < t] if rev else [u <= t, u > t]
        for l in range(HG_LEVELS):
            half = 1 << l
            mid = (t & -(2 * half)) + half
            upper = (t & half) != 0
            if rev:
                blocks.append(np.where(upper, (u >= mid) & (u < t), (u >= t) & (u < mid)))
            else:
                blocks.append(np.where(upper, (u >= mid) & (u <= t), (u > t) & (u < mid)))
        m = np.concatenate(blocks, axis=0).astype(np.float32)
        out += [m, m]
    return jnp.asarray(np.concatenate(out, axis=1), dtype=BF16)


def _hgrn_local(blocks, dmat, consts):
    C = HG_CHUNK
    G = range(len(blocks))
    level_of_pair, eye, r_in = consts
    qf = [blocks[g][0].astype(F32) for g in G]
    kf = [(blocks[g][1].astype(F32), blocks[g][2].astype(F32)) for g in G]
    z = jnp.zeros((C, HG_DK), BF16)

    def block_diag(lf_f, lf_b):
        f1, f2 = _split2(lf_f)
        b1, b2 = _split2(lf_b)
        return jnp.concatenate([jnp.concatenate([f1, z], axis=1), jnp.concatenate([f2, z], axis=1),
                                jnp.concatenate([z, b1], axis=1), jnp.concatenate([z, b2], axis=1)], axis=0)

    logs = [_dot(dmat, block_diag(blocks[g][3], blocks[g][4])) for g in G]
    ex = [jnp.exp(logs[g]) for g in G]
    ex = [(ex[g][:, :HG_DK], ex[g][:, HG_DK:]) for g in G]
    q_in = [[(qf[g] * ex[g][d][0:C]).astype(BF16) for d in range(2)] for g in G]
    k_out = [[(kf[g][d] * ex[g][d][C:2 * C]).astype(BF16) for d in range(2)] for g in G]
    g_tot = [[ex[g][0][C - 1:C], ex[g][1][0:1]] for g in G]
    scores = [jnp.where(eye, jnp.sum(qf[g] * (kf[g][0] + kf[g][1]), axis=-1, keepdims=True), 0.0) for g in G]
    for l in range(HG_LEVELS):
        upper = (r_in & (1 << l)) != 0
        rows = slice((2 + l) * C, (3 + l) * C)
        for g in G:
            e = [ex[g][d][rows] for d in range(2)]
            qs = jnp.concatenate([jnp.where(upper, qf[g] * e[0], 0.0), jnp.where(upper, 0.0, qf[g] * e[1])], axis=1)
            ks = jnp.concatenate([jnp.where(upper, 0.0, kf[g][0] * e[0]), jnp.where(upper, kf[g][1] * e[1], 0.0)],
                                 axis=1)
            scores[g] = jnp.where(level_of_pair == l + 1, _dot_nt(qs.astype(BF16), ks.astype(BF16)), scores[g])
    o = [_dot(scores[g].astype(BF16), blocks[g][5]) for g in G]
    return q_in, k_out, g_tot, o


def _hgrn_scan_kernel(g_ref, d_ref, q_ref, kf_ref, lff_ref, kb_ref, lfb_ref, v_ref, og_ref,
                      cq_ref, ckf_ref, clff_ref, ckb_ref, clfb_ref, cv_ref, cog_ref,
                      y_ref, yc_ref, o_ref, qin_ref, kout_ref, gtot_ref, st_ref, *, T, L):
    C = HG_CHUNK
    row_id = lax.broadcasted_iota(jnp.int32, (C, C), 0)
    col_id = lax.broadcasted_iota(jnp.int32, (C, C), 1)
    r_in = lax.broadcasted_iota(jnp.int32, (C, HG_DK), 0)
    diff = row_id ^ col_id
    level_of_pair = functools.reduce(jnp.add, [(diff >= (1 << j)).astype(jnp.int32) for j in range(HG_LEVELS)])
    consts = (level_of_pair, row_id == col_id, r_in)

    st_ref[...] = jnp.zeros_like(st_ref)

    def scan(q_r, k_rs, lf_rs, v_r, n):
        def local(i, carry):
            cs = [i * HG_GROUP + g for g in range(HG_GROUP)]
            rows = [pl.ds(pl.multiple_of(c * C, C), C) for c in cs]
            blocks = [(q_r[0, r, :], k_rs[0][0, r, :], k_rs[1][0, r, :], lf_rs[0][0, r, :], lf_rs[1][0, r, :],
                       v_r[0, r, :]) for r in rows]
            q_in, k_out, g_tot, o = _hgrn_local(blocks, d_ref[...], consts)
            for g, (c, r) in enumerate(zip(cs, rows)):
                for d in range(2):
                    qin_ref[d, r, :] = q_in[g][d]
                    kout_ref[d, r, :] = k_out[g][d]
                    gtot_ref[d, pl.ds(pl.multiple_of(c * 8, 8), 8), :] = jnp.broadcast_to(g_tot[g][d], (8, HG_DK))
                o_ref[r, :] = o[g]
            return carry

        def carried(i, carry):
            for d in range(2):
                c = i if d == 0 else n - 1 - i
                rows = pl.ds(pl.multiple_of(c * C, C), C)
                st = st_ref[d]
                o_ref[rows, :] += _dot_nt(qin_ref[d, rows, :], st.astype(BF16))
                v_t = v_r[0, rows, :].astype(F32).T.astype(BF16)
                g_tot = gtot_ref[d, pl.ds(pl.multiple_of(c * 8, 8), 8), :][0:1]
                st_ref[d] = st * g_tot + _dot(v_t, kout_ref[d, rows, :])
            return carry

        lax.fori_loop(0, n // HG_GROUP, local, 0)
        lax.fori_loop(0, n, carried, 0, unroll=4)

    def finish(n_rows, og_r, y_r):
        y_r[0] = (_rms(o_ref[0:n_rows, :], g_ref[...]) * og_r[0].astype(F32)).astype(BF16)

    scan(cq_ref, (ckf_ref, ckb_ref), (clff_ref, clfb_ref), cv_ref, L // C)
    finish(L, cog_ref, yc_ref)
    scan(q_ref, (kf_ref, kb_ref), (lff_ref, lfb_ref), v_ref, T // C)
    finish(T, og_ref, y_ref)


def _hgrn_scan(lat, ctx, g):
    B, T, _ = lat[0].shape
    L = ctx[0].shape[1]
    dmat = _hgrn_decay_matrix()
    lat_spec = pl.BlockSpec((1, T, LANES), lambda b, h: (b, 0, h))
    ctx_spec = pl.BlockSpec((1, L, LANES), lambda b, h: (b, 0, h))
    return pl.pallas_call(
        functools.partial(_hgrn_scan_kernel, T=T, L=L),
        grid=(B, HG_HEADS),
        in_specs=[pl.BlockSpec((1, HG_DV), lambda b, h: (0, 0)), _resident(dmat.shape)]
        + [lat_spec] * 7 + [ctx_spec] * 7,
        out_specs=[lat_spec, ctx_spec],
        out_shape=[jax.ShapeDtypeStruct((B, T, HG_VW), BF16), jax.ShapeDtypeStruct((B, L, HG_VW), BF16)],
        scratch_shapes=[
            pltpu.VMEM((T, HG_DV), F32),
            pltpu.VMEM((2, T, HG_DK), BF16),
            pltpu.VMEM((2, T, HG_DK), BF16),
            pltpu.VMEM((2, T // HG_CHUNK * 8, HG_DK), F32),
            pltpu.VMEM((2, HG_DV, HG_DK), F32),
        ],
        compiler_params=_params("parallel", "parallel"),
        name="hgrn_scan",
    )(g.reshape(1, HG_DV), dmat, *lat, *ctx)


def _merge_kernel(x_ref, h_ref, ya_ref, yg_ref, yh_ref, wgate_ref, wa_ref, wg_ref, wh_ref, wo_ref,
                  gpost_ref, gt_ref, gpre_ref, sh_ref, sc_ref, x1_ref, h2_ref):
    h = h_ref[0]
    D = D_MODEL
    y = jax.nn.sigmoid(_dot(h, wgate_ref[:, :D])) * _dot(ya_ref[0], wa_ref[...])
    y = y + jax.nn.sigmoid(_dot(h, wgate_ref[:, D:2 * D])) * _dot(yg_ref[0], wg_ref[...])
    y = y + jax.nn.sigmoid(_dot(h, wgate_ref[:, 2 * D:])) * _dot(yh_ref[0], wh_ref[...])
    z = _dot(y.astype(BF16), wo_ref[...])
    x1 = x_ref[0] + gt_ref[0] * _rms(z, gpost_ref[...])
    x1_ref[0] = x1
    h2_ref[0] = (_rms(x1, gpre_ref[...]) * (1.0 + sc_ref[0]) + sh_ref[0]).astype(BF16)


def _merge(x, h, ya, yg, yh, wgate, wa, wg, wh, wo, g_post, gt, g_pre, sh, sc, tm):
    B, T, D = x.shape
    row = pl.BlockSpec((1, tm, D), lambda b, i: (b, i, 0))
    vec = pl.BlockSpec((1, 1, D), lambda b, i: (b, 0, 0))
    par = pl.BlockSpec((1, D), lambda b, i: (0, 0))
    return pl.pallas_call(
        _merge_kernel,
        grid=(B, T // tm),
        in_specs=[row] * 5 + [_resident(wgate.shape)] + [_resident(wa.shape)] * 4 + [par, vec, par, vec, vec],
        out_specs=[row, row],
        out_shape=[jax.ShapeDtypeStruct((B, T, D), F32), jax.ShapeDtypeStruct((B, T, D), BF16)],
        compiler_params=_params("parallel", "parallel"),
        name="merge",
    )(x, h, ya, yg, yh, wgate, wa, wg, wh, wo, g_post.reshape(1, D), gt, g_pre.reshape(1, D), sh, sc)


def _ffn_kernel(x_ref, h_ref, hp_ref, hn_ref, wup_ref, cw_ref, cb_ref, wdn_ref, gpost_ref, gt_ref,
                o_ref, acc_ref, *, tm):
    i = pl.program_id(1)
    last = pl.num_programs(1) - 1
    halo = BF16_ROWS
    hp = jnp.where(i > 0, hp_ref[0], jnp.zeros_like(hp_ref[0]))
    hn = jnp.where(i < last, hn_ref[0], jnp.zeros_like(hn_ref[0]))
    h_ext = jnp.concatenate([hp, h_ref[0], hn], axis=0)
    rows = tm + 2 * halo
    acc_ref[...] = jnp.zeros_like(acc_ref)

    def conv(u, part, n):
        cw = cw_ref[part, n]
        u_prev = pltpu.roll(u, 1, 0)
        u_next = pltpu.roll(u, rows - 1, 0)
        return (u_prev * cw[0:1] + u * cw[1:2] + u_next * cw[2:3] + cb_ref[part, n])[halo:halo + tm]

    def body(n, carry):
        a = conv(_dot(h_ext, wup_ref[0, n]), 0, n)
        b = conv(_dot(h_ext, wup_ref[1, n]), 1, n)
        acc_ref[...] += _dot((_silu(a) * b).astype(BF16), wdn_ref[n])
        return carry

    lax.fori_loop(0, D_FF // FF_CHUNK, body, 0)
    o_ref[0] = x_ref[0] + gt_ref[0] * _rms(acc_ref[...], gpost_ref[...])


def _ffn(x, h, wup, cw, cb, wdn, g_post, gt, tm):
    B, T, D = x.shape
    nh = tm // BF16_ROWS
    row = pl.BlockSpec((1, tm, D), lambda b, i: (b, i, 0))
    prev = pl.BlockSpec((1, BF16_ROWS, D), lambda b, i: (b, jnp.maximum(i * nh - 1, 0), 0))
    nxt = pl.BlockSpec((1, BF16_ROWS, D), lambda b, i: (b, jnp.minimum((i + 1) * nh, T // BF16_ROWS - 1), 0))
    return pl.pallas_call(
        functools.partial(_ffn_kernel, tm=tm),
        grid=(B, T // tm),
        in_specs=[row, row, prev, nxt, _resident(wup.shape), _resident(cw.shape), _resident(cb.shape),
                  _resident(wdn.shape), pl.BlockSpec((1, D), lambda b, i: (0, 0)),
                  pl.BlockSpec((1, 1, D), lambda b, i: (b, 0, 0))],
        out_specs=row,
        out_shape=jax.ShapeDtypeStruct((B, T, D), F32),
        scratch_shapes=[pltpu.VMEM((tm, D), F32)],
        compiler_params=_params("parallel", "parallel"),
        name="ffn",
    )(x, h, h, h, wup, cw, cb, wdn, g_post.reshape(1, D), gt)


def _rope_tables(T):
    half = ATT_DH // 4
    t = jnp.arange(T, dtype=jnp.int32)
    rows = (t // GRID_W).astype(F32)
    cols = (t % GRID_W).astype(F32)
    inv = ROPE_BASE ** (-jnp.arange(half, dtype=F32) / half)
    ang_r = rows[:, None] * inv[None, :]
    ang_c = cols[:, None] * inv[None, :]
    z = jnp.zeros_like(ang_r)
    cos64 = jnp.concatenate([jnp.cos(ang_r)] * 2 + [jnp.cos(ang_c)] * 2, axis=1)
    sa64 = jnp.concatenate([-jnp.sin(ang_r), z, -jnp.sin(ang_c), z], axis=1)
    sb64 = jnp.concatenate([z, jnp.sin(ang_r), z, jnp.sin(ang_c)], axis=1)
    return tuple(jnp.concatenate([a, a], axis=1) for a in (cos64, sa64, sb64))


def _col_offsets():
    offs, start = [], 0
    for w in IN_SPLITS:
        offs.append(start)
        start += w
    return offs


def kernel(x, c, ctx, c_ctx, w_ada, b_ada, g_pre_mix, g_post_mix, g_pre_ffn, g_post_ffn, w_in, lam_q1, lam_k1, lam_q2, lam_k2, att_subln_g, gm_ln_g, gm_ln_b, gm_ws, gm_bs, hg_lb, hg_norm_g, w_br_att, w_br_gm, w_br_hg, w_out, w_up, conv_w, conv_b, w_down):
    B, T, D = x.shape
    L = ctx.shape[1]
    TM = 512
    TQ = 512

    rows = 16
    c_all = jnp.concatenate([c, c_ctx[None, :], jnp.zeros((rows - B - 1, D), F32)], axis=0)
    mod = _ada(c_all, w_ada, b_ada)

    lb_all = jnp.cumsum(jax.nn.softmax(hg_lb.astype(F32), axis=0), axis=0)
    lb_all = lb_all - lb_all[0]
    tables = _rope_tables(T)
    offs = _col_offsets()
    nff = D_FF // FF_CHUNK

    xc = ctx
    for l in range(DEPTH):
        last = l == DEPTH - 1
        lam_init = 0.8 - 0.6 * math.exp(-0.3 * l)
        lam = (jnp.exp(jnp.sum(lam_q1[l] * lam_k1[l])) - jnp.exp(jnp.sum(lam_q2[l] * lam_k2[l])) + lam_init)

        def mods(k, ctx_row):
            m = mod[l, :, k * D:(k + 1) * D]
            if ctx_row:
                return jnp.broadcast_to(m[B:B + 1], (B, D)).reshape(B, 1, D)
            return m[:B].reshape(B, 1, D)

        w_l = w_in[l].astype(BF16)
        w_qkv = w_l[:, offs[0]:offs[3]]
        w_uv = w_l[:, offs[3]:offs[5]]
        w_hg = w_l[:, offs[5]:offs[10]]
        w_gate = w_l[:, offs[10]:]
        lb = lb_all[l]
        lb_tab = jnp.stack([jnp.log(lb[0]), jnp.log1p(-lb[0]), 1.0 - lb[0],
                            jnp.log(lb[1]), jnp.log1p(-lb[1]), 1.0 - lb[1],
                            jnp.zeros_like(lb[0]), jnp.zeros_like(lb[0])], axis=0)
        ws = gm_ws[l].astype(BF16)
        dg = GM_WIDTH // GM_GROUPS
        bias_full = jnp.broadcast_to(gm_bs[l].T[:, :, None], (GM_CHUNK, GM_GROUPS, dg)).reshape(GM_CHUNK, GM_WIDTH)
        wa, wg, wh, wo = (w[l].astype(BF16) for w in (w_br_att, w_br_gm, w_br_hg, w_out))
        wup = w_up[l].astype(BF16).reshape(D, 2, nff, FF_CHUNK).transpose(1, 2, 0, 3)
        cw = conv_w[l].reshape(CONV_W, 2, nff, FF_CHUNK).transpose(1, 2, 0, 3)
        cb = conv_b[l].reshape(2, nff, 1, FF_CHUNK)
        wdn = w_down[l].astype(BF16).reshape(nff, FF_CHUNK, D)

        hc = _prenorm(xc, g_pre_mix[l], mods(0, True), mods(1, True), L)
        h = _prenorm(x, g_pre_mix[l], mods(0, False), mods(1, False), TM)
        cq, ck, cv = _att_proj(hc, w_qkv, None, L)
        q, k, v = _att_proj(h, w_qkv, tables, TM)
        c_hg = _hgrn_proj(hc, w_hg, lb_tab, L)
        l_hg = _hgrn_proj(h, w_hg, lb_tab, TM)

        y_att = _attention(q, [(k, v), (ck, cv)], lam, att_subln_g[l], 1.0 - lam_init, TQ)
        y_gm = _gmlp(h, w_uv, gm_ln_g[l], gm_ln_b[l], ws, bias_full, TM)
        y_hg, yc_hg = _hgrn_scan(l_hg, c_hg, hg_norm_g[l])

        x, h2 = _merge(x, h, y_att, y_gm, y_hg, w_gate, wa, wg, wh, wo, g_post_mix[l], mods(2, False),
                       g_pre_ffn[l], mods(3, False), mods(4, False), TM)
        x = _ffn(x, h2, wup, cw, cb, wdn, g_post_ffn[l], mods(5, False), TM)

        if not last:
            yc_att = _attention(cq, [(ck, cv)], lam, att_subln_g[l], 1.0 - lam_init, L)
            yc_gm = _gmlp(hc, w_uv, gm_ln_g[l], gm_ln_b[l], ws, bias_full, L)
            xc, hc2 = _merge(xc, hc, yc_att, yc_gm, yc_hg, w_gate, wa, wg, wh, wo, g_post_mix[l],
                             mods(2, True), g_pre_ffn[l], mods(3, True), mods(4, True), L)
            xc = _ffn(xc, hc2, wup, cw, cb, wdn, g_post_ffn[l], mods(5, True), L)
    return x
```

```python
import functools
import math

import jax
import jax.numpy as jnp
import numpy as np
from jax import lax
from jax.experimental import pallas as pl
from jax.experimental.pallas import tpu as pltpu

F32 = jnp.float32
BF16 = jnp.bfloat16

D_MODEL = 1024
DEPTH = 2
GRID_W = 64
ATT_HEADS = 8
ATT_DH = 64
ATT_DV = 2 * ATT_DH
GM_WIDTH = 1024
GM_GROUPS = 8
GM_CHUNK = 128
HG_HEADS = 8
HG_DK = 128
HG_DV = 128
HG_CHUNK = 64
HG_LEVELS = 6
HG_GROUP = 4
N_BRANCH = 3
D_FF = 2816
FF_CHUNK = 256
CONV_W = 3
ROPE_BASE = 10000.0
EPS = 1e-6

ATT_QW = ATT_HEADS * 2 * ATT_DH
ATT_VW = ATT_HEADS * ATT_DV
HG_KW = HG_HEADS * HG_DK
HG_VW = HG_HEADS * HG_DV
IN_SPLITS = (ATT_QW, ATT_QW, ATT_VW, GM_WIDTH, GM_WIDTH, HG_KW, HG_KW, HG_KW, HG_VW, HG_VW,
             N_BRANCH * D_MODEL)

LANES = 128
BF16_ROWS = 16
MXU_N = 256
VMEM_LIMIT = 56 * 1024 * 1024

NEG_INF = float("-inf")
ATT_Q_SCALE = ATT_DH ** -0.5 * math.log2(math.e)
ATT_KEY_CHUNK = 512


def _params(*sem):
    return pltpu.CompilerParams(dimension_semantics=sem, vmem_limit_bytes=VMEM_LIMIT)


def _resident(shape):
    nd = len(shape)
    return pl.BlockSpec(shape, lambda *_: (0,) * nd, pipeline_mode=pl.Buffered(1))


def _dot(a, b):
    return jnp.dot(a, b, preferred_element_type=F32)


def _dot_nt(a, b):
    return lax.dot_general(a, b, (((1,), (1,)), ((), ())), preferred_element_type=F32)


def _rms(x, g):
    return x * lax.rsqrt(jnp.mean(x * x, axis=-1, keepdims=True) + EPS) * g


def _silu(x):
    return x * jax.nn.sigmoid(x)


def _gelu_tanh(x):
    return 0.5 * x * (1.0 + jnp.tanh(math.sqrt(2.0 / math.pi) * (x + 0.044715 * (x * x * x))))


def _ada_kernel(c_ref, w_ref, b_ref, o_ref):
    c = c_ref[...]
    o_ref[0] = _dot(_silu(c).astype(BF16), w_ref[0].astype(BF16)) + b_ref[0]


def _ada(c_all, w_ada, b_ada):
    rows = c_all.shape[0]
    n = w_ada.shape[-1]
    tn = 1536
    return pl.pallas_call(
        _ada_kernel,
        grid=(DEPTH, n // tn),
        in_specs=[
            pl.BlockSpec((rows, D_MODEL), lambda l, j: (0, 0)),
            pl.BlockSpec((1, D_MODEL, tn), lambda l, j: (l, 0, j)),
            pl.BlockSpec((1, 1, tn), lambda l, j: (l, 0, j)),
        ],
        out_specs=pl.BlockSpec((1, rows, tn), lambda l, j: (l, 0, j)),
        out_shape=jax.ShapeDtypeStruct((DEPTH, rows, n), F32),
        compiler_params=_params("parallel", "parallel"),
        name="ada",
    )(c_all, w_ada, b_ada.reshape(DEPTH, 1, n))


def _prenorm_kernel(x_ref, g_ref, sh_ref, sc_ref, h_ref):
    y = _rms(x_ref[0], g_ref[...])
    h_ref[0] = (y * (1.0 + sc_ref[0]) + sh_ref[0]).astype(BF16)


def _prenorm(x, g, sh, sc, tm):
    B, T, D = x.shape
    row = pl.BlockSpec((1, tm, D), lambda b, i: (b, i, 0))
    vec = pl.BlockSpec((1, 1, D), lambda b, i: (b, 0, 0))
    return pl.pallas_call(
        _prenorm_kernel,
        grid=(B, T // tm),
        in_specs=[row, pl.BlockSpec((1, D), lambda b, i: (0, 0)), vec, vec],
        out_specs=row,
        out_shape=jax.ShapeDtypeStruct((B, T, D), BF16),
        compiler_params=_params("parallel", "parallel"),
        name="prenorm",
    )(x, g.reshape(1, D), sh, sc)


def _att_proj_kernel(h_ref, w_ref, *rest, rope):
    if rope:
        cos_ref, sa_ref, sb_ref, q_ref, k_ref, v_ref = rest
        cos, sa, sb = cos_ref[...], sa_ref[...], sb_ref[...]
    else:
        q_ref, k_ref, v_ref = rest
    h = h_ref[0]

    def rot(a):
        half = ATT_DH // 4
        return a * cos + pltpu.roll(a, LANES - half, 1) * sa + pltpu.roll(a, half, 1) * sb

    for j in range(ATT_QW // MXU_N):
        cols = slice(j * MXU_N, (j + 1) * MXU_N)
        aq = _dot(h, w_ref[:, j * MXU_N:(j + 1) * MXU_N])
        ak = _dot(h, w_ref[:, ATT_QW + j * MXU_N:ATT_QW + (j + 1) * MXU_N])
        av = _dot(h, w_ref[:, 2 * ATT_QW + j * MXU_N:2 * ATT_QW + (j + 1) * MXU_N])
        if rope:
            aq = jnp.concatenate([rot(aq[:, :LANES]), rot(aq[:, LANES:])], axis=1)
            ak = jnp.concatenate([rot(ak[:, :LANES]), rot(ak[:, LANES:])], axis=1)
        q_ref[0, :, cols] = (aq * ATT_Q_SCALE).astype(BF16)
        k_ref[0, :, cols] = ak.astype(BF16)
        v_ref[0, :, cols] = av.astype(BF16)


def _att_proj(h, w_qkv, tables, tm):
    B, T, D = h.shape
    rope = tables is not None
    row = pl.BlockSpec((1, tm, D), lambda b, i: (b, i, 0))
    out = pl.BlockSpec((1, tm, ATT_QW), lambda b, i: (b, i, 0))
    in_specs = [row, _resident(w_qkv.shape)]
    args = [h, w_qkv]
    if rope:
        in_specs += [pl.BlockSpec((tm, LANES), lambda b, i: (i, 0))] * 3
        args += list(tables)
    return pl.pallas_call(
        functools.partial(_att_proj_kernel, rope=rope),
        grid=(B, T // tm),
        in_specs=in_specs,
        out_specs=[out, out, out],
        out_shape=[jax.ShapeDtypeStruct((B, T, ATT_QW), BF16)] * 3,
        compiler_params=_params("parallel", "parallel"),
        name="att_proj",
    )(*args)


def _attn_kernel(lam_ref, q_ref, g_ref, *rest, n_kv, post_scale):
    kv = rest[:2 * n_kv]
    o_ref = rest[2 * n_kv]
    vext = rest[2 * n_kv + 1:]

    @pl.when(pl.program_id(2) == 0)
    def _():
        for j in range(n_kv):
            v = kv[2 * j + 1][0]
            vext[j][:, :LANES] = v
            vext[j][:, LANES:] = jnp.ones_like(v)

    q = q_ref[0]
    lane = lax.broadcasted_iota(jnp.int32, q.shape, 1)
    zero = jnp.zeros_like(q)
    outs = []
    for i in range(2):
        qi = jnp.where(lane < ATT_DH, q, zero) if i == 0 else jnp.where(lane >= ATT_DH, q, zero)
        m = acc = None
        for j in range(n_kv):
            n_keys = kv[2 * j].shape[1]
            kc = min(n_keys, ATT_KEY_CHUNK)
            for c0 in range(0, n_keys, kc):
                s = _dot_nt(qi, kv[2 * j][0, c0:c0 + kc, :])
                mc = jnp.max(s, axis=-1, keepdims=True)
                m_new = mc if m is None else jnp.maximum(m, mc)
                pv = _dot(jnp.exp2(s - m_new).astype(BF16), vext[j][c0:c0 + kc, :])
                acc = pv if m is None else acc * jnp.exp2(m - m_new) + pv
                m = m_new
        outs.append(acc[:, :LANES] / acc[:, LANES:])
    o = outs[0] - lam_ref[0] * outs[1]
    o_ref[0] = (_rms(o, g_ref[...]) * post_scale).astype(BF16)


def _attention(q, kvs, lam, g, post_scale, tq):
    B, T, _ = q.shape
    in_specs = [
        pl.BlockSpec(memory_space=pltpu.SMEM),
        pl.BlockSpec((1, tq, LANES), lambda b, h, i: (b, i, h)),
        pl.BlockSpec((1, LANES), lambda b, h, i: (0, 0)),
    ]
    args = [lam.reshape(1), q, g.reshape(1, ATT_DV)]
    scratch = []
    for k, v in kvs:
        spec = pl.BlockSpec((1, k.shape[1], LANES), lambda b, h, i: (b, 0, h))
        in_specs += [spec, spec]
        args += [k, v]
        scratch.append(pltpu.VMEM((k.shape[1], 2 * LANES), BF16))
    return pl.pallas_call(
        functools.partial(_attn_kernel, n_kv=len(kvs), post_scale=post_scale),
        grid=(B, ATT_HEADS, T // tq),
        in_specs=in_specs,
        out_specs=pl.BlockSpec((1, tq, LANES), lambda b, h, i: (b, i, h)),
        out_shape=jax.ShapeDtypeStruct((B, T, ATT_VW), BF16),
        scratch_shapes=scratch,
        compiler_params=_params("parallel", "parallel", "arbitrary"),
        name="attention",
    )(*args)


def _gmlp_kernel(h_ref, w_ref, lng_ref, lnb_ref, ws_ref, bias_ref, o_ref, *, tm):
    h = h_ref[0]
    u = _gelu_tanh(_dot(h, w_ref[:, :GM_WIDTH]))
    v = _gelu_tanh(_dot(h, w_ref[:, GM_WIDTH:]))
    mu = jnp.mean(v, axis=-1, keepdims=True)
    vc = v - mu
    var = jnp.mean(vc * vc, axis=-1, keepdims=True)
    vn = (vc * lax.rsqrt(var + EPS) * lng_ref[...] + lnb_ref[...]).astype(BF16)
    dg = GM_WIDTH // GM_GROUPS
    for r in range(tm // GM_CHUNK):
        rows = slice(r * GM_CHUNK, (r + 1) * GM_CHUNK)
        for g in range(GM_GROUPS):
            cols = slice(g * dg, (g + 1) * dg)
            s = _dot(ws_ref[g], vn[rows, cols]) + bias_ref[:, cols]
            o_ref[0, rows, cols] = (u[rows, cols] * s).astype(BF16)


def _gmlp(h, w_uv, ln_g, ln_b, ws, bias_full, tm):
    B, T, D = h.shape
    return pl.pallas_call(
        functools.partial(_gmlp_kernel, tm=tm),
        grid=(B, T // tm),
        in_specs=[
            pl.BlockSpec((1, tm, D), lambda b, i: (b, i, 0)),
            _resident(w_uv.shape),
            _resident((1, GM_WIDTH)),
            _resident((1, GM_WIDTH)),
            _resident(ws.shape),
            _resident(bias_full.shape),
        ],
        out_specs=pl.BlockSpec((1, tm, GM_WIDTH), lambda b, i: (b, i, 0)),
        out_shape=jax.ShapeDtypeStruct((B, T, GM_WIDTH), BF16),
        compiler_params=_params("parallel", "parallel"),
        name="gmlp",
    )(h, w_uv, ln_g.reshape(1, GM_WIDTH), ln_b.reshape(1, GM_WIDTH), ws, bias_full)


def _hgrn_proj_kernel(h_ref, w_ref, lb_ref, q_ref, kf_ref, lff_ref, kb_ref, lfb_ref, v_ref, g_ref):
    h = h_ref[0]

    def gates(z, d, cols):
        log_lb = lb_ref[3 * d + 0:3 * d + 1, cols]
        log_1m = lb_ref[3 * d + 1:3 * d + 2, cols]
        one_m = lb_ref[3 * d + 2:3 * d + 3, cols]
        e = jnp.exp(-jnp.abs(z))
        one_p = 1.0 + e
        r = 1.0 / one_p
        log_sig = jnp.minimum(z, 0.0) - jnp.log(one_p)
        b = log_1m + log_sig
        logf = jnp.maximum(log_lb, b) + jnp.log(1.0 + jnp.exp(-jnp.abs(log_lb - b)))
        return logf, one_m * jnp.where(z >= 0.0, e * r, r)

    for j in range(HG_KW // MXU_N):
        cols = slice(j * MXU_N, (j + 1) * MXU_N)

        def proj(seg):
            return _dot(h, w_ref[:, seg * HG_KW + j * MXU_N:seg * HG_KW + (j + 1) * MXU_N])

        q_ref[0, :, cols] = _silu(proj(0)).astype(BF16)
        lf, k = gates(proj(1), 0, cols)
        lff_ref[0, :, cols] = lf
        kf_ref[0, :, cols] = k.astype(BF16)
        lf, k = gates(proj(2), 1, cols)
        lfb_ref[0, :, cols] = lf
        kb_ref[0, :, cols] = k.astype(BF16)
        v_ref[0, :, cols] = proj(3).astype(BF16)
        g_ref[0, :, cols] = _silu(proj(4)).astype(BF16)


def _hgrn_proj(h, w_hg, lb_tab, tm):
    B, T, D = h.shape
    out = pl.BlockSpec((1, tm, HG_KW), lambda b, i: (b, i, 0))
    bf = jax.ShapeDtypeStruct((B, T, HG_KW), BF16)
    f32 = jax.ShapeDtypeStruct((B, T, HG_KW), F32)
    return pl.pallas_call(
        _hgrn_proj_kernel,
        grid=(B, T // tm),
        in_specs=[pl.BlockSpec((1, tm, D), lambda b, i: (b, i, 0)), _resident(w_hg.shape),
                  _resident(lb_tab.shape)],
        out_specs=[out] * 7,
        out_shape=[bf, bf, f32, bf, f32, bf, bf],
        compiler_params=_params("parallel", "parallel"),
        name="hgrn_proj",
    )(h, w_hg, lb_tab)


def _split2(x):
    a = x.astype(BF16)
    return a, (x - a.astype(F32)).astype(BF16)


def _hgrn_decay_matrix():
    C = HG_CHUNK
    t = np.arange(C)[:, None]
    u = np.arange(C)[None, :]
    out = []
    for rev in (False, True):
        blocks = [u >= t, u < t] if rev else [u <= t, u > t]
        for l in range(HG_LEVELS):
            half = 1 << l
            mid = (t & -(2 * half)) + half
            upper = (t & half) != 0
            if rev:
                blocks.append(np.where(upper, (u >= mid) & (u < t), (u >= t) & (u < mid)))
            else:
                blocks.append(np.where(upper, (u >= mid) & (u <= t), (u > t) & (u < mid)))
        m = np.concatenate(blocks, axis=0).astype(np.float32)
        out += [m, m]
    return jnp.asarray(np.concatenate(out, axis=1), dtype=BF16)


def _hgrn_local(blocks, dmat, consts):
    C = HG_CHUNK
    G = range(len(blocks))
    level_of_pair, eye, r_in = consts
    qf = [blocks[g][0].astype(F32) for g in G]
    kf = [(blocks[g][1].astype(F32), blocks[g][2].astype(F32)) for g in G]
    z = jnp.zeros((C, HG_DK), BF16)

    def block_diag(lf_f, lf_b):
        f1, f2 = _split2(lf_f)
        b1, b2 = _split2(lf_b)
        return jnp.concatenate([jnp.concatenate([f1, z], axis=1), jnp.concatenate([f2, z], axis=1),
                                jnp.concatenate([z, b1], axis=1), jnp.concatenate([z, b2], axis=1)], axis=0)

    logs = [_dot(dmat, block_diag(blocks[g][3], blocks[g][4])) for g in G]
    ex = [jnp.exp(logs[g]) for g in G]
    ex = [(ex[g][:, :HG_DK], ex[g][:, HG_DK:]) for g in G]
    q_in = [[(qf[g] * ex[g][d][0:C]).astype(BF16) for d in range(2)] for g in G]
    k_out = [[(kf[g][d] * ex[g][d][C:2 * C]).astype(BF16) for d in range(2)] for g in G]
    g_tot = [[ex[g][0][C - 1:C], ex[g][1][0:1]] for g in G]
    scores = [jnp.where(eye, jnp.sum(qf[g] * (kf[g][0] + kf[g][1]), axis=-1, keepdims=True), 0.0) for g in G]
    for l in range(HG_LEVELS):
        upper = (r_in & (1 << l)) != 0
        rows = slice((2 + l) * C, (3 + l) * C)
        for g in G:
            e = [ex[g][d][rows] for d in range(2)]
            qs = jnp.concatenate([jnp.where(upper, qf[g] * e[0], 0.0), jnp.where(upper, 0.0, qf[g] * e[1])], axis=1)
            ks = jnp.concatenate([jnp.where(upper, 0.0, kf[g][0] * e[0]), jnp.where(upper, kf[g][1] * e[1], 0.0)],
                                 axis=1)
            scores[g] = jnp.where(level_of_pair == l + 1, _dot_nt(qs.astype(BF16), ks.astype(BF16)), scores[g])
    o = [_dot(scores[g].astype(BF16), blocks[g][5]) for g in G]
    return q_in, k_out, g_tot, o


def _hgrn_scan_kernel(g_ref, d_ref, q_ref, kf_ref, lff_ref, kb_ref, lfb_ref, v_ref, og_ref,
                      cq_ref, ckf_ref, clff_ref, ckb_ref, clfb_ref, cv_ref, cog_ref,
                      y_ref, yc_ref, o_ref, qin_ref, kout_ref, gtot_ref, st_ref, *, T, L):
    C = HG_CHUNK
    row_id = lax.broadcasted_iota(jnp.int32, (C, C), 0)
    col_id = lax.broadcasted_iota(jnp.int32, (C, C), 1)
    r_in = lax.broadcasted_iota(jnp.int32, (C, HG_DK), 0)
    diff = row_id ^ col_id
    level_of_pair = functools.reduce(jnp.add, [(diff >= (1 << j)).astype(jnp.int32) for j in range(HG_LEVELS)])
    consts = (level_of_pair, row_id == col_id, r_in)

    st_ref[...] = jnp.zeros_like(st_ref)

    def scan(q_r, k_rs, lf_rs, v_r, n):
        def local(i, carry):
            cs = [i * HG_GROUP + g for g in range(HG_GROUP)]
            rows = [pl.ds(pl.multiple_of(c * C, C), C) for c in cs]
            blocks = [(q_r[0, r, :], k_rs[0][0, r, :], k_rs[1][0, r, :], lf_rs[0][0, r, :], lf_rs[1][0, r, :],
                       v_r[0, r, :]) for r in rows]
            q_in, k_out, g_tot, o = _hgrn_local(blocks, d_ref[...], consts)
            for g, (c, r) in enumerate(zip(cs, rows)):
                for d in range(2):
                    qin_ref[d, r, :] = q_in[g][d]
                    kout_ref[d, r, :] = k_out[g][d]
                    gtot_ref[d, pl.ds(pl.multiple_of(c * 8, 8), 8), :] = jnp.broadcast_to(g_tot[g][d], (8, HG_DK))
                o_ref[r, :] = o[g]
            return carry

        def carried(i, carry):
            for d in range(2):
                c = i if d == 0 else n - 1 - i
                rows = pl.ds(pl.multiple_of(c * C, C), C)
                st = st_ref[d]
                o_ref[rows, :] += _dot_nt(qin_ref[d, rows, :], st.astype(BF16))
                v_t = v_r[0, rows, :].astype(F32).T.astype(BF16)
                g_tot = gtot_ref[d, pl.ds(pl.multiple_of(c * 8, 8), 8), :][0:1]
                st_ref[d] = st * g_tot + _dot(v_t, kout_ref[d, rows, :])
            return carry

        lax.fori_loop(0, n // HG_GROUP, local, 0)
        lax.fori_loop(0, n, carried, 0, unroll=4)

    def finish(n_rows, og_r, y_r):
        y_r[0] = (_rms(o_ref[0:n_rows, :], g_ref[...]) * og_r[0].astype(F32)).astype(BF16)

    scan(cq_ref, (ckf_ref, ckb_ref), (clff_ref, clfb_ref), cv_ref, L // C)
    finish(L, cog_ref, yc_ref)
    scan(q_ref, (kf_ref, kb_ref), (lff_ref, lfb_ref), v_ref, T // C)
    finish(T, og_ref, y_ref)


def _hgrn_scan(lat, ctx, g):
    B, T, _ = lat[0].shape
    L = ctx[0].shape[1]
    assert T % (HG_CHUNK * HG_GROUP) == 0 and L % (HG_CHUNK * HG_GROUP) == 0
    dmat = _hgrn_decay_matrix()
    lat_spec = pl.BlockSpec((1, T, LANES), lambda b, h: (b, 0, h))
    ctx_spec = pl.BlockSpec((1, L, LANES), lambda b, h: (b, 0, h))
    return pl.pallas_call(
        functools.partial(_hgrn_scan_kernel, T=T, L=L),
        grid=(B, HG_HEADS),
        in_specs=[pl.BlockSpec((1, HG_DV), lambda b, h: (0, 0)), _resident(dmat.shape)]
        + [lat_spec] * 7 + [ctx_spec] * 7,
        out_specs=[lat_spec, ctx_spec],
        out_shape=[jax.ShapeDtypeStruct((B, T, HG_VW), BF16), jax.ShapeDtypeStruct((B, L, HG_VW), BF16)],
        scratch_shapes=[
            pltpu.VMEM((T, HG_DV), F32),
            pltpu.VMEM((2, T, HG_DK), BF16),
            pltpu.VMEM((2, T, HG_DK), BF16),
            pltpu.VMEM((2, T // HG_CHUNK * 8, HG_DK), F32),
            pltpu.VMEM((2, HG_DV, HG_DK), F32),
        ],
        compiler_params=_params("parallel", "parallel"),
        name="hgrn_scan",
    )(g.reshape(1, HG_DV), dmat, *lat, *ctx)


def _merge_kernel(x_ref, h_ref, ya_ref, yg_ref, yh_ref, wgate_ref, wa_ref, wg_ref, wh_ref, wo_ref,
                  gpost_ref, gt_ref, gpre_ref, sh_ref, sc_ref, x1_ref, h2_ref):
    h = h_ref[0]
    D = D_MODEL
    y = jax.nn.sigmoid(_dot(h, wgate_ref[:, :D])) * _dot(ya_ref[0], wa_ref[...])
    y = y + jax.nn.sigmoid(_dot(h, wgate_ref[:, D:2 * D])) * _dot(yg_ref[0], wg_ref[...])
    y = y + jax.nn.sigmoid(_dot(h, wgate_ref[:, 2 * D:])) * _dot(yh_ref[0], wh_ref[...])
    z = _dot(y.astype(BF16), wo_ref[...])
    x1 = x_ref[0] + gt_ref[0] * _rms(z, gpost_ref[...])
    x1_ref[0] = x1
    h2_ref[0] = (_rms(x1, gpre_ref[...]) * (1.0 + sc_ref[0]) + sh_ref[0]).astype(BF16)


def _merge(x, h, ya, yg, yh, wgate, wa, wg, wh, wo, g_post, gt, g_pre, sh, sc, tm):
    B, T, D = x.shape
    row = pl.BlockSpec((1, tm, D), lambda b, i: (b, i, 0))
    vec = pl.BlockSpec((1, 1, D), lambda b, i: (b, 0, 0))
    par = pl.BlockSpec((1, D), lambda b, i: (0, 0))
    return pl.pallas_call(
        _merge_kernel,
        grid=(B, T // tm),
        in_specs=[row] * 5 + [_resident(wgate.shape)] + [_resident(wa.shape)] * 4 + [par, vec, par, vec, vec],
        out_specs=[row, row],
        out_shape=[jax.ShapeDtypeStruct((B, T, D), F32), jax.ShapeDtypeStruct((B, T, D), BF16)],
        compiler_params=_params("parallel", "parallel"),
        name="merge",
    )(x, h, ya, yg, yh, wgate, wa, wg, wh, wo, g_post.reshape(1, D), gt, g_pre.reshape(1, D), sh, sc)


def _ffn_kernel(x_ref, h_ref, hp_ref, hn_ref, wup_ref, cw_ref, cb_ref, wdn_ref, gpost_ref, gt_ref,
                o_ref, act_ref, *, tm):
    i = pl.program_id(1)
    last = pl.num_programs(1) - 1
    halo = BF16_ROWS
    hp = jnp.where(i > 0, hp_ref[0], jnp.zeros_like(hp_ref[0]))
    hn = jnp.where(i < last, hn_ref[0], jnp.zeros_like(hn_ref[0]))
    h_ext = jnp.concatenate([hp, h_ref[0], hn], axis=0)
    rows = tm + 2 * halo

    def conv(u, part, n):
        cw = cw_ref[part, n]
        u_prev = pltpu.roll(u, 1, 0)
        u_next = pltpu.roll(u, rows - 1, 0)
        return (u_prev * cw[0:1] + u * cw[1:2] + u_next * cw[2:3] + cb_ref[part, n])[halo:halo + tm]

    for n in range(D_FF // FF_CHUNK):
        a = conv(_dot(h_ext, wup_ref[0, n]), 0, n)
        b = conv(_dot(h_ext, wup_ref[1, n]), 1, n)
        act_ref[:, n * FF_CHUNK:(n + 1) * FF_CHUNK] = (_silu(a) * b).astype(BF16)
    y = _dot(act_ref[...], wdn_ref[...])
    o_ref[0] = x_ref[0] + gt_ref[0] * _rms(y, gpost_ref[...])


def _ffn(x, h, wup, cw, cb, wdn, g_post, gt, tm):
    B, T, D = x.shape
    nh = tm // BF16_ROWS
    row = pl.BlockSpec((1, tm, D), lambda b, i: (b, i, 0))
    prev = pl.BlockSpec((1, BF16_ROWS, D), lambda b, i: (b, jnp.maximum(i * nh - 1, 0), 0))
    nxt = pl.BlockSpec((1, BF16_ROWS, D), lambda b, i: (b, jnp.minimum((i + 1) * nh, T // BF16_ROWS - 1), 0))
    return pl.pallas_call(
        functools.partial(_ffn_kernel, tm=tm),
        grid=(B, T // tm),
        in_specs=[row, row, prev, nxt, _resident(wup.shape), _resident(cw.shape), _resident(cb.shape),
                  _resident(wdn.shape), pl.BlockSpec((1, D), lambda b, i: (0, 0)),
                  pl.BlockSpec((1, 1, D), lambda b, i: (b, 0, 0))],
        out_specs=row,
        out_shape=jax.ShapeDtypeStruct((B, T, D), F32),
        scratch_shapes=[pltpu.VMEM((tm, D_FF), BF16)],
        compiler_params=_params("parallel", "parallel"),
        name="ffn",
    )(x, h, h, h, wup, cw, cb, wdn, g_post.reshape(1, D), gt)


def _rope_tables(T):
    half = ATT_DH // 4
    t = jnp.arange(T, dtype=jnp.int32)
    rows = (t // GRID_W).astype(F32)
    cols = (t % GRID_W).astype(F32)
    inv = ROPE_BASE ** (-jnp.arange(half, dtype=F32) / half)
    ang_r = rows[:, None] * inv[None, :]
    ang_c = cols[:, None] * inv[None, :]
    z = jnp.zeros_like(ang_r)
    cos64 = jnp.concatenate([jnp.cos(ang_r)] * 2 + [jnp.cos(ang_c)] * 2, axis=1)
    sa64 = jnp.concatenate([-jnp.sin(ang_r), z, -jnp.sin(ang_c), z], axis=1)
    sb64 = jnp.concatenate([z, jnp.sin(ang_r), z, jnp.sin(ang_c)], axis=1)
    return tuple(jnp.concatenate([a, a], axis=1) for a in (cos64, sa64, sb64))


def _col_offsets():
    offs, start = [], 0
    for w in IN_SPLITS:
        offs.append(start)
        start += w
    return offs


def kernel(x, c, ctx, c_ctx, w_ada, b_ada, g_pre_mix, g_post_mix, g_pre_ffn, g_post_ffn, w_in, lam_q1, lam_k1, lam_q2, lam_k2, att_subln_g, gm_ln_g, gm_ln_b, gm_ws, gm_bs, hg_lb, hg_norm_g, w_br_att, w_br_gm, w_br_hg, w_out, w_up, conv_w, conv_b, w_down):
    B, T, D = x.shape
    L = ctx.shape[1]
    TM = 512
    TQ = 512

    rows = 16
    c_all = jnp.concatenate([c, c_ctx[None, :], jnp.zeros((rows - B - 1, D), F32)], axis=0)
    mod = _ada(c_all, w_ada, b_ada)

    lb_all = jnp.cumsum(jax.nn.softmax(hg_lb.astype(F32), axis=0), axis=0)
    lb_all = lb_all - lb_all[0]
    tables = _rope_tables(T)
    offs = _col_offsets()
    nff = D_FF // FF_CHUNK

    xc = ctx
    for l in range(DEPTH):
        last = l == DEPTH - 1
        lam_init = 0.8 - 0.6 * math.exp(-0.3 * l)
        lam = (jnp.exp(jnp.sum(lam_q1[l] * lam_k1[l])) - jnp.exp(jnp.sum(lam_q2[l] * lam_k2[l])) + lam_init)

        def mods(k, ctx_row):
            m = mod[l, :, k * D:(k + 1) * D]
            if ctx_row:
                return jnp.broadcast_to(m[B:B + 1], (B, D)).reshape(B, 1, D)
            return m[:B].reshape(B, 1, D)

        w_l = w_in[l].astype(BF16)
        w_qkv = w_l[:, offs[0]:offs[3]]
        w_uv = w_l[:, offs[3]:offs[5]]
        w_hg = w_l[:, offs[5]:offs[10]]
        w_gate = w_l[:, offs[10]:]
        lb = lb_all[l]
        lb_tab = jnp.stack([jnp.log(lb[0]), jnp.log1p(-lb[0]), 1.0 - lb[0],
                            jnp.log(lb[1]), jnp.log1p(-lb[1]), 1.0 - lb[1],
                            jnp.zeros_like(lb[0]), jnp.zeros_like(lb[0])], axis=0)
        ws = gm_ws[l].astype(BF16)
        dg = GM_WIDTH // GM_GROUPS
        bias_full = jnp.broadcast_to(gm_bs[l].T[:, :, None], (GM_CHUNK, GM_GROUPS, dg)).reshape(GM_CHUNK, GM_WIDTH)
        wa, wg, wh, wo = (w[l].astype(BF16) for w in (w_br_att, w_br_gm, w_br_hg, w_out))
        wup = w_up[l].astype(BF16).reshape(D, 2, nff, FF_CHUNK).transpose(1, 2, 0, 3)
        cw = conv_w[l].reshape(CONV_W, 2, nff, FF_CHUNK).transpose(1, 2, 0, 3)
        cb = conv_b[l].reshape(2, nff, 1, FF_CHUNK)
        wdn = w_down[l].astype(BF16)

        hc = _prenorm(xc, g_pre_mix[l], mods(0, True), mods(1, True), L)
        h = _prenorm(x, g_pre_mix[l], mods(0, False), mods(1, False), TM)
        cq, ck, cv = _att_proj(hc, w_qkv, None, L)
        q, k, v = _att_proj(h, w_qkv, tables, TM)
        c_hg = _hgrn_proj(hc, w_hg, lb_tab, L)
        l_hg = _hgrn_proj(h, w_hg, lb_tab, TM)

        y_att = _attention(q, [(k, v), (ck, cv)], lam, att_subln_g[l], 1.0 - lam_init, TQ)
        y_gm = _gmlp(h, w_uv, gm_ln_g[l], gm_ln_b[l], ws, bias_full, TM)
        y_hg, yc_hg = _hgrn_scan(l_hg, c_hg, hg_norm_g[l])

        x, h2 = _merge(x, h, y_att, y_gm, y_hg, w_gate, wa, wg, wh, wo, g_post_mix[l], mods(2, False),
                       g_pre_ffn[l], mods(3, False), mods(4, False), TM)
        x = _ffn(x, h2, wup, cw, cb, wdn, g_post_ffn[l], mods(5, False), TM)

        if not last:
            yc_att = _attention(cq, [(ck, cv)], lam, att_subln_g[l], 1.0 - lam_init, L)
            yc_gm = _gmlp(hc, w_uv, gm_ln_g[l], gm_ln_b[l], ws, bias_full, L)
            xc, hc2 = _merge(xc, hc, yc_att, yc_gm, yc_hg, w_gate, wa, wg, wh, wo, g_post_mix[l],
                             mods(2, True), g_pre_ffn[l], mods(3, True), mods(4, True), L)
            xc = _ffn(xc, hc2, wup, cw, cb, wdn, g_post_ffn[l], mods(5, True), L)
    return x
```

```python
import functools
import math

import jax
import jax.numpy as jnp
import numpy as np
from jax import lax
from jax.experimental import pallas as pl
from jax.experimental.pallas import tpu as pltpu

F32 = jnp.float32
BF16 = jnp.bfloat16

D_MODEL = 1024
DEPTH = 2
GRID_W = 64
ATT_HEADS = 8
ATT_DH = 64
ATT_DV = 2 * ATT_DH
GM_WIDTH = 1024
GM_GROUPS = 8
GM_CHUNK = 128
HG_HEADS = 8
HG_DK = 128
HG_DV = 128
HG_CHUNK = 64
HG_LEVELS = 6
HG_GROUP = 8
HG_OUT_GROUP = 16
N_BRANCH = 3
D_FF = 2816
FF_CHUNK = 256
CONV_W = 3
ROPE_BASE = 10000.0
EPS = 1e-6

ATT_QW = ATT_HEADS * 2 * ATT_DH
ATT_VW = ATT_HEADS * ATT_DV
HG_KW = HG_HEADS * HG_DK
HG_VW = HG_HEADS * HG_DV
IN_SPLITS = (ATT_QW, ATT_QW, ATT_VW, GM_WIDTH, GM_WIDTH, HG_KW, HG_KW, HG_KW, HG_VW, HG_VW,
             N_BRANCH * D_MODEL)

LANES = 128
BF16_ROWS = 16
MXU_N = 256
VMEM_LIMIT = 56 * 1024 * 1024

NEG_INF = float("-inf")
ATT_Q_SCALE = ATT_DH ** -0.5 * math.log2(math.e)
ATT_KEY_CHUNK = 512


def _params(*sem):
    return pltpu.CompilerParams(dimension_semantics=sem, vmem_limit_bytes=VMEM_LIMIT)


def _resident(shape):
    nd = len(shape)
    return pl.BlockSpec(shape, lambda *_: (0,) * nd, pipeline_mode=pl.Buffered(1))


def _dot(a, b):
    return jnp.dot(a, b, preferred_element_type=F32)


def _dot_nt(a, b):
    return lax.dot_general(a, b, (((1,), (1,)), ((), ())), preferred_element_type=F32)


def _rms(x, g):
    return x * lax.rsqrt(jnp.mean(x * x, axis=-1, keepdims=True) + EPS) * g


def _silu(x):
    return x * jax.nn.sigmoid(x)


def _gelu_tanh(x):
    return 0.5 * x * (1.0 + jnp.tanh(math.sqrt(2.0 / math.pi) * (x + 0.044715 * (x * x * x))))


def _ada_kernel(c_ref, w_ref, b_ref, o_ref):
    c = c_ref[...]
    o_ref[0] = _dot(_silu(c).astype(BF16), w_ref[0].astype(BF16)) + b_ref[0]


def _ada(c_all, w_ada, b_ada):
    rows = c_all.shape[0]
    n = w_ada.shape[-1]
    tn = 1536
    return pl.pallas_call(
        _ada_kernel,
        grid=(DEPTH, n // tn),
        in_specs=[
            pl.BlockSpec((rows, D_MODEL), lambda l, j: (0, 0)),
            pl.BlockSpec((1, D_MODEL, tn), lambda l, j: (l, 0, j)),
            pl.BlockSpec((1, 1, tn), lambda l, j: (l, 0, j)),
        ],
        out_specs=pl.BlockSpec((1, rows, tn), lambda l, j: (l, 0, j)),
        out_shape=jax.ShapeDtypeStruct((DEPTH, rows, n), F32),
        compiler_params=_params("parallel", "parallel"),
        name="ada",
    )(c_all, w_ada, b_ada.reshape(DEPTH, 1, n))


def _prenorm_kernel(x_ref, g_ref, sh_ref, sc_ref, h_ref):
    y = _rms(x_ref[0], g_ref[...])
    h_ref[0] = (y * (1.0 + sc_ref[0]) + sh_ref[0]).astype(BF16)


def _prenorm(x, g, sh, sc, tm):
    B, T, D = x.shape
    row = pl.BlockSpec((1, tm, D), lambda b, i: (b, i, 0))
    vec = pl.BlockSpec((1, 1, D), lambda b, i: (b, 0, 0))
    return pl.pallas_call(
        _prenorm_kernel,
        grid=(B, T // tm),
        in_specs=[row, pl.BlockSpec((1, D), lambda b, i: (0, 0)), vec, vec],
        out_specs=row,
        out_shape=jax.ShapeDtypeStruct((B, T, D), BF16),
        compiler_params=_params("parallel", "parallel"),
        name="prenorm",
    )(x, g.reshape(1, D), sh, sc)


def _att_proj_kernel(h_ref, w_ref, *rest, rope):
    if rope:
        cos_ref, sa_ref, sb_ref, q_ref, k_ref, v_ref = rest
        cos, sa, sb = cos_ref[...], sa_ref[...], sb_ref[...]
    else:
        q_ref, k_ref, v_ref = rest
    h = h_ref[0]

    def rot(a):
        half = ATT_DH // 4
        return a * cos + pltpu.roll(a, LANES - half, 1) * sa + pltpu.roll(a, half, 1) * sb

    for j in range(ATT_QW // MXU_N):
        cols = slice(j * MXU_N, (j + 1) * MXU_N)
        aq = _dot(h, w_ref[:, j * MXU_N:(j + 1) * MXU_N])
        ak = _dot(h, w_ref[:, ATT_QW + j * MXU_N:ATT_QW + (j + 1) * MXU_N])
        av = _dot(h, w_ref[:, 2 * ATT_QW + j * MXU_N:2 * ATT_QW + (j + 1) * MXU_N])
        if rope:
            aq = jnp.concatenate([rot(aq[:, :LANES]), rot(aq[:, LANES:])], axis=1)
            ak = jnp.concatenate([rot(ak[:, :LANES]), rot(ak[:, LANES:])], axis=1)
        q_ref[0, :, cols] = (aq * ATT_Q_SCALE).astype(BF16)
        k_ref[0, :, cols] = ak.astype(BF16)
        v_ref[0, :, cols] = av.astype(BF16)


def _att_proj(h, w_qkv, tables, tm):
    B, T, D = h.shape
    rope = tables is not None
    row = pl.BlockSpec((1, tm, D), lambda b, i: (b, i, 0))
    out = pl.BlockSpec((1, tm, ATT_QW), lambda b, i: (b, i, 0))
    in_specs = [row, _resident(w_qkv.shape)]
    args = [h, w_qkv]
    if rope:
        in_specs += [pl.BlockSpec((tm, LANES), lambda b, i: (i, 0))] * 3
        args += list(tables)
    return pl.pallas_call(
        functools.partial(_att_proj_kernel, rope=rope),
        grid=(B, T // tm),
        in_specs=in_specs,
        out_specs=[out, out, out],
        out_shape=[jax.ShapeDtypeStruct((B, T, ATT_QW), BF16)] * 3,
        compiler_params=_params("parallel", "parallel"),
        name="att_proj",
    )(*args)


def _attn_kernel(lam_ref, q_ref, g_ref, *rest, n_kv, post_scale):
    kv = rest[:2 * n_kv]
    o_ref = rest[2 * n_kv]
    vext = rest[2 * n_kv + 1:]

    @pl.when(pl.program_id(2) == 0)
    def _():
        for j in range(n_kv):
            v = kv[2 * j + 1][0]
            vext[j][:, :LANES] = v
            vext[j][:, LANES:] = jnp.ones_like(v)

    q = q_ref[0]
    lane = lax.broadcasted_iota(jnp.int32, q.shape, 1)
    zero = jnp.zeros_like(q)
    outs = []
    for i in range(2):
        qi = jnp.where(lane < ATT_DH, q, zero) if i == 0 else jnp.where(lane >= ATT_DH, q, zero)
        m = acc = None
        for j in range(n_kv):
            n_keys = kv[2 * j].shape[1]
            kc = min(n_keys, ATT_KEY_CHUNK)
            for c0 in range(0, n_keys, kc):
                s = _dot_nt(qi, kv[2 * j][0, c0:c0 + kc, :])
                mc = jnp.max(s, axis=-1, keepdims=True)
                m_new = mc if m is None else jnp.maximum(m, mc)
                pv = _dot(jnp.exp2(s - m_new).astype(BF16), vext[j][c0:c0 + kc, :])
                acc = pv if m is None else acc * jnp.exp2(m - m_new) + pv
                m = m_new
        outs.append(acc[:, :LANES] / acc[:, LANES:])
    o = outs[0] - lam_ref[0] * outs[1]
    o_ref[0] = (_rms(o, g_ref[...]) * post_scale).astype(BF16)


def _attention(q, kvs, lam, g, post_scale, tq):
    B, T, _ = q.shape
    in_specs = [
        pl.BlockSpec(memory_space=pltpu.SMEM),
        pl.BlockSpec((1, tq, LANES), lambda b, h, i: (b, i, h)),
        pl.BlockSpec((1, LANES), lambda b, h, i: (0, 0)),
    ]
    args = [lam.reshape(1), q, g.reshape(1, ATT_DV)]
    scratch = []
    for k, v in kvs:
        spec = pl.BlockSpec((1, k.shape[1], LANES), lambda b, h, i: (b, 0, h))
        in_specs += [spec, spec]
        args += [k, v]
        scratch.append(pltpu.VMEM((k.shape[1], 2 * LANES), BF16))
    return pl.pallas_call(
        functools.partial(_attn_kernel, n_kv=len(kvs), post_scale=post_scale),
        grid=(B, ATT_HEADS, T // tq),
        in_specs=in_specs,
        out_specs=pl.BlockSpec((1, tq, LANES), lambda b, h, i: (b, i, h)),
        out_shape=jax.ShapeDtypeStruct((B, T, ATT_VW), BF16),
        scratch_shapes=scratch,
        compiler_params=_params("parallel", "parallel", "arbitrary"),
        name="attention",
    )(*args)


def _gmlp_kernel(h_ref, w_ref, lng_ref, lnb_ref, ws_ref, bias_ref, o_ref, *, tm):
    h = h_ref[0]
    u = _gelu_tanh(_dot(h, w_ref[:, :GM_WIDTH]))
    v = _gelu_tanh(_dot(h, w_ref[:, GM_WIDTH:]))
    mu = jnp.mean(v, axis=-1, keepdims=True)
    vc = v - mu
    var = jnp.mean(vc * vc, axis=-1, keepdims=True)
    vn = (vc * lax.rsqrt(var + EPS) * lng_ref[...] + lnb_ref[...]).astype(BF16)
    dg = GM_WIDTH // GM_GROUPS
    for r in range(tm // GM_CHUNK):
        rows = slice(r * GM_CHUNK, (r + 1) * GM_CHUNK)
        for g in range(GM_GROUPS):
            cols = slice(g * dg, (g + 1) * dg)
            s = _dot(ws_ref[g], vn[rows, cols]) + bias_ref[:, cols]
            o_ref[0, rows, cols] = (u[rows, cols] * s).astype(BF16)


def _gmlp(h, w_uv, ln_g, ln_b, ws, bias_full, tm):
    B, T, D = h.shape
    return pl.pallas_call(
        functools.partial(_gmlp_kernel, tm=tm),
        grid=(B, T // tm),
        in_specs=[
            pl.BlockSpec((1, tm, D), lambda b, i: (b, i, 0)),
            _resident(w_uv.shape),
            _resident((1, GM_WIDTH)),
            _resident((1, GM_WIDTH)),
            _resident(ws.shape),
            _resident(bias_full.shape),
        ],
        out_specs=pl.BlockSpec((1, tm, GM_WIDTH), lambda b, i: (b, i, 0)),
        out_shape=jax.ShapeDtypeStruct((B, T, GM_WIDTH), BF16),
        compiler_params=_params("parallel", "parallel"),
        name="gmlp",
    )(h, w_uv, ln_g.reshape(1, GM_WIDTH), ln_b.reshape(1, GM_WIDTH), ws, bias_full)


def _hgrn_proj_kernel(h_ref, w_ref, lb_ref, q_ref, kf_ref, lff_ref, kb_ref, lfb_ref, v_ref, g_ref):
    h = h_ref[0]

    def gates(z, d, cols):
        log_lb = lb_ref[3 * d + 0:3 * d + 1, cols]
        log_1m = lb_ref[3 * d + 1:3 * d + 2, cols]
        one_m = lb_ref[3 * d + 2:3 * d + 3, cols]
        e = jnp.exp(-jnp.abs(z))
        one_p = 1.0 + e
        r = 1.0 / one_p
        log_sig = jnp.minimum(z, 0.0) - jnp.log(one_p)
        b = log_1m + log_sig
        logf = jnp.maximum(log_lb, b) + jnp.log(1.0 + jnp.exp(-jnp.abs(log_lb - b)))
        return logf, one_m * jnp.where(z >= 0.0, e * r, r)

    for j in range(HG_KW // MXU_N):
        cols = slice(j * MXU_N, (j + 1) * MXU_N)

        def proj(seg):
            return _dot(h, w_ref[:, seg * HG_KW + j * MXU_N:seg * HG_KW + (j + 1) * MXU_N])

        q_ref[0, :, cols] = _silu(proj(0)).astype(BF16)
        lf, k = gates(proj(1), 0, cols)
        lff_ref[0, :, cols] = lf
        kf_ref[0, :, cols] = k.astype(BF16)
        lf, k = gates(proj(2), 1, cols)
        lfb_ref[0, :, cols] = lf
        kb_ref[0, :, cols] = k.astype(BF16)
        v_ref[0, :, cols] = proj(3).astype(BF16)
        g_ref[0, :, cols] = _silu(proj(4)).astype(BF16)


def _hgrn_proj(h, w_hg, lb_tab, tm):
    B, T, D = h.shape
    out = pl.BlockSpec((1, tm, HG_KW), lambda b, i: (b, i, 0))
    bf = jax.ShapeDtypeStruct((B, T, HG_KW), BF16)
    f32 = jax.ShapeDtypeStruct((B, T, HG_KW), F32)
    return pl.pallas_call(
        _hgrn_proj_kernel,
        grid=(B, T // tm),
        in_specs=[pl.BlockSpec((1, tm, D), lambda b, i: (b, i, 0)), _resident(w_hg.shape),
                  _resident(lb_tab.shape)],
        out_specs=[out] * 7,
        out_shape=[bf, bf, f32, bf, f32, bf, bf],
        compiler_params=_params("parallel", "parallel"),
        name="hgrn_proj",
    )(h, w_hg, lb_tab)


def _split2(x):
    a = x.astype(BF16)
    return a, (x - a.astype(F32)).astype(BF16)


def _hgrn_decay_matrix():
    C = HG_CHUNK
    t = np.arange(C)[:, None]
    u = np.arange(C)[None, :]
    out = []
    for rev in (False, True):
        blocks = [u >= t, u < t] if rev else [u <= t, u > t]
        for l in range(HG_LEVELS):
            half = 1 << l
            mid = (t & -(2 * half)) + half
            upper = (t & half) != 0
            if rev:
                blocks.append(np.where(upper, (u >= mid) & (u < t), (u >= t) & (u < mid)))
            else:
                blocks.append(np.where(upper, (u >= mid) & (u <= t), (u > t) & (u < mid)))
        m = np.concatenate(blocks, axis=0).astype(np.float32)
        out += [m, m]
    return jnp.asarray(np.concatenate(out, axis=1), dtype=BF16)


def _hgrn_local(blocks, dmat, consts):
    C = HG_CHUNK
    G = range(len(blocks))
    level_of_pair, eye, r_in = consts
    qf = [blocks[g][0].astype(F32) for g in G]
    kf = [(blocks[g][1].astype(F32), blocks[g][2].astype(F32)) for g in G]
    z = jnp.zeros((C, HG_DK), BF16)

    def block_diag(lf_f, lf_b):
        f1, f2 = _split2(lf_f)
        b1, b2 = _split2(lf_b)
        return jnp.concatenate([jnp.concatenate([f1, z], axis=1), jnp.concatenate([f2, z], axis=1),
                                jnp.concatenate([z, b1], axis=1), jnp.concatenate([z, b2], axis=1)], axis=0)

    logs = [_dot(dmat, block_diag(blocks[g][3], blocks[g][4])) for g in G]
    ex = [jnp.exp(logs[g]) for g in G]
    ex = [(ex[g][:, :HG_DK], ex[g][:, HG_DK:]) for g in G]
    q_in = [[(qf[g] * ex[g][d][0:C]).astype(BF16) for d in range(2)] for g in G]
    k_out = [[(kf[g][d] * ex[g][d][C:2 * C]).astype(BF16) for d in range(2)] for g in G]
    g_tot = [[ex[g][0][C - 1:C], ex[g][1][0:1]] for g in G]
    scores = [jnp.where(eye, jnp.sum(qf[g] * (kf[g][0] + kf[g][1]), axis=-1, keepdims=True), 0.0) for g in G]
    for l in range(HG_LEVELS):
        upper = (r_in & (1 << l)) != 0
        rows = slice((2 + l) * C, (3 + l) * C)
        for g in G:
            e = [ex[g][d][rows] for d in range(2)]
            qs = jnp.concatenate([jnp.where(upper, qf[g] * e[0], 0.0), jnp.where(upper, 0.0, qf[g] * e[1])], axis=1)
            ks = jnp.concatenate([jnp.where(upper, 0.0, kf[g][0] * e[0]), jnp.where(upper, kf[g][1] * e[1], 0.0)],
                                 axis=1)
            scores[g] = jnp.where(level_of_pair == l + 1, _dot_nt(qs.astype(BF16), ks.astype(BF16)), scores[g])
    o = [_dot(scores[g].astype(BF16), blocks[g][5]) for g in G]
    upd = [_dot(blocks[g][5].astype(F32).T.astype(BF16), jnp.concatenate(k_out[g], axis=1)) for g in G]
    return [jnp.concatenate(q_in[g], axis=1) for g in G], upd, g_tot, o


def _hgrn_scan_kernel(g_ref, d_ref, q_ref, kf_ref, lff_ref, kb_ref, lfb_ref, v_ref, og_ref,
                      cq_ref, ckf_ref, clff_ref, ckb_ref, clfb_ref, cv_ref, cog_ref,
                      y_ref, yc_ref, o_ref, qin_ref, upd_ref, hist_ref, gtot_ref, st_ref, *, T, L):
    C = HG_CHUNK
    row_id = lax.broadcasted_iota(jnp.int32, (C, C), 0)
    col_id = lax.broadcasted_iota(jnp.int32, (C, C), 1)
    r_in = lax.broadcasted_iota(jnp.int32, (C, HG_DK), 0)
    diff = row_id ^ col_id
    level_of_pair = functools.reduce(jnp.add, [(diff >= (1 << j)).astype(jnp.int32) for j in range(HG_LEVELS)])
    consts = (level_of_pair, row_id == col_id, r_in)

    st_ref[...] = jnp.zeros_like(st_ref)

    def block_rows(c):
        return pl.ds(pl.multiple_of(c * C, C), C)

    def scan(q_r, k_rs, lf_rs, v_r, og_r, y_r, n):
        def local(i, carry):
            cs = [i * group + g for g in range(group)]
            blocks = [(q_r[0, block_rows(c), :], k_rs[0][0, block_rows(c), :], k_rs[1][0, block_rows(c), :],
                       lf_rs[0][0, block_rows(c), :], lf_rs[1][0, block_rows(c), :], v_r[0, block_rows(c), :])
                      for c in cs]
            q_in, upd, g_tot, o = _hgrn_local(blocks, d_ref[...], consts)
            for g, c in enumerate(cs):
                qin_ref[block_rows(c), :] = q_in[g]
                upd_ref[c] = upd[g]
                for d in range(2):
                    gtot_ref[d, pl.ds(pl.multiple_of(c * 8, 8), 8), :] = jnp.broadcast_to(g_tot[g][d], (8, HG_DK))
                o_ref[block_rows(c), :] = o[g]
            return carry

        def carry_states(i, carry):
            for d in range(2):
                c = i if d == 0 else n - 1 - i
                lanes = slice(d * HG_DK, (d + 1) * HG_DK)
                st = st_ref[d]
                hist_ref[c, :, lanes] = st.astype(BF16)
                g_tot = gtot_ref[d, pl.ds(pl.multiple_of(c * 8, 8), 8), :][0:1]
                st_ref[d] = st * g_tot + upd_ref[c, :, lanes]
            return carry

        group = min(n, HG_GROUP)
        out_group = min(n, HG_OUT_GROUP)

        def outputs(i, carry):
            cs = [i * out_group + g for g in range(out_group)]
            o = [o_ref[block_rows(c), :] + _dot_nt(qin_ref[block_rows(c), :], hist_ref[c]) for c in cs]
            for c, o_c in zip(cs, o):
                y_r[0, block_rows(c), :] = (_rms(o_c, g_ref[...]) * og_r[0, block_rows(c), :].astype(F32)
                                            ).astype(BF16)
            return carry

        lax.fori_loop(0, n // group, local, 0)
        lax.fori_loop(0, n, carry_states, 0, unroll=4)
        lax.fori_loop(0, n // out_group, outputs, 0)

    scan(cq_ref, (ckf_ref, ckb_ref), (clff_ref, clfb_ref), cv_ref, cog_ref, yc_ref, L // C)
    scan(q_ref, (kf_ref, kb_ref), (lff_ref, lfb_ref), v_ref, og_ref, y_ref, T // C)


def _hgrn_scan(lat, ctx, g):
    B, T, _ = lat[0].shape
    L = ctx[0].shape[1]
    for n in (T // HG_CHUNK, L // HG_CHUNK):
        assert n * HG_CHUNK in (T, L) and n % min(n, HG_GROUP) == 0 and n % min(n, HG_OUT_GROUP) == 0
    dmat = _hgrn_decay_matrix()
    lat_spec = pl.BlockSpec((1, T, LANES), lambda b, h: (b, 0, h))
    ctx_spec = pl.BlockSpec((1, L, LANES), lambda b, h: (b, 0, h))
    return pl.pallas_call(
        functools.partial(_hgrn_scan_kernel, T=T, L=L),
        grid=(B, HG_HEADS),
        in_specs=[pl.BlockSpec((1, HG_DV), lambda b, h: (0, 0)), _resident(dmat.shape)]
        + [lat_spec] * 7 + [ctx_spec] * 7,
        out_specs=[lat_spec, ctx_spec],
        out_shape=[jax.ShapeDtypeStruct((B, T, HG_VW), BF16), jax.ShapeDtypeStruct((B, L, HG_VW), BF16)],
        scratch_shapes=[
            pltpu.VMEM((T, HG_DV), F32),
            pltpu.VMEM((T, 2 * HG_DK), BF16),
            pltpu.VMEM((T // HG_CHUNK, HG_DV, 2 * HG_DK), F32),
            pltpu.VMEM((T // HG_CHUNK, HG_DV, 2 * HG_DK), BF16),
            pltpu.VMEM((2, T // HG_CHUNK * 8, HG_DK), F32),
            pltpu.VMEM((2, HG_DV, HG_DK), F32),
        ],
        compiler_params=_params("parallel", "parallel"),
        name="hgrn_scan",
    )(g.reshape(1, HG_DV), dmat, *lat, *ctx)


def _merge_kernel(x_ref, h_ref, ya_ref, yg_ref, yh_ref, wgate_ref, wa_ref, wg_ref, wh_ref, wo_ref,
                  gpost_ref, gt_ref, gpre_ref, sh_ref, sc_ref, x1_ref, h2_ref):
    h = h_ref[0]
    D = D_MODEL
    y = jax.nn.sigmoid(_dot(h, wgate_ref[:, :D])) * _dot(ya_ref[0], wa_ref[...])
    y = y + jax.nn.sigmoid(_dot(h, wgate_ref[:, D:2 * D])) * _dot(yg_ref[0], wg_ref[...])
    y = y + jax.nn.sigmoid(_dot(h, wgate_ref[:, 2 * D:])) * _dot(yh_ref[0], wh_ref[...])
    z = _dot(y.astype(BF16), wo_ref[...])
    x1 = x_ref[0] + gt_ref[0] * _rms(z, gpost_ref[...])
    x1_ref[0] = x1
    h2_ref[0] = (_rms(x1, gpre_ref[...]) * (1.0 + sc_ref[0]) + sh_ref[0]).astype(BF16)


def _merge(x, h, ya, yg, yh, wgate, wa, wg, wh, wo, g_post, gt, g_pre, sh, sc, tm):
    B, T, D = x.shape
    row = pl.BlockSpec((1, tm, D), lambda b, i: (b, i, 0))
    vec = pl.BlockSpec((1, 1, D), lambda b, i: (b, 0, 0))
    par = pl.BlockSpec((1, D), lambda b, i: (0, 0))
    return pl.pallas_call(
        _merge_kernel,
        grid=(B, T // tm),
        in_specs=[row] * 5 + [_resident(wgate.shape)] + [_resident(wa.shape)] * 4 + [par, vec, par, vec, vec],
        out_specs=[row, row],
        out_shape=[jax.ShapeDtypeStruct((B, T, D), F32), jax.ShapeDtypeStruct((B, T, D), BF16)],
        compiler_params=_params("parallel", "parallel"),
        name="merge",
    )(x, h, ya, yg, yh, wgate, wa, wg, wh, wo, g_post.reshape(1, D), gt, g_pre.reshape(1, D), sh, sc)


def _ffn_kernel(x_ref, h_ref, hp_ref, hn_ref, wup_ref, cw_ref, cb_ref, wdn_ref, gpost_ref, gt_ref,
                o_ref, act_ref, *, tm):
    i = pl.program_id(1)
    last = pl.num_programs(1) - 1
    halo = BF16_ROWS
    hp = jnp.where(i > 0, hp_ref[0], jnp.zeros_like(hp_ref[0]))
    hn = jnp.where(i < last, hn_ref[0], jnp.zeros_like(hn_ref[0]))
    h_ext = jnp.concatenate([hp, h_ref[0], hn], axis=0)
    rows = tm + 2 * halo

    def conv(u, part, n):
        cw = cw_ref[part, n]
        u_prev = pltpu.roll(u, 1, 0)
        u_next = pltpu.roll(u, rows - 1, 0)
        return (u_prev * cw[0:1] + u * cw[1:2] + u_next * cw[2:3] + cb_ref[part, n])[halo:halo + tm]

    for n in range(D_FF // FF_CHUNK):
        a = conv(_dot(h_ext, wup_ref[0, n]), 0, n)
        b = conv(_dot(h_ext, wup_ref[1, n]), 1, n)
        act_ref[:, n * FF_CHUNK:(n + 1) * FF_CHUNK] = (_silu(a) * b).astype(BF16)
    y = _dot(act_ref[...], wdn_ref[...])
    o_ref[0] = x_ref[0] + gt_ref[0] * _rms(y, gpost_ref[...])


def _ffn(x, h, wup, cw, cb, wdn, g_post, gt, tm):
    B, T, D = x.shape
    nh = tm // BF16_ROWS
    row = pl.BlockSpec((1, tm, D), lambda b, i: (b, i, 0))
    prev = pl.BlockSpec((1, BF16_ROWS, D), lambda b, i: (b, jnp.maximum(i * nh - 1, 0), 0))
    nxt = pl.BlockSpec((1, BF16_ROWS, D), lambda b, i: (b, jnp.minimum((i + 1) * nh, T // BF16_ROWS - 1), 0))
    return pl.pallas_call(
        functools.partial(_ffn_kernel, tm=tm),
        grid=(B, T // tm),
        in_specs=[row, row, prev, nxt, _resident(wup.shape), _resident(cw.shape), _resident(cb.shape),
                  _resident(wdn.shape), pl.BlockSpec((1, D), lambda b, i: (0, 0)),
                  pl.BlockSpec((1, 1, D), lambda b, i: (b, 0, 0))],
        out_specs=row,
        out_shape=jax.ShapeDtypeStruct((B, T, D), F32),
        scratch_shapes=[pltpu.VMEM((tm, D_FF), BF16)],
        compiler_params=_params("parallel", "parallel"),
        name="ffn",
    )(x, h, h, h, wup, cw, cb, wdn, g_post.reshape(1, D), gt)


def _rope_tables(T):
    half = ATT_DH // 4
    t = jnp.arange(T, dtype=jnp.int32)
    rows = (t // GRID_W).astype(F32)
    cols = (t % GRID_W).astype(F32)
    inv = ROPE_BASE ** (-jnp.arange(half, dtype=F32) / half)
    ang_r = rows[:, None] * inv[None, :]
    ang_c = cols[:, None] * inv[None, :]
    z = jnp.zeros_like(ang_r)
    cos64 = jnp.concatenate([jnp.cos(ang_r)] * 2 + [jnp.cos(ang_c)] * 2, axis=1)
    sa64 = jnp.concatenate([-jnp.sin(ang_r), z, -jnp.sin(ang_c), z], axis=1)
    sb64 = jnp.concatenate([z, jnp.sin(ang_r), z, jnp.sin(ang_c)], axis=1)
    return tuple(jnp.concatenate([a, a], axis=1) for a in (cos64, sa64, sb64))


def _col_offsets():
    offs, start = [], 0
    for w in IN_SPLITS:
        offs.append(start)
        start += w
    return offs


def kernel(x, c, ctx, c_ctx, w_ada, b_ada, g_pre_mix, g_post_mix, g_pre_ffn, g_post_ffn, w_in, lam_q1, lam_k1, lam_q2, lam_k2, att_subln_g, gm_ln_g, gm_ln_b, gm_ws, gm_bs, hg_lb, hg_norm_g, w_br_att, w_br_gm, w_br_hg, w_out, w_up, conv_w, conv_b, w_down):
    B, T, D = x.shape
    L = ctx.shape[1]
    TM = 512
    TQ = 1024

    rows = 16
    c_all = jnp.concatenate([c, c_ctx[None, :], jnp.zeros((rows - B - 1, D), F32)], axis=0)
    mod = _ada(c_all, w_ada, b_ada)

    lb_all = jnp.cumsum(jax.nn.softmax(hg_lb.astype(F32), axis=0), axis=0)
    lb_all = lb_all - lb_all[0]
    tables = _rope_tables(T)
    offs = _col_offsets()
    nff = D_FF // FF_CHUNK

    xc = ctx
    for l in range(DEPTH):
        last = l == DEPTH - 1
        lam_init = 0.8 - 0.6 * math.exp(-0.3 * l)
        lam = (jnp.exp(jnp.sum(lam_q1[l] * lam_k1[l])) - jnp.exp(jnp.sum(lam_q2[l] * lam_k2[l])) + lam_init)

        def mods(k, ctx_row):
            m = mod[l, :, k * D:(k + 1) * D]
            if ctx_row:
                return jnp.broadcast_to(m[B:B + 1], (B, D)).reshape(B, 1, D)
            return m[:B].reshape(B, 1, D)

        w_l = w_in[l].astype(BF16)
        w_qkv = w_l[:, offs[0]:offs[3]]
        w_uv = w_l[:, offs[3]:offs[5]]
        w_hg = w_l[:, offs[5]:offs[10]]
        w_gate = w_l[:, offs[10]:]
        lb = lb_all[l]
        lb_tab = jnp.stack([jnp.log(lb[0]), jnp.log1p(-lb[0]), 1.0 - lb[0],
                            jnp.log(lb[1]), jnp.log1p(-lb[1]), 1.0 - lb[1],
                            jnp.zeros_like(lb[0]), jnp.zeros_like(lb[0])], axis=0)
        ws = gm_ws[l].astype(BF16)
        dg = GM_WIDTH // GM_GROUPS
        bias_full = jnp.broadcast_to(gm_bs[l].T[:, :, None], (GM_CHUNK, GM_GROUPS, dg)).reshape(GM_CHUNK, GM_WIDTH)
        wa, wg, wh, wo = (w[l].astype(BF16) for w in (w_br_att, w_br_gm, w_br_hg, w_out))
        wup = w_up[l].astype(BF16).reshape(D, 2, nff, FF_CHUNK).transpose(1, 2, 0, 3)
        cw = conv_w[l].reshape(CONV_W, 2, nff, FF_CHUNK).transpose(1, 2, 0, 3)
        cb = conv_b[l].reshape(2, nff, 1, FF_CHUNK)
        wdn = w_down[l].astype(BF16)

        hc = _prenorm(xc, g_pre_mix[l], mods(0, True), mods(1, True), L)
        h = _prenorm(x, g_pre_mix[l], mods(0, False), mods(1, False), TM)
        cq, ck, cv = _att_proj(hc, w_qkv, None, L)
        q, k, v = _att_proj(h, w_qkv, tables, TM)
        c_hg = _hgrn_proj(hc, w_hg, lb_tab, L)
        l_hg = _hgrn_proj(h, w_hg, lb_tab, TM)

        y_att = _attention(q, [(k, v), (ck, cv)], lam, att_subln_g[l], 1.0 - lam_init, TQ)
        y_gm = _gmlp(h, w_uv, gm_ln_g[l], gm_ln_b[l], ws, bias_full, TM)
        y_hg, yc_hg = _hgrn_scan(l_hg, c_hg, hg_norm_g[l])

        x, h2 = _merge(x, h, y_att, y_gm, y_hg, w_gate, wa, wg, wh, wo, g_post_mix[l], mods(2, False),
                       g_pre_ffn[l], mods(3, False), mods(4, False), TM)
        x = _ffn(x, h2, wup, cw, cb, wdn, g_post_ffn[l], mods(5, False), TM)

        if not last:
            yc_att = _attention(cq, [(ck, cv)], lam, att_subln_g[l], 1.0 - lam_init, L)
            yc_gm = _gmlp(hc, w_uv, gm_ln_g[l], gm_ln_b[l], ws, bias_full, L)
            xc, hc2 = _merge(xc, hc, yc_att, yc_gm, yc_hg, w_gate, wa, wg, wh, wo, g_post_mix[l],
                             mods(2, True), g_pre_ffn[l], mods(3, True), mods(4, True), L)
            xc = _ffn(xc, hc2, wup, cw, cb, wdn, g_post_ffn[l], mods(5, True), L)
    return x
```

```python
import functools
import math

import jax
import jax.numpy as jnp
import numpy as np
from jax import lax
from jax.experimental import pallas as pl
from jax.experimental.pallas import tpu as pltpu

F32 = jnp.float32
BF16 = jnp.bfloat16

D_MODEL = 1024
DEPTH = 2
GRID_W = 64
ATT_HEADS = 8
ATT_DH = 64
ATT_DV = 2 * ATT_DH
GM_WIDTH = 1024
GM_GROUPS = 8
GM_CHUNK = 128
HG_HEADS = 8
HG_DK = 128
HG_DV = 128
HG_CHUNK = 64
HG_LEVELS = 6
HG_GROUP = 8
HG_OUT_GROUP = 16
N_BRANCH = 3
D_FF = 2816
FF_CHUNK = 256
CONV_W = 3
ROPE_BASE = 10000.0
EPS = 1e-6

ATT_QW = ATT_HEADS * 2 * ATT_DH
ATT_VW = ATT_HEADS * ATT_DV
HG_KW = HG_HEADS * HG_DK
HG_VW = HG_HEADS * HG_DV
IN_SPLITS = (ATT_QW, ATT_QW, ATT_VW, GM_WIDTH, GM_WIDTH, HG_KW, HG_KW, HG_KW, HG_VW, HG_VW,
             N_BRANCH * D_MODEL)

LANES = 128
BF16_ROWS = 16
MXU_N = 256
VMEM_LIMIT = 56 * 1024 * 1024

NEG_INF = float("-inf")
ATT_Q_SCALE = ATT_DH ** -0.5 * math.log2(math.e)
ATT_KEY_CHUNK = 512


def _params(*sem):
    return pltpu.CompilerParams(dimension_semantics=sem, vmem_limit_bytes=VMEM_LIMIT)


def _resident(shape):
    nd = len(shape)
    return pl.BlockSpec(shape, lambda *_: (0,) * nd, pipeline_mode=pl.Buffered(1))


def _dot(a, b):
    return jnp.dot(a, b, preferred_element_type=F32)


def _dot_nt(a, b):
    return lax.dot_general(a, b, (((1,), (1,)), ((), ())), preferred_element_type=F32)


def _rms(x, g):
    return x * lax.rsqrt(jnp.mean(x * x, axis=-1, keepdims=True) + EPS) * g


def _sigmoid(x):
    return 0.5 * jnp.tanh(0.5 * x) + 0.5


def _silu(x):
    h = 0.5 * x
    return h + h * jnp.tanh(h)


def _gelu_tanh(x):
    return 0.5 * x * (1.0 + jnp.tanh(math.sqrt(2.0 / math.pi) * (x + 0.044715 * (x * x * x))))


def _ada_kernel(c_ref, w_ref, b_ref, o_ref):
    c = c_ref[...]
    o_ref[0] = _dot(_silu(c).astype(BF16), w_ref[0].astype(BF16)) + b_ref[0]


def _ada(c_all, w_ada, b_ada):
    rows = c_all.shape[0]
    n = w_ada.shape[-1]
    tn = 1536
    return pl.pallas_call(
        _ada_kernel,
        grid=(DEPTH, n // tn),
        in_specs=[
            pl.BlockSpec((rows, D_MODEL), lambda l, j: (0, 0)),
            pl.BlockSpec((1, D_MODEL, tn), lambda l, j: (l, 0, j)),
            pl.BlockSpec((1, 1, tn), lambda l, j: (l, 0, j)),
        ],
        out_specs=pl.BlockSpec((1, rows, tn), lambda l, j: (l, 0, j)),
        out_shape=jax.ShapeDtypeStruct((DEPTH, rows, n), F32),
        compiler_params=_params("parallel", "parallel"),
        name="ada",
    )(c_all, w_ada, b_ada.reshape(DEPTH, 1, n))


def _prenorm_kernel(x_ref, g_ref, sh_ref, sc_ref, h_ref):
    y = _rms(x_ref[0], g_ref[...])
    h_ref[0] = (y * (1.0 + sc_ref[0]) + sh_ref[0]).astype(BF16)


def _prenorm(x, g, sh, sc, tm):
    B, T, D = x.shape
    row = pl.BlockSpec((1, tm, D), lambda b, i: (b, i, 0))
    vec = pl.BlockSpec((1, 1, D), lambda b, i: (b, 0, 0))
    return pl.pallas_call(
        _prenorm_kernel,
        grid=(B, T // tm),
        in_specs=[row, pl.BlockSpec((1, D), lambda b, i: (0, 0)), vec, vec],
        out_specs=row,
        out_shape=jax.ShapeDtypeStruct((B, T, D), BF16),
        compiler_params=_params("parallel", "parallel"),
        name="prenorm",
    )(x, g.reshape(1, D), sh, sc)


def _att_proj_kernel(h_ref, w_ref, *rest, rope):
    if rope:
        cos_ref, sa_ref, sb_ref, q_ref, k_ref, v_ref = rest
        cos, sa, sb = cos_ref[...], sa_ref[...], sb_ref[...]
    else:
        q_ref, k_ref, v_ref = rest
    h = h_ref[0]

    def rot(a):
        half = ATT_DH // 4
        return a * cos + pltpu.roll(a, LANES - half, 1) * sa + pltpu.roll(a, half, 1) * sb

    for j in range(ATT_QW // MXU_N):
        cols = slice(j * MXU_N, (j + 1) * MXU_N)
        aq = _dot(h, w_ref[:, j * MXU_N:(j + 1) * MXU_N])
        ak = _dot(h, w_ref[:, ATT_QW + j * MXU_N:ATT_QW + (j + 1) * MXU_N])
        av = _dot(h, w_ref[:, 2 * ATT_QW + j * MXU_N:2 * ATT_QW + (j + 1) * MXU_N])
        if rope:
            aq = jnp.concatenate([rot(aq[:, :LANES]), rot(aq[:, LANES:])], axis=1)
            ak = jnp.concatenate([rot(ak[:, :LANES]), rot(ak[:, LANES:])], axis=1)
        q_ref[0, :, cols] = (aq * ATT_Q_SCALE).astype(BF16)
        k_ref[0, :, cols] = ak.astype(BF16)
        v_ref[0, :, cols] = av.astype(BF16)


def _att_proj(h, w_qkv, tables, tm):
    B, T, D = h.shape
    rope = tables is not None
    row = pl.BlockSpec((1, tm, D), lambda b, i: (b, i, 0))
    out = pl.BlockSpec((1, tm, ATT_QW), lambda b, i: (b, i, 0))
    in_specs = [row, _resident(w_qkv.shape)]
    args = [h, w_qkv]
    if rope:
        in_specs += [pl.BlockSpec((tm, LANES), lambda b, i: (i, 0))] * 3
        args += list(tables)
    return pl.pallas_call(
        functools.partial(_att_proj_kernel, rope=rope),
        grid=(B, T // tm),
        in_specs=in_specs,
        out_specs=[out, out, out],
        out_shape=[jax.ShapeDtypeStruct((B, T, ATT_QW), BF16)] * 3,
        compiler_params=_params("parallel", "parallel"),
        name="att_proj",
    )(*args)


def _attn_kernel(lam_ref, q_ref, g_ref, *rest, n_kv, post_scale):
    kv = rest[:2 * n_kv]
    o_ref = rest[2 * n_kv]
    vext = rest[2 * n_kv + 1:]

    @pl.when(pl.program_id(2) == 0)
    def _():
        for j in range(n_kv):
            v = kv[2 * j + 1][0]
            vext[j][:, :LANES] = v
            vext[j][:, LANES:] = jnp.ones_like(v)

    q = q_ref[0]
    lane = lax.broadcasted_iota(jnp.int32, q.shape, 1)
    zero = jnp.zeros_like(q)
    outs = []
    for i in range(2):
        qi = jnp.where(lane < ATT_DH, q, zero) if i == 0 else jnp.where(lane >= ATT_DH, q, zero)
        m = acc = None
        for j in range(n_kv):
            n_keys = kv[2 * j].shape[1]
            kc = min(n_keys, ATT_KEY_CHUNK)
            for c0 in range(0, n_keys, kc):
                s = _dot_nt(qi, kv[2 * j][0, c0:c0 + kc, :])
                mc = jnp.max(s, axis=-1, keepdims=True)
                m_new = mc if m is None else jnp.maximum(m, mc)
                pv = _dot(jnp.exp2(s - m_new).astype(BF16), vext[j][c0:c0 + kc, :])
                acc = pv if m is None else acc * jnp.exp2(m - m_new) + pv
                m = m_new
        outs.append(acc[:, :LANES] / acc[:, LANES:])
    o = outs[0] - lam_ref[0] * outs[1]
    o_ref[0] = (_rms(o, g_ref[...]) * post_scale).astype(BF16)


def _attention(q, kvs, lam, g, post_scale, tq):
    B, T, _ = q.shape
    in_specs = [
        pl.BlockSpec(memory_space=pltpu.SMEM),
        pl.BlockSpec((1, tq, LANES), lambda b, h, i: (b, i, h)),
        pl.BlockSpec((1, LANES), lambda b, h, i: (0, 0)),
    ]
    args = [lam.reshape(1), q, g.reshape(1, ATT_DV)]
    scratch = []
    for k, v in kvs:
        spec = pl.BlockSpec((1, k.shape[1], LANES), lambda b, h, i: (b, 0, h))
        in_specs += [spec, spec]
        args += [k, v]
        scratch.append(pltpu.VMEM((k.shape[1], 2 * LANES), BF16))
    return pl.pallas_call(
        functools.partial(_attn_kernel, n_kv=len(kvs), post_scale=post_scale),
        grid=(B, ATT_HEADS, T // tq),
        in_specs=in_specs,
        out_specs=pl.BlockSpec((1, tq, LANES), lambda b, h, i: (b, i, h)),
        out_shape=jax.ShapeDtypeStruct((B, T, ATT_VW), BF16),
        scratch_shapes=scratch,
        compiler_params=_params("parallel", "parallel", "arbitrary"),
        name="attention",
    )(*args)


def _gmlp_kernel(h_ref, w_ref, lng_ref, lnb_ref, ws_ref, bias_ref, o_ref, *, tm):
    h = h_ref[0]
    u = _gelu_tanh(_dot(h, w_ref[:, :GM_WIDTH]))
    v = _gelu_tanh(_dot(h, w_ref[:, GM_WIDTH:]))
    mu = jnp.mean(v, axis=-1, keepdims=True)
    vc = v - mu
    var = jnp.mean(vc * vc, axis=-1, keepdims=True)
    vn = (vc * lax.rsqrt(var + EPS) * lng_ref[...] + lnb_ref[...]).astype(BF16)
    dg = GM_WIDTH // GM_GROUPS
    for r in range(tm // GM_CHUNK):
        rows = slice(r * GM_CHUNK, (r + 1) * GM_CHUNK)
        for g in range(GM_GROUPS):
            cols = slice(g * dg, (g + 1) * dg)
            s = _dot(ws_ref[g], vn[rows, cols]) + bias_ref[:, cols]
            o_ref[0, rows, cols] = (u[rows, cols] * s).astype(BF16)


def _gmlp(h, w_uv, ln_g, ln_b, ws, bias_full, tm):
    B, T, D = h.shape
    return pl.pallas_call(
        functools.partial(_gmlp_kernel, tm=tm),
        grid=(B, T // tm),
        in_specs=[
            pl.BlockSpec((1, tm, D), lambda b, i: (b, i, 0)),
            _resident(w_uv.shape),
            _resident((1, GM_WIDTH)),
            _resident((1, GM_WIDTH)),
            _resident(ws.shape),
            _resident(bias_full.shape),
        ],
        out_specs=pl.BlockSpec((1, tm, GM_WIDTH), lambda b, i: (b, i, 0)),
        out_shape=jax.ShapeDtypeStruct((B, T, GM_WIDTH), BF16),
        compiler_params=_params("parallel", "parallel"),
        name="gmlp",
    )(h, w_uv, ln_g.reshape(1, GM_WIDTH), ln_b.reshape(1, GM_WIDTH), ws, bias_full)


def _hgrn_proj_kernel(h_ref, w_ref, lb_ref, q_ref, kf_ref, lff_ref, kb_ref, lfb_ref, v_ref, g_ref):
    h = h_ref[0]

    def gates(z, d, cols):
        lb = lb_ref[3 * d + 0:3 * d + 1, cols]
        log_1m = lb_ref[3 * d + 1:3 * d + 2, cols]
        one_m = lb_ref[3 * d + 2:3 * d + 3, cols]
        e = jnp.exp(-jnp.abs(z))
        one_p = 1.0 + e
        r = 1.0 / one_p
        er = e * r
        pos = z >= 0.0
        b = log_1m + (jnp.minimum(z, 0.0) - jnp.log(one_p))
        logf = jnp.maximum(jnp.log(lb + one_m * jnp.where(pos, r, er)), b)
        return logf, one_m * jnp.where(pos, er, r)

    def proj(j):
        return [_dot(h, w_ref[:, seg * HG_KW + j * MXU_N:seg * HG_KW + (j + 1) * MXU_N]) for seg in range(5)]

    n_chunks = HG_KW // MXU_N
    z = proj(0)
    for j in range(n_chunks):
        z_next = proj(j + 1) if j + 1 < n_chunks else None
        cols = slice(j * MXU_N, (j + 1) * MXU_N)
        q_ref[0, :, cols] = _silu(z[0]).astype(BF16)
        lf, k = gates(z[1], 0, cols)
        lff_ref[0, :, cols] = lf
        kf_ref[0, :, cols] = k.astype(BF16)
        lf, k = gates(z[2], 1, cols)
        lfb_ref[0, :, cols] = lf
        kb_ref[0, :, cols] = k.astype(BF16)
        v_ref[0, :, cols] = z[3].astype(BF16)
        g_ref[0, :, cols] = _silu(z[4]).astype(BF16)
        z = z_next


def _hgrn_proj(h, w_hg, lb_tab, tm):
    B, T, D = h.shape
    out = pl.BlockSpec((1, tm, HG_KW), lambda b, i: (b, i, 0))
    bf = jax.ShapeDtypeStruct((B, T, HG_KW), BF16)
    f32 = jax.ShapeDtypeStruct((B, T, HG_KW), F32)
    return pl.pallas_call(
        _hgrn_proj_kernel,
        grid=(B, T // tm),
        in_specs=[pl.BlockSpec((1, tm, D), lambda b, i: (b, i, 0)), _resident(w_hg.shape),
                  _resident(lb_tab.shape)],
        out_specs=[out] * 7,
        out_shape=[bf, bf, f32, bf, f32, bf, bf],
        compiler_params=_params("parallel", "parallel"),
        name="hgrn_proj",
    )(h, w_hg, lb_tab)


def _split2(x):
    a = x.astype(BF16)
    return a, (x - a.astype(F32)).astype(BF16)


def _hgrn_decay_matrix():
    C = HG_CHUNK
    t = np.arange(C)[:, None]
    u = np.arange(C)[None, :]
    out = []
    for rev in (False, True):
        blocks = [u >= t] if rev else [u <= t]
        for l in range(1, HG_LEVELS):
            half = 1 << l
            mid = (t & -(2 * half)) + half
            upper = (t & half) != 0
            if rev:
                blocks.append(np.where(upper, (u >= mid) & (u < t), (u >= t) & (u < mid)))
            else:
                blocks.append(np.where(upper, (u >= mid) & (u <= t), (u > t) & (u < mid)))
        m = np.concatenate(blocks, axis=0).astype(np.float32)
        out += [m, m]
    return jnp.asarray(np.concatenate(out, axis=1), dtype=BF16)


def _hgrn_local(blocks, dmat, consts):
    C = HG_CHUNK
    G = range(len(blocks))
    level_of_pair, eye, r_in = consts
    qf = [blocks[g][0].astype(F32) for g in G]
    kf = [(blocks[g][1].astype(F32), blocks[g][2].astype(F32)) for g in G]
    z = jnp.zeros((C, HG_DK), BF16)

    def block_diag(lf_f, lf_b):
        f1, f2 = _split2(lf_f)
        b1, b2 = _split2(lf_b)
        return jnp.concatenate([jnp.concatenate([f1, z], axis=1), jnp.concatenate([f2, z], axis=1),
                                jnp.concatenate([z, b1], axis=1), jnp.concatenate([z, b2], axis=1)], axis=0)

    logs = [_dot(dmat, block_diag(blocks[g][3], blocks[g][4])) for g in G]
    logs = [(logs[g][:, :HG_DK], logs[g][:, HG_DK:]) for g in G]
    ex = [[jnp.exp(logs[g][d]) for d in range(2)] for g in G]
    q_in = [[(qf[g] * ex[g][d][0:C]).astype(BF16) for d in range(2)] for g in G]
    tot = [[logs[g][0][C - 1:C], logs[g][1][0:1]] for g in G]
    k_out = [[(kf[g][d] * jnp.exp(tot[g][d] - logs[g][d][0:C])).astype(BF16) for d in range(2)] for g in G]
    g_tot = [[ex[g][0][C - 1:C], ex[g][1][0:1]] for g in G]
    scores = [jnp.where(eye, jnp.sum(qf[g] * (kf[g][0] + kf[g][1]), axis=-1, keepdims=True), 0.0) for g in G]
    for l in range(HG_LEVELS):
        upper = (r_in & (1 << l)) != 0
        rows = slice(l * C, (l + 1) * C)
        for g in G:
            if l == 0:
                eq = [jnp.exp(blocks[g][3 + d]) for d in range(2)]
                kq = kf[g]
            else:
                eq = [ex[g][d][rows] for d in range(2)]
                kq = [kf[g][d] * eq[d] for d in range(2)]
            qs = jnp.concatenate([jnp.where(upper, qf[g] * eq[0], 0.0), jnp.where(upper, 0.0, qf[g] * eq[1])], axis=1)
            ks = jnp.concatenate([jnp.where(upper, 0.0, kq[0]), jnp.where(upper, kq[1], 0.0)], axis=1)
            scores[g] = jnp.where(level_of_pair == l + 1, _dot_nt(qs.astype(BF16), ks.astype(BF16)), scores[g])
    o = [_dot(scores[g].astype(BF16), blocks[g][5]) for g in G]
    upd = [_dot(blocks[g][5].astype(F32).T.astype(BF16), jnp.concatenate(k_out[g], axis=1)) for g in G]
    return [jnp.concatenate(q_in[g], axis=1) for g in G], upd, g_tot, o


def _hgrn_scan_kernel(g_ref, d_ref, q_ref, kf_ref, lff_ref, kb_ref, lfb_ref, v_ref, og_ref,
                      cq_ref, ckf_ref, clff_ref, ckb_ref, clfb_ref, cv_ref, cog_ref,
                      y_ref, yc_ref, o_ref, qin_ref, upd_ref, hist_ref, gtot_ref, st_ref, *, T, L):
    C = HG_CHUNK
    row_id = lax.broadcasted_iota(jnp.int32, (C, C), 0)
    col_id = lax.broadcasted_iota(jnp.int32, (C, C), 1)
    r_in = lax.broadcasted_iota(jnp.int32, (C, HG_DK), 0)
    diff = row_id ^ col_id
    level_of_pair = functools.reduce(jnp.add, [(diff >= (1 << j)).astype(jnp.int32) for j in range(HG_LEVELS)])
    consts = (level_of_pair, row_id == col_id, r_in)

    st_ref[...] = jnp.zeros_like(st_ref)

    def block_rows(c):
        return pl.ds(pl.multiple_of(c * C, C), C)

    def scan(q_r, k_rs, lf_rs, v_r, og_r, y_r, n):
        def local(i, carry):
            cs = [i * group + g for g in range(group)]
            blocks = [(q_r[0, block_rows(c), :], k_rs[0][0, block_rows(c), :], k_rs[1][0, block_rows(c), :],
                       lf_rs[0][0, block_rows(c), :], lf_rs[1][0, block_rows(c), :], v_r[0, block_rows(c), :])
                      for c in cs]
            q_in, upd, g_tot, o = _hgrn_local(blocks, d_ref[...], consts)
            for g, c in enumerate(cs):
                qin_ref[block_rows(c), :] = q_in[g]
                upd_ref[c] = upd[g]
                for d in range(2):
                    gtot_ref[d, pl.ds(pl.multiple_of(c * 8, 8), 8), :] = jnp.broadcast_to(g_tot[g][d], (8, HG_DK))
                o_ref[block_rows(c), :] = o[g]
            return carry

        def carry_states(i, carry):
            for d in range(2):
                c = i if d == 0 else n - 1 - i
                lanes = slice(d * HG_DK, (d + 1) * HG_DK)
                st = st_ref[d]
                hist_ref[c, :, lanes] = st.astype(BF16)
                g_tot = gtot_ref[d, pl.ds(pl.multiple_of(c * 8, 8), 8), :][0:1]
                st_ref[d] = st * g_tot + upd_ref[c, :, lanes]
            return carry

        group = min(n, HG_GROUP)
        out_group = min(n, HG_OUT_GROUP)

        def outputs(i, carry):
            cs = [i * out_group + g for g in range(out_group)]
            o = [o_ref[block_rows(c), :] + _dot_nt(qin_ref[block_rows(c), :], hist_ref[c]) for c in cs]
            for c, o_c in zip(cs, o):
                y_r[0, block_rows(c), :] = (_rms(o_c, g_ref[...]) * og_r[0, block_rows(c), :].astype(F32)
                                            ).astype(BF16)
            return carry

        lax.fori_loop(0, n // group, local, 0)
        lax.fori_loop(0, n, carry_states, 0, unroll=4)
        lax.fori_loop(0, n // out_group, outputs, 0)

    scan(cq_ref, (ckf_ref, ckb_ref), (clff_ref, clfb_ref), cv_ref, cog_ref, yc_ref, L // C)
    scan(q_ref, (kf_ref, kb_ref), (lff_ref, lfb_ref), v_ref, og_ref, y_ref, T // C)


def _hgrn_scan(lat, ctx, g):
    B, T, _ = lat[0].shape
    L = ctx[0].shape[1]
    for n in (T // HG_CHUNK, L // HG_CHUNK):
        assert n * HG_CHUNK in (T, L) and n % min(n, HG_GROUP) == 0 and n % min(n, HG_OUT_GROUP) == 0
    dmat = _hgrn_decay_matrix()
    lat_spec = pl.BlockSpec((1, T, LANES), lambda b, h: (b, 0, h))
    ctx_spec = pl.BlockSpec((1, L, LANES), lambda b, h: (b, 0, h))
    return pl.pallas_call(
        functools.partial(_hgrn_scan_kernel, T=T, L=L),
        grid=(B, HG_HEADS),
        in_specs=[pl.BlockSpec((1, HG_DV), lambda b, h: (0, 0)), _resident(dmat.shape)]
        + [lat_spec] * 7 + [ctx_spec] * 7,
        out_specs=[lat_spec, ctx_spec],
        out_shape=[jax.ShapeDtypeStruct((B, T, HG_VW), BF16), jax.ShapeDtypeStruct((B, L, HG_VW), BF16)],
        scratch_shapes=[
            pltpu.VMEM((T, HG_DV), F32),
            pltpu.VMEM((T, 2 * HG_DK), BF16),
            pltpu.VMEM((T // HG_CHUNK, HG_DV, 2 * HG_DK), F32),
            pltpu.VMEM((T // HG_CHUNK, HG_DV, 2 * HG_DK), BF16),
            pltpu.VMEM((2, T // HG_CHUNK * 8, HG_DK), F32),
            pltpu.VMEM((2, HG_DV, HG_DK), F32),
        ],
        compiler_params=_params("parallel", "parallel"),
        name="hgrn_scan",
    )(g.reshape(1, HG_DV), dmat, *lat, *ctx)


def _merge_kernel(x_ref, h_ref, ya_ref, yg_ref, yh_ref, wgate_ref, wa_ref, wg_ref, wh_ref, wo_ref,
                  gpost_ref, gt_ref, gpre_ref, sh_ref, sc_ref, x1_ref, h2_ref):
    h = h_ref[0]
    D = D_MODEL
    y = _sigmoid(_dot(h, wgate_ref[:, :D])) * _dot(ya_ref[0], wa_ref[...])
    y = y + _sigmoid(_dot(h, wgate_ref[:, D:2 * D])) * _dot(yg_ref[0], wg_ref[...])
    y = y + _sigmoid(_dot(h, wgate_ref[:, 2 * D:])) * _dot(yh_ref[0], wh_ref[...])
    z = _dot(y.astype(BF16), wo_ref[...])
    x1 = x_ref[0] + gt_ref[0] * _rms(z, gpost_ref[...])
    x1_ref[0] = x1
    h2_ref[0] = (_rms(x1, gpre_ref[...]) * (1.0 + sc_ref[0]) + sh_ref[0]).astype(BF16)


def _merge(x, h, ya, yg, yh, wgate, wa, wg, wh, wo, g_post, gt, g_pre, sh, sc, tm):
    B, T, D = x.shape
    row = pl.BlockSpec((1, tm, D), lambda b, i: (b, i, 0))
    vec = pl.BlockSpec((1, 1, D), lambda b, i: (b, 0, 0))
    par = pl.BlockSpec((1, D), lambda b, i: (0, 0))
    return pl.pallas_call(
        _merge_kernel,
        grid=(B, T // tm),
        in_specs=[row] * 5 + [_resident(wgate.shape)] + [_resident(wa.shape)] * 4 + [par, vec, par, vec, vec],
        out_specs=[row, row],
        out_shape=[jax.ShapeDtypeStruct((B, T, D), F32), jax.ShapeDtypeStruct((B, T, D), BF16)],
        compiler_params=_params("parallel", "parallel"),
        name="merge",
    )(x, h, ya, yg, yh, wgate, wa, wg, wh, wo, g_post.reshape(1, D), gt, g_pre.reshape(1, D), sh, sc)


def _ffn_kernel(x_ref, h_ref, hp_ref, hn_ref, wup_ref, cw_ref, cb_ref, wdn_ref, gpost_ref, gt_ref,
                o_ref, act_ref, *, tm):
    i = pl.program_id(1)
    last = pl.num_programs(1) - 1
    halo = BF16_ROWS
    hp = jnp.where(i > 0, hp_ref[0], jnp.zeros_like(hp_ref[0]))
    hn = jnp.where(i < last, hn_ref[0], jnp.zeros_like(hn_ref[0]))
    h_ext = jnp.concatenate([hp, h_ref[0], hn], axis=0)
    rows = tm + 2 * halo

    def conv(u, part, n):
        cw = cw_ref[part, n]
        u_prev = pltpu.roll(u, 1, 0)
        u_next = pltpu.roll(u, rows - 1, 0)
        return (u_prev * cw[0:1] + u * cw[1:2] + u_next * cw[2:3] + cb_ref[part, n])[halo:halo + tm]

    for n in range(D_FF // FF_CHUNK):
        a = conv(_dot(h_ext, wup_ref[0, n]), 0, n)
        b = conv(_dot(h_ext, wup_ref[1, n]), 1, n)
        act_ref[:, n * FF_CHUNK:(n + 1) * FF_CHUNK] = (_silu(a) * b).astype(BF16)
    y = _dot(act_ref[...], wdn_ref[...])
    o_ref[0] = x_ref[0] + gt_ref[0] * _rms(y, gpost_ref[...])


def _ffn(x, h, wup, cw, cb, wdn, g_post, gt, tm):
    B, T, D = x.shape
    nh = tm // BF16_ROWS
    row = pl.BlockSpec((1, tm, D), lambda b, i: (b, i, 0))
    prev = pl.BlockSpec((1, BF16_ROWS, D), lambda b, i: (b, jnp.maximum(i * nh - 1, 0), 0))
    nxt = pl.BlockSpec((1, BF16_ROWS, D), lambda b, i: (b, jnp.minimum((i + 1) * nh, T // BF16_ROWS - 1), 0))
    return pl.pallas_call(
        functools.partial(_ffn_kernel, tm=tm),
        grid=(B, T // tm),
        in_specs=[row, row, prev, nxt, _resident(wup.shape), _resident(cw.shape), _resident(cb.shape),
                  _resident(wdn.shape), pl.BlockSpec((1, D), lambda b, i: (0, 0)),
                  pl.BlockSpec((1, 1, D), lambda b, i: (b, 0, 0))],
        out_specs=row,
        out_shape=jax.ShapeDtypeStruct((B, T, D), F32),
        scratch_shapes=[pltpu.VMEM((tm, D_FF), BF16)],
        compiler_params=_params("parallel", "parallel"),
        name="ffn",
    )(x, h, h, h, wup, cw, cb, wdn, g_post.reshape(1, D), gt)


def _rope_tables(T):
    half = ATT_DH // 4
    t = jnp.arange(T, dtype=jnp.int32)
    rows = (t // GRID_W).astype(F32)
    cols = (t % GRID_W).astype(F32)
    inv = ROPE_BASE ** (-jnp.arange(half, dtype=F32) / half)
    ang_r = rows[:, None] * inv[None, :]
    ang_c = cols[:, None] * inv[None, :]
    z = jnp.zeros_like(ang_r)
    cos64 = jnp.concatenate([jnp.cos(ang_r)] * 2 + [jnp.cos(ang_c)] * 2, axis=1)
    sa64 = jnp.concatenate([-jnp.sin(ang_r), z, -jnp.sin(ang_c), z], axis=1)
    sb64 = jnp.concatenate([z, jnp.sin(ang_r), z, jnp.sin(ang_c)], axis=1)
    return tuple(jnp.concatenate([a, a], axis=1) for a in (cos64, sa64, sb64))


def _col_offsets():
    offs, start = [], 0
    for w in IN_SPLITS:
        offs.append(start)
        start += w
    return offs


def kernel(x, c, ctx, c_ctx, w_ada, b_ada, g_pre_mix, g_post_mix, g_pre_ffn, g_post_ffn, w_in, lam_q1, lam_k1, lam_q2, lam_k2, att_subln_g, gm_ln_g, gm_ln_b, gm_ws, gm_bs, hg_lb, hg_norm_g, w_br_att, w_br_gm, w_br_hg, w_out, w_up, conv_w, conv_b, w_down):
    B, T, D = x.shape
    L = ctx.shape[1]
    TM = 512
    TQ = 1024

    rows = 16
    c_all = jnp.concatenate([c, c_ctx[None, :], jnp.zeros((rows - B - 1, D), F32)], axis=0)
    mod = _ada(c_all, w_ada, b_ada)

    lb_all = jnp.cumsum(jax.nn.softmax(hg_lb.astype(F32), axis=0), axis=0)
    lb_all = lb_all - lb_all[0]
    tables = _rope_tables(T)
    offs = _col_offsets()
    nff = D_FF // FF_CHUNK

    xc = ctx
    for l in range(DEPTH):
        last = l == DEPTH - 1
        lam_init = 0.8 - 0.6 * math.exp(-0.3 * l)
        lam = (jnp.exp(jnp.sum(lam_q1[l] * lam_k1[l])) - jnp.exp(jnp.sum(lam_q2[l] * lam_k2[l])) + lam_init)

        def mods(k, ctx_row):
            m = mod[l, :, k * D:(k + 1) * D]
            if ctx_row:
                return jnp.broadcast_to(m[B:B + 1], (B, D)).reshape(B, 1, D)
            return m[:B].reshape(B, 1, D)

        w_l = w_in[l].astype(BF16)
        w_qkv = w_l[:, offs[0]:offs[3]]
        w_uv = w_l[:, offs[3]:offs[5]]
        w_hg = w_l[:, offs[5]:offs[10]]
        w_gate = w_l[:, offs[10]:]
        lb = lb_all[l]
        lb_tab = jnp.stack([lb[0], jnp.log1p(-lb[0]), 1.0 - lb[0],
                            lb[1], jnp.log1p(-lb[1]), 1.0 - lb[1],
                            jnp.zeros_like(lb[0]), jnp.zeros_like(lb[0])], axis=0)
        ws = gm_ws[l].astype(BF16)
        dg = GM_WIDTH // GM_GROUPS
        bias_full = jnp.broadcast_to(gm_bs[l].T[:, :, None], (GM_CHUNK, GM_GROUPS, dg)).reshape(GM_CHUNK, GM_WIDTH)
        wa, wg, wh, wo = (w[l].astype(BF16) for w in (w_br_att, w_br_gm, w_br_hg, w_out))
        wup = w_up[l].astype(BF16).reshape(D, 2, nff, FF_CHUNK).transpose(1, 2, 0, 3)
        cw = conv_w[l].reshape(CONV_W, 2, nff, FF_CHUNK).transpose(1, 2, 0, 3)
        cb = conv_b[l].reshape(2, nff, 1, FF_CHUNK)
        wdn = w_down[l].astype(BF16)

        def flat(a):
            return a.reshape(1, B * L, a.shape[-1])

        def unflat(a):
            return a.reshape(B, L, a.shape[-1])

        def cmod(k):
            return mod[l, B:B + 1, k * D:(k + 1) * D].reshape(1, 1, D)

        hc = _prenorm(flat(xc), g_pre_mix[l], cmod(0), cmod(1), TM)
        h = _prenorm(x, g_pre_mix[l], mods(0, False), mods(1, False), TM)
        cq, ck, cv = (unflat(a) for a in _att_proj(hc, w_qkv, None, TM))
        q, k, v = _att_proj(h, w_qkv, tables, 2 * TM)
        c_hg = [unflat(a) for a in _hgrn_proj(hc, w_hg, lb_tab, TM)]
        l_hg = _hgrn_proj(h, w_hg, lb_tab, TM)

        y_att = _attention(q, [(k, v), (ck, cv)], lam, att_subln_g[l], 1.0 - lam_init, TQ)
        y_gm = _gmlp(h, w_uv, gm_ln_g[l], gm_ln_b[l], ws, bias_full, 2 * TM)
        y_hg, yc_hg = _hgrn_scan(l_hg, c_hg, hg_norm_g[l])

        x, h2 = _merge(x, h, y_att, y_gm, y_hg, w_gate, wa, wg, wh, wo, g_post_mix[l], mods(2, False),
                       g_pre_ffn[l], mods(3, False), mods(4, False), TM)
        x = _ffn(x, h2, wup, cw, cb, wdn, g_post_ffn[l], mods(5, False), TM)

        if not last:
            yc_att = _attention(cq, [(ck, cv)], lam, att_subln_g[l], 1.0 - lam_init, L)
            yc_gm = _gmlp(hc, w_uv, gm_ln_g[l], gm_ln_b[l], ws, bias_full, TM)
            xc, hc2 = _merge(flat(xc), hc, flat(yc_att), yc_gm, flat(yc_hg), w_gate, wa, wg, wh, wo, g_post_mix[l],
                             cmod(2), g_pre_ffn[l], cmod(3), cmod(4), TM)
            xc = _ffn(unflat(xc), unflat(hc2), wup, cw, cb, wdn, g_post_ffn[l], mods(5, True), L)
    return x
```

```python
import functools
import math

import jax
import jax.numpy as jnp
import numpy as np
from jax import lax
from jax.experimental import pallas as pl
from jax.experimental.pallas import tpu as pltpu

F32 = jnp.float32
BF16 = jnp.bfloat16

D_MODEL = 1024
DEPTH = 2
GRID_W = 64
ATT_HEADS = 8
ATT_DH = 64
ATT_DV = 2 * ATT_DH
GM_WIDTH = 1024
GM_GROUPS = 8
GM_CHUNK = 128
HG_HEADS = 8
HG_DK = 128
HG_DV = 128
HG_CHUNK = 64
HG_LEVELS = 6
HG_GROUP = 8
HG_OUT_GROUP = 16
N_BRANCH = 3
D_FF = 2816
FF_CHUNK = 256
CONV_W = 3
ROPE_BASE = 10000.0
EPS = 1e-6

ATT_QW = ATT_HEADS * 2 * ATT_DH
ATT_VW = ATT_HEADS * ATT_DV
HG_KW = HG_HEADS * HG_DK
HG_VW = HG_HEADS * HG_DV
IN_SPLITS = (ATT_QW, ATT_QW, ATT_VW, GM_WIDTH, GM_WIDTH, HG_KW, HG_KW, HG_KW, HG_VW, HG_VW,
             N_BRANCH * D_MODEL)

ROW_TILE = 512
WIDE_ROW_TILE = 1024
ADA_ROWS = 16
ADA_COL_TILE = 1536

LANES = 128
BF16_ROWS = 16
MXU_N = 256
VMEM_LIMIT = 56 * 1024 * 1024

NEG_INF = float("-inf")
ATT_Q_SCALE = ATT_DH ** -0.5 * math.log2(math.e)
ATT_KEY_CHUNK = 512


def _params(*sem):
    return pltpu.CompilerParams(dimension_semantics=sem, vmem_limit_bytes=VMEM_LIMIT)


def _resident(shape):
    nd = len(shape)
    return pl.BlockSpec(shape, lambda *_: (0,) * nd, pipeline_mode=pl.Buffered(1))


def _dot(a, b):
    return jnp.dot(a, b, preferred_element_type=F32)


def _dot_nt(a, b):
    return lax.dot_general(a, b, (((1,), (1,)), ((), ())), preferred_element_type=F32)


def _rms(x, g):
    return x * lax.rsqrt(jnp.mean(x * x, axis=-1, keepdims=True) + EPS) * g


def _sigmoid(x):
    return 0.5 * jnp.tanh(0.5 * x) + 0.5


def _silu(x):
    h = 0.5 * x
    return h + h * jnp.tanh(h)


def _gelu_tanh(x):
    return 0.5 * x * (1.0 + jnp.tanh(math.sqrt(2.0 / math.pi) * (x + 0.044715 * (x * x * x))))


def _ada_kernel(c_ref, w_ref, b_ref, o_ref):
    c = c_ref[...]
    o_ref[0] = _dot(_silu(c).astype(BF16), w_ref[0].astype(BF16)) + b_ref[0]


def _ada(c_all, w_ada, b_ada):
    rows = c_all.shape[0]
    n = w_ada.shape[-1]
    tn = ADA_COL_TILE
    return pl.pallas_call(
        _ada_kernel,
        grid=(DEPTH, n // tn),
        in_specs=[
            pl.BlockSpec((rows, D_MODEL), lambda l, j: (0, 0)),
            pl.BlockSpec((1, D_MODEL, tn), lambda l, j: (l, 0, j)),
            pl.BlockSpec((1, 1, tn), lambda l, j: (l, 0, j)),
        ],
        out_specs=pl.BlockSpec((1, rows, tn), lambda l, j: (l, 0, j)),
        out_shape=jax.ShapeDtypeStruct((DEPTH, rows, n), F32),
        compiler_params=_params("parallel", "parallel"),
        name="ada",
    )(c_all, w_ada, b_ada.reshape(DEPTH, 1, n))


def _prenorm_kernel(x_ref, g_ref, sh_ref, sc_ref, h_ref):
    y = _rms(x_ref[0], g_ref[...])
    h_ref[0] = (y * (1.0 + sc_ref[0]) + sh_ref[0]).astype(BF16)


def _prenorm(x, g, sh, sc, tm):
    B, T, D = x.shape
    row = pl.BlockSpec((1, tm, D), lambda b, i: (b, i, 0))
    vec = pl.BlockSpec((1, 1, D), lambda b, i: (b, 0, 0))
    return pl.pallas_call(
        _prenorm_kernel,
        grid=(B, T // tm),
        in_specs=[row, pl.BlockSpec((1, D), lambda b, i: (0, 0)), vec, vec],
        out_specs=row,
        out_shape=jax.ShapeDtypeStruct((B, T, D), BF16),
        compiler_params=_params("parallel", "parallel"),
        name="prenorm",
    )(x, g.reshape(1, D), sh, sc)


def _att_proj_kernel(h_ref, w_ref, *rest, rope):
    if rope:
        cos_ref, sa_ref, sb_ref, q_ref, k_ref, v_ref = rest
        cos, sa, sb = cos_ref[...], sa_ref[...], sb_ref[...]
    else:
        q_ref, k_ref, v_ref = rest
    h = h_ref[0]

    def rot(a):
        half = ATT_DH // 4
        return a * cos + pltpu.roll(a, LANES - half, 1) * sa + pltpu.roll(a, half, 1) * sb

    for j in range(ATT_QW // MXU_N):
        cols = slice(j * MXU_N, (j + 1) * MXU_N)
        aq = _dot(h, w_ref[:, j * MXU_N:(j + 1) * MXU_N])
        ak = _dot(h, w_ref[:, ATT_QW + j * MXU_N:ATT_QW + (j + 1) * MXU_N])
        av = _dot(h, w_ref[:, 2 * ATT_QW + j * MXU_N:2 * ATT_QW + (j + 1) * MXU_N])
        if rope:
            aq = jnp.concatenate([rot(aq[:, :LANES]), rot(aq[:, LANES:])], axis=1)
            ak = jnp.concatenate([rot(ak[:, :LANES]), rot(ak[:, LANES:])], axis=1)
        q_ref[0, :, cols] = (aq * ATT_Q_SCALE).astype(BF16)
        k_ref[0, :, cols] = ak.astype(BF16)
        v_ref[0, :, cols] = av.astype(BF16)


def _att_proj(h, w_qkv, tables, tm):
    B, T, D = h.shape
    rope = tables is not None
    row = pl.BlockSpec((1, tm, D), lambda b, i: (b, i, 0))
    out = pl.BlockSpec((1, tm, ATT_QW), lambda b, i: (b, i, 0))
    in_specs = [row, _resident(w_qkv.shape)]
    args = [h, w_qkv]
    if rope:
        in_specs += [pl.BlockSpec((tm, LANES), lambda b, i: (i, 0))] * 3
        args += list(tables)
    return pl.pallas_call(
        functools.partial(_att_proj_kernel, rope=rope),
        grid=(B, T // tm),
        in_specs=in_specs,
        out_specs=[out, out, out],
        out_shape=[jax.ShapeDtypeStruct((B, T, ATT_QW), BF16)] * 3,
        compiler_params=_params("parallel", "parallel"),
        name="att_proj",
    )(*args)


def _attn_kernel(lam_ref, q_ref, g_ref, *rest, n_kv, post_scale):
    kv = rest[:2 * n_kv]
    o_ref = rest[2 * n_kv]
    vext = rest[2 * n_kv + 1:]

    @pl.when(pl.program_id(2) == 0)
    def _():
        for j in range(n_kv):
            v = kv[2 * j + 1][0]
            vext[j][:, :LANES] = v
            vext[j][:, LANES:] = jnp.ones_like(v)

    q = q_ref[0]
    lane = lax.broadcasted_iota(jnp.int32, q.shape, 1)
    zero = jnp.zeros_like(q)
    outs = []
    for i in range(2):
        qi = jnp.where(lane < ATT_DH, q, zero) if i == 0 else jnp.where(lane >= ATT_DH, q, zero)
        m = acc = None
        for j in range(n_kv):
            n_keys = kv[2 * j].shape[1]
            kc = min(n_keys, ATT_KEY_CHUNK)
            for c0 in range(0, n_keys, kc):
                s = _dot_nt(qi, kv[2 * j][0, c0:c0 + kc, :])
                mc = jnp.max(s, axis=-1, keepdims=True)
                m_new = mc if m is None else jnp.maximum(m, mc)
                pv = _dot(jnp.exp2(s - m_new).astype(BF16), vext[j][c0:c0 + kc, :])
                acc = pv if m is None else acc * jnp.exp2(m - m_new) + pv
                m = m_new
        outs.append(acc[:, :LANES] / acc[:, LANES:])
    o = outs[0] - lam_ref[0] * outs[1]
    o_ref[0] = (_rms(o, g_ref[...]) * post_scale).astype(BF16)


def _attention(q, kvs, lam, g, post_scale, tq):
    B, T, _ = q.shape
    in_specs = [
        pl.BlockSpec(memory_space=pltpu.SMEM),
        pl.BlockSpec((1, tq, LANES), lambda b, h, i: (b, i, h)),
        pl.BlockSpec((1, LANES), lambda b, h, i: (0, 0)),
    ]
    args = [lam.reshape(1), q, g.reshape(1, ATT_DV)]
    scratch = []
    for k, v in kvs:
        spec = pl.BlockSpec((1, k.shape[1], LANES), lambda b, h, i: (b, 0, h))
        in_specs += [spec, spec]
        args += [k, v]
        scratch.append(pltpu.VMEM((k.shape[1], 2 * LANES), BF16))
    return pl.pallas_call(
        functools.partial(_attn_kernel, n_kv=len(kvs), post_scale=post_scale),
        grid=(B, ATT_HEADS, T // tq),
        in_specs=in_specs,
        out_specs=pl.BlockSpec((1, tq, LANES), lambda b, h, i: (b, i, h)),
        out_shape=jax.ShapeDtypeStruct((B, T, ATT_VW), BF16),
        scratch_shapes=scratch,
        compiler_params=_params("parallel", "parallel", "arbitrary"),
        name="attention",
    )(*args)


def _gmlp_kernel(h_ref, w_ref, lng_ref, lnb_ref, ws_ref, bias_ref, o_ref, *, tm):
    h = h_ref[0]
    u = _gelu_tanh(_dot(h, w_ref[:, :GM_WIDTH]))
    v = _gelu_tanh(_dot(h, w_ref[:, GM_WIDTH:]))
    mu = jnp.mean(v, axis=-1, keepdims=True)
    vc = v - mu
    var = jnp.mean(vc * vc, axis=-1, keepdims=True)
    vn = (vc * lax.rsqrt(var + EPS) * lng_ref[...] + lnb_ref[...]).astype(BF16)
    dg = GM_WIDTH // GM_GROUPS
    for r in range(tm // GM_CHUNK):
        rows = slice(r * GM_CHUNK, (r + 1) * GM_CHUNK)
        for g in range(GM_GROUPS):
            cols = slice(g * dg, (g + 1) * dg)
            s = _dot(ws_ref[g], vn[rows, cols]) + bias_ref[:, cols]
            o_ref[0, rows, cols] = (u[rows, cols] * s).astype(BF16)


def _gmlp(h, w_uv, ln_g, ln_b, ws, bias_full, tm):
    B, T, D = h.shape
    return pl.pallas_call(
        functools.partial(_gmlp_kernel, tm=tm),
        grid=(B, T // tm),
        in_specs=[
            pl.BlockSpec((1, tm, D), lambda b, i: (b, i, 0)),
            _resident(w_uv.shape),
            _resident((1, GM_WIDTH)),
            _resident((1, GM_WIDTH)),
            _resident(ws.shape),
            _resident(bias_full.shape),
        ],
        out_specs=pl.BlockSpec((1, tm, GM_WIDTH), lambda b, i: (b, i, 0)),
        out_shape=jax.ShapeDtypeStruct((B, T, GM_WIDTH), BF16),
        compiler_params=_params("parallel", "parallel"),
        name="gmlp",
    )(h, w_uv, ln_g.reshape(1, GM_WIDTH), ln_b.reshape(1, GM_WIDTH), ws, bias_full)


def _hgrn_proj_kernel(h_ref, w_ref, lb_ref, q_ref, kf_ref, lff_ref, kb_ref, lfb_ref, v_ref, g_ref):
    h = h_ref[0]

    def gates(z, d, cols):
        lb = lb_ref[3 * d + 0:3 * d + 1, cols]
        log_1m = lb_ref[3 * d + 1:3 * d + 2, cols]
        one_m = lb_ref[3 * d + 2:3 * d + 3, cols]
        e = jnp.exp(-jnp.abs(z))
        one_p = 1.0 + e
        r = 1.0 / one_p
        er = e * r
        pos = z >= 0.0
        b = log_1m + (jnp.minimum(z, 0.0) - jnp.log(one_p))
        logf = jnp.maximum(jnp.log(lb + one_m * jnp.where(pos, r, er)), b)
        return logf, one_m * jnp.where(pos, er, r)

    def proj(j):
        return [_dot(h, w_ref[:, seg * HG_KW + j * MXU_N:seg * HG_KW + (j + 1) * MXU_N]) for seg in range(5)]

    n_chunks = HG_KW // MXU_N
    z = proj(0)
    for j in range(n_chunks):
        z_next = proj(j + 1) if j + 1 < n_chunks else None
        cols = slice(j * MXU_N, (j + 1) * MXU_N)
        q_ref[0, :, cols] = _silu(z[0]).astype(BF16)
        lf, k = gates(z[1], 0, cols)
        lff_ref[0, :, cols] = lf
        kf_ref[0, :, cols] = k.astype(BF16)
        lf, k = gates(z[2], 1, cols)
        lfb_ref[0, :, cols] = lf
        kb_ref[0, :, cols] = k.astype(BF16)
        v_ref[0, :, cols] = z[3].astype(BF16)
        g_ref[0, :, cols] = _silu(z[4]).astype(BF16)
        z = z_next


def _hgrn_proj(h, w_hg, lb_tab, tm):
    B, T, D = h.shape
    out = pl.BlockSpec((1, tm, HG_KW), lambda b, i: (b, i, 0))
    bf = jax.ShapeDtypeStruct((B, T, HG_KW), BF16)
    f32 = jax.ShapeDtypeStruct((B, T, HG_KW), F32)
    return pl.pallas_call(
        _hgrn_proj_kernel,
        grid=(B, T // tm),
        in_specs=[pl.BlockSpec((1, tm, D), lambda b, i: (b, i, 0)), _resident(w_hg.shape),
                  _resident(lb_tab.shape)],
        out_specs=[out] * 7,
        out_shape=[bf, bf, f32, bf, f32, bf, bf],
        compiler_params=_params("parallel", "parallel"),
        name="hgrn_proj",
    )(h, w_hg, lb_tab)


def _split2(x):
    a = x.astype(BF16)
    return a, (x - a.astype(F32)).astype(BF16)


def _hgrn_decay_matrix():
    C = HG_CHUNK
    t = np.arange(C)[:, None]
    u = np.arange(C)[None, :]
    out = []
    for rev in (False, True):
        blocks = [u >= t] if rev else [u <= t]
        for l in range(1, HG_LEVELS):
            half = 1 << l
            mid = (t & -(2 * half)) + half
            upper = (t & half) != 0
            if rev:
                blocks.append(np.where(upper, (u >= mid) & (u < t), (u >= t) & (u < mid)))
            else:
                blocks.append(np.where(upper, (u >= mid) & (u <= t), (u > t) & (u < mid)))
        m = np.concatenate(blocks, axis=0).astype(np.float32)
        out += [m, m]
    return jnp.asarray(np.concatenate(out, axis=1), dtype=BF16)


def _hgrn_local(blocks, dmat, consts):
    C = HG_CHUNK
    G = range(len(blocks))
    level_of_pair, eye, r_in = consts
    qf = [blocks[g][0].astype(F32) for g in G]
    kf = [(blocks[g][1].astype(F32), blocks[g][2].astype(F32)) for g in G]
    z = jnp.zeros((C, HG_DK), BF16)

    def block_diag(lf_f, lf_b):
        f1, f2 = _split2(lf_f)
        b1, b2 = _split2(lf_b)
        return jnp.concatenate([jnp.concatenate([f1, z], axis=1), jnp.concatenate([f2, z], axis=1),
                                jnp.concatenate([z, b1], axis=1), jnp.concatenate([z, b2], axis=1)], axis=0)

    logs = [_dot(dmat, block_diag(blocks[g][3], blocks[g][4])) for g in G]
    logs = [(logs[g][:, :HG_DK], logs[g][:, HG_DK:]) for g in G]
    ex = [[jnp.exp(logs[g][d]) for d in range(2)] for g in G]
    q_in = [[(qf[g] * ex[g][d][0:C]).astype(BF16) for d in range(2)] for g in G]
    tot = [[logs[g][0][C - 1:C], logs[g][1][0:1]] for g in G]
    k_out = [[(kf[g][d] * jnp.exp(tot[g][d] - logs[g][d][0:C])).astype(BF16) for d in range(2)] for g in G]
    g_tot = [[ex[g][0][C - 1:C], ex[g][1][0:1]] for g in G]
    scores = [jnp.where(eye, jnp.sum(qf[g] * (kf[g][0] + kf[g][1]), axis=-1, keepdims=True), 0.0) for g in G]
    for l in range(HG_LEVELS):
        upper = (r_in & (1 << l)) != 0
        rows = slice(l * C, (l + 1) * C)
        for g in G:
            if l == 0:
                eq = [jnp.exp(blocks[g][3 + d]) for d in range(2)]
                kq = kf[g]
            else:
                eq = [ex[g][d][rows] for d in range(2)]
                kq = [kf[g][d] * eq[d] for d in range(2)]
            qs = jnp.concatenate([jnp.where(upper, qf[g] * eq[0], 0.0), jnp.where(upper, 0.0, qf[g] * eq[1])], axis=1)
            ks = jnp.concatenate([jnp.where(upper, 0.0, kq[0]), jnp.where(upper, kq[1], 0.0)], axis=1)
            scores[g] = jnp.where(level_of_pair == l + 1, _dot_nt(qs.astype(BF16), ks.astype(BF16)), scores[g])
    o = [_dot(scores[g].astype(BF16), blocks[g][5]) for g in G]
    upd = [_dot(blocks[g][5].astype(F32).T.astype(BF16), jnp.concatenate(k_out[g], axis=1)) for g in G]
    return [jnp.concatenate(q_in[g], axis=1) for g in G], upd, g_tot, o


def _hgrn_scan_kernel(g_ref, d_ref, q_ref, kf_ref, lff_ref, kb_ref, lfb_ref, v_ref, og_ref,
                      cq_ref, ckf_ref, clff_ref, ckb_ref, clfb_ref, cv_ref, cog_ref,
                      y_ref, yc_ref, o_ref, qin_ref, upd_ref, hist_ref, gtot_ref, st_ref, *, T, L):
    C = HG_CHUNK
    row_id = lax.broadcasted_iota(jnp.int32, (C, C), 0)
    col_id = lax.broadcasted_iota(jnp.int32, (C, C), 1)
    r_in = lax.broadcasted_iota(jnp.int32, (C, HG_DK), 0)
    diff = row_id ^ col_id
    level_of_pair = functools.reduce(jnp.add, [(diff >= (1 << j)).astype(jnp.int32) for j in range(HG_LEVELS)])
    consts = (level_of_pair, row_id == col_id, r_in)

    st_ref[...] = jnp.zeros_like(st_ref)

    def block_rows(c):
        return pl.ds(pl.multiple_of(c * C, C), C)

    def scan(q_r, k_rs, lf_rs, v_r, og_r, y_r, n):
        def local(i, carry):
            cs = [i * group + g for g in range(group)]
            blocks = [(q_r[0, block_rows(c), :], k_rs[0][0, block_rows(c), :], k_rs[1][0, block_rows(c), :],
                       lf_rs[0][0, block_rows(c), :], lf_rs[1][0, block_rows(c), :], v_r[0, block_rows(c), :])
                      for c in cs]
            q_in, upd, g_tot, o = _hgrn_local(blocks, d_ref[...], consts)
            for g, c in enumerate(cs):
                qin_ref[block_rows(c), :] = q_in[g]
                upd_ref[c] = upd[g]
                for d in range(2):
                    gtot_ref[d, pl.ds(pl.multiple_of(c * 8, 8), 8), :] = jnp.broadcast_to(g_tot[g][d], (8, HG_DK))
                o_ref[block_rows(c), :] = o[g]
            return carry

        def carry_states(i, carry):
            for d in range(2):
                c = i if d == 0 else n - 1 - i
                lanes = slice(d * HG_DK, (d + 1) * HG_DK)
                st = st_ref[d]
                hist_ref[c, :, lanes] = st.astype(BF16)
                g_tot = gtot_ref[d, pl.ds(pl.multiple_of(c * 8, 8), 8), :][0:1]
                st_ref[d] = st * g_tot + upd_ref[c, :, lanes]
            return carry

        group = min(n, HG_GROUP)
        out_group = min(n, HG_OUT_GROUP)

        def outputs(i, carry):
            cs = [i * out_group + g for g in range(out_group)]
            o = [o_ref[block_rows(c), :] + _dot_nt(qin_ref[block_rows(c), :], hist_ref[c]) for c in cs]
            for c, o_c in zip(cs, o):
                y_r[0, block_rows(c), :] = (_rms(o_c, g_ref[...]) * og_r[0, block_rows(c), :].astype(F32)
                                            ).astype(BF16)
            return carry

        lax.fori_loop(0, n // group, local, 0)
        lax.fori_loop(0, n, carry_states, 0, unroll=4)
        lax.fori_loop(0, n // out_group, outputs, 0)

    scan(cq_ref, (ckf_ref, ckb_ref), (clff_ref, clfb_ref), cv_ref, cog_ref, yc_ref, L // C)
    scan(q_ref, (kf_ref, kb_ref), (lff_ref, lfb_ref), v_ref, og_ref, y_ref, T // C)


def _hgrn_scan(lat, ctx, g):
    B, T, _ = lat[0].shape
    L = ctx[0].shape[1]
    for n in (T // HG_CHUNK, L // HG_CHUNK):
        assert n * HG_CHUNK in (T, L) and n % min(n, HG_GROUP) == 0 and n % min(n, HG_OUT_GROUP) == 0
    dmat = _hgrn_decay_matrix()
    lat_spec = pl.BlockSpec((1, T, LANES), lambda b, h: (b, 0, h))
    ctx_spec = pl.BlockSpec((1, L, LANES), lambda b, h: (b, 0, h))
    return pl.pallas_call(
        functools.partial(_hgrn_scan_kernel, T=T, L=L),
        grid=(B, HG_HEADS),
        in_specs=[pl.BlockSpec((1, HG_DV), lambda b, h: (0, 0)), _resident(dmat.shape)]
        + [lat_spec] * 7 + [ctx_spec] * 7,
        out_specs=[lat_spec, ctx_spec],
        out_shape=[jax.ShapeDtypeStruct((B, T, HG_VW), BF16), jax.ShapeDtypeStruct((B, L, HG_VW), BF16)],
        scratch_shapes=[
            pltpu.VMEM((T, HG_DV), F32),
            pltpu.VMEM((T, 2 * HG_DK), BF16),
            pltpu.VMEM((T // HG_CHUNK, HG_DV, 2 * HG_DK), F32),
            pltpu.VMEM((T // HG_CHUNK, HG_DV, 2 * HG_DK), BF16),
            pltpu.VMEM((2, T // HG_CHUNK * 8, HG_DK), F32),
            pltpu.VMEM((2, HG_DV, HG_DK), F32),
        ],
        compiler_params=_params("parallel", "parallel"),
        name="hgrn_scan",
    )(g.reshape(1, HG_DV), dmat, *lat, *ctx)


def _merge_kernel(x_ref, h_ref, ya_ref, yg_ref, yh_ref, wgate_ref, wa_ref, wg_ref, wh_ref, wo_ref,
                  gpost_ref, gt_ref, gpre_ref, sh_ref, sc_ref, x1_ref, h2_ref):
    h = h_ref[0]
    D = D_MODEL
    y = _sigmoid(_dot(h, wgate_ref[:, :D])) * _dot(ya_ref[0], wa_ref[...])
    y = y + _sigmoid(_dot(h, wgate_ref[:, D:2 * D])) * _dot(yg_ref[0], wg_ref[...])
    y = y + _sigmoid(_dot(h, wgate_ref[:, 2 * D:])) * _dot(yh_ref[0], wh_ref[...])
    z = _dot(y.astype(BF16), wo_ref[...])
    x1 = x_ref[0] + gt_ref[0] * _rms(z, gpost_ref[...])
    x1_ref[0] = x1
    h2_ref[0] = (_rms(x1, gpre_ref[...]) * (1.0 + sc_ref[0]) + sh_ref[0]).astype(BF16)


def _merge(x, h, ya, yg, yh, wgate, wa, wg, wh, wo, g_post, gt, g_pre, sh, sc, tm):
    B, T, D = x.shape
    row = pl.BlockSpec((1, tm, D), lambda b, i: (b, i, 0))
    vec = pl.BlockSpec((1, 1, D), lambda b, i: (b, 0, 0))
    par = pl.BlockSpec((1, D), lambda b, i: (0, 0))
    return pl.pallas_call(
        _merge_kernel,
        grid=(B, T // tm),
        in_specs=[row] * 5 + [_resident(wgate.shape)] + [_resident(wa.shape)] * 4 + [par, vec, par, vec, vec],
        out_specs=[row, row],
        out_shape=[jax.ShapeDtypeStruct((B, T, D), F32), jax.ShapeDtypeStruct((B, T, D), BF16)],
        compiler_params=_params("parallel", "parallel"),
        name="merge",
    )(x, h, ya, yg, yh, wgate, wa, wg, wh, wo, g_post.reshape(1, D), gt, g_pre.reshape(1, D), sh, sc)


def _ffn_kernel(x_ref, h_ref, hp_ref, hn_ref, wup_ref, cw_ref, cb_ref, wdn_ref, gpost_ref, gt_ref,
                o_ref, act_ref, *, tm):
    i = pl.program_id(1)
    last = pl.num_programs(1) - 1
    halo = BF16_ROWS
    hp = jnp.where(i > 0, hp_ref[0], jnp.zeros_like(hp_ref[0]))
    hn = jnp.where(i < last, hn_ref[0], jnp.zeros_like(hn_ref[0]))
    h_ext = jnp.concatenate([hp, h_ref[0], hn], axis=0)
    rows = tm + 2 * halo

    def up_conv(cols):
        u = _dot(h_ext, wup_ref[:, cols])
        cw = cw_ref[:, cols]
        u_prev = pltpu.roll(u, 1, 0)
        u_next = pltpu.roll(u, rows - 1, 0)
        return (u_prev * cw[0:1] + u * cw[1:2] + u_next * cw[2:3] + cb_ref[:, cols])[halo:halo + tm]

    for n in range(D_FF // FF_CHUNK):
        a = up_conv(slice(n * FF_CHUNK, (n + 1) * FF_CHUNK))
        b = up_conv(slice(D_FF + n * FF_CHUNK, D_FF + (n + 1) * FF_CHUNK))
        act_ref[:, n * FF_CHUNK:(n + 1) * FF_CHUNK] = (_silu(a) * b).astype(BF16)
    y = _dot(act_ref[...], wdn_ref[...])
    o_ref[0] = x_ref[0] + gt_ref[0] * _rms(y, gpost_ref[...])


def _ffn(x, h, wup, cw, cb, wdn, g_post, gt, tm):
    B, T, D = x.shape
    nh = tm // BF16_ROWS
    row = pl.BlockSpec((1, tm, D), lambda b, i: (b, i, 0))
    prev = pl.BlockSpec((1, BF16_ROWS, D), lambda b, i: (b, jnp.maximum(i * nh - 1, 0), 0))
    nxt = pl.BlockSpec((1, BF16_ROWS, D), lambda b, i: (b, jnp.minimum((i + 1) * nh, T // BF16_ROWS - 1), 0))
    return pl.pallas_call(
        functools.partial(_ffn_kernel, tm=tm),
        grid=(B, T // tm),
        in_specs=[row, row, prev, nxt, _resident(wup.shape), _resident(cw.shape), _resident(cb.shape),
                  _resident(wdn.shape), pl.BlockSpec((1, D), lambda b, i: (0, 0)),
                  pl.BlockSpec((1, 1, D), lambda b, i: (b, 0, 0))],
        out_specs=row,
        out_shape=jax.ShapeDtypeStruct((B, T, D), F32),
        scratch_shapes=[pltpu.VMEM((tm, D_FF), BF16)],
        compiler_params=_params("parallel", "parallel"),
        name="ffn",
    )(x, h, h, h, wup, cw, cb, wdn, g_post.reshape(1, D), gt)


def _rope_tables(T):
    half = ATT_DH // 4
    inv = ROPE_BASE ** (-jnp.arange(half, dtype=F32) / half)
    n_rows = T // GRID_W
    ang_r = jnp.arange(n_rows, dtype=F32)[:, None] * inv[None, :]
    ang_c = jnp.arange(GRID_W, dtype=F32)[:, None] * inv[None, :]

    def by_row(a):
        return jnp.repeat(a, GRID_W, axis=0)

    def by_col(a):
        return jnp.tile(a, (n_rows, 1))

    cos_r, sin_r = by_row(jnp.cos(ang_r)), by_row(jnp.sin(ang_r))
    cos_c, sin_c = by_col(jnp.cos(ang_c)), by_col(jnp.sin(ang_c))
    z = jnp.zeros_like(cos_r)
    cos64 = jnp.concatenate([cos_r, cos_r, cos_c, cos_c], axis=1)
    sa64 = jnp.concatenate([-sin_r, z, -sin_c, z], axis=1)
    sb64 = jnp.concatenate([z, sin_r, z, sin_c], axis=1)
    return tuple(jnp.concatenate([a, a], axis=1) for a in (cos64, sa64, sb64))


def _col_offsets():
    offs, start = [], 0
    for w in IN_SPLITS:
        offs.append(start)
        start += w
    return offs


def kernel(x, c, ctx, c_ctx, w_ada, b_ada, g_pre_mix, g_post_mix, g_pre_ffn, g_post_ffn, w_in, lam_q1, lam_k1, lam_q2, lam_k2, att_subln_g, gm_ln_g, gm_ln_b, gm_ws, gm_bs, hg_lb, hg_norm_g, w_br_att, w_br_gm, w_br_hg, w_out, w_up, conv_w, conv_b, w_down):
    B, T, D = x.shape
    L = ctx.shape[1]
    TM, TQ = ROW_TILE, WIDE_ROW_TILE
    assert T % WIDE_ROW_TILE == 0 and (B * L) % WIDE_ROW_TILE == 0 and B + 1 <= ADA_ROWS

    c_all = jnp.concatenate([c, c_ctx[None, :], jnp.zeros((ADA_ROWS - B - 1, D), F32)], axis=0)
    mod = _ada(c_all, w_ada, b_ada)

    lb_all = jnp.cumsum(jax.nn.softmax(hg_lb.astype(F32), axis=0), axis=0)
    lb_all = lb_all - lb_all[0]
    tables = _rope_tables(T)
    offs = _col_offsets()

    xc = ctx
    for l in range(DEPTH):
        last = l == DEPTH - 1
        lam_init = 0.8 - 0.6 * math.exp(-0.3 * l)
        lam = (jnp.exp(jnp.sum(lam_q1[l] * lam_k1[l])) - jnp.exp(jnp.sum(lam_q2[l] * lam_k2[l])) + lam_init)

        def mods(k, ctx_row):
            m = mod[l, :, k * D:(k + 1) * D]
            if ctx_row:
                return jnp.broadcast_to(m[B:B + 1], (B, D)).reshape(B, 1, D)
            return m[:B].reshape(B, 1, D)

        w_l = w_in[l].astype(BF16)
        w_qkv = w_l[:, offs[0]:offs[3]]
        w_uv = w_l[:, offs[3]:offs[5]]
        w_hg = w_l[:, offs[5]:offs[10]]
        w_gate = w_l[:, offs[10]:]
        lb = lb_all[l]
        lb_tab = jnp.stack([lb[0], jnp.log1p(-lb[0]), 1.0 - lb[0],
                            lb[1], jnp.log1p(-lb[1]), 1.0 - lb[1],
                            jnp.zeros_like(lb[0]), jnp.zeros_like(lb[0])], axis=0)
        ws = gm_ws[l].astype(BF16)
        dg = GM_WIDTH // GM_GROUPS
        bias_full = jnp.broadcast_to(gm_bs[l].T[:, :, None], (GM_CHUNK, GM_GROUPS, dg)).reshape(GM_CHUNK, GM_WIDTH)
        wa, wg, wh, wo = (w[l].astype(BF16) for w in (w_br_att, w_br_gm, w_br_hg, w_out))
        wup = w_up[l].astype(BF16)
        cw = conv_w[l]
        cb = conv_b[l].reshape(1, 2 * D_FF)
        wdn = w_down[l].astype(BF16)

        def flat(a):
            return a.reshape(1, B * L, a.shape[-1])

        def unflat(a):
            return a.reshape(B, L, a.shape[-1])

        def cmod(k):
            return mod[l, B:B + 1, k * D:(k + 1) * D].reshape(1, 1, D)

        hc = _prenorm(flat(xc), g_pre_mix[l], cmod(0), cmod(1), TM)
        h = _prenorm(x, g_pre_mix[l], mods(0, False), mods(1, False), TM)
        cq, ck, cv = (unflat(a) for a in _att_proj(hc, w_qkv, None, TM))
        q, k, v = _att_proj(h, w_qkv, tables, WIDE_ROW_TILE)
        c_hg = [unflat(a) for a in _hgrn_proj(hc, w_hg, lb_tab, TM)]
        l_hg = _hgrn_proj(h, w_hg, lb_tab, WIDE_ROW_TILE)

        y_att = _attention(q, [(k, v), (ck, cv)], lam, att_subln_g[l], 1.0 - lam_init, TQ)
        y_gm = _gmlp(h, w_uv, gm_ln_g[l], gm_ln_b[l], ws, bias_full, WIDE_ROW_TILE)
        y_hg, yc_hg = _hgrn_scan(l_hg, c_hg, hg_norm_g[l])

        x, h2 = _merge(x, h, y_att, y_gm, y_hg, w_gate, wa, wg, wh, wo, g_post_mix[l], mods(2, False),
                       g_pre_ffn[l], mods(3, False), mods(4, False), TM)
        x = _ffn(x, h2, wup, cw, cb, wdn, g_post_ffn[l], mods(5, False), WIDE_ROW_TILE)

        if not last:
            yc_att = _attention(cq, [(ck, cv)], lam, att_subln_g[l], 1.0 - lam_init, L)
            yc_gm = _gmlp(hc, w_uv, gm_ln_g[l], gm_ln_b[l], ws, bias_full, TM)
            xc, hc2 = _merge(flat(xc), hc, flat(yc_att), yc_gm, flat(yc_hg), w_gate, wa, wg, wh, wo, g_post_mix[l],
                             cmod(2), g_pre_ffn[l], cmod(3), cmod(4), TM)
            xc = _ffn(unflat(xc), unflat(hc2), wup, cw, cb, wdn, g_post_ffn[l], mods(5, True), L)
    return x
```

```python
import functools
import math

import jax
import jax.numpy as jnp
import numpy as np
from jax import lax
from jax.experimental import pallas as pl
from jax.experimental.pallas import tpu as pltpu

F32 = jnp.float32
BF16 = jnp.bfloat16

D_MODEL = 1024
DEPTH = 2
GRID_W = 64
ATT_HEADS = 8
ATT_DH = 64
ATT_DV = 2 * ATT_DH
GM_WIDTH = 1024
GM_GROUPS = 8
GM_CHUNK = 128
HG_HEADS = 8
HG_DK = 128
HG_DV = 128
HG_CHUNK = 64
HG_LEVELS = 6
HG_GROUP = 8
HG_OUT_GROUP = 16
N_BRANCH = 3
D_FF = 2816
FF_CHUNK = 256
CONV_W = 3
ROPE_BASE = 10000.0
EPS = 1e-6

ATT_QW = ATT_HEADS * 2 * ATT_DH
ATT_VW = ATT_HEADS * ATT_DV
HG_KW = HG_HEADS * HG_DK
HG_VW = HG_HEADS * HG_DV
IN_SPLITS = (ATT_QW, ATT_QW, ATT_VW, GM_WIDTH, GM_WIDTH, HG_KW, HG_KW, HG_KW, HG_VW, HG_VW,
             N_BRANCH * D_MODEL)

ROW_TILE = 512
WIDE_ROW_TILE = 1024
ADA_ROWS = 16
ADA_COL_TILE = 1536

LANES = 128
BF16_ROWS = 16
MXU_N = 256
VMEM_LIMIT = 56 * 1024 * 1024

NEG_INF = float("-inf")
ATT_Q_SCALE = ATT_DH ** -0.5 * math.log2(math.e)
ATT_KEY_CHUNK = 512


def _params(*sem):
    return pltpu.CompilerParams(dimension_semantics=sem, vmem_limit_bytes=VMEM_LIMIT)


def _resident(shape):
    nd = len(shape)
    return pl.BlockSpec(shape, lambda *_: (0,) * nd, pipeline_mode=pl.Buffered(1))


def _dot(a, b):
    return jnp.dot(a, b, preferred_element_type=F32)


def _dot_nt(a, b):
    return lax.dot_general(a, b, (((1,), (1,)), ((), ())), preferred_element_type=F32)


def _rms(x, g):
    return x * lax.rsqrt(jnp.mean(x * x, axis=-1, keepdims=True) + EPS) * g


def _sigmoid(x):
    return 0.5 * jnp.tanh(0.5 * x) + 0.5


def _silu(x):
    h = 0.5 * x
    return h + h * jnp.tanh(h)


def _gelu_tanh(x):
    return 0.5 * x * (1.0 + jnp.tanh(math.sqrt(2.0 / math.pi) * (x + 0.044715 * (x * x * x))))


def _ada_kernel(c_ref, w_ref, b_ref, o_ref):
    c = c_ref[...]
    o_ref[0] = _dot(_silu(c).astype(BF16), w_ref[0].astype(BF16)) + b_ref[0]


def _ada(c_all, w_ada, b_ada):
    rows = c_all.shape[0]
    n = w_ada.shape[-1]
    tn = ADA_COL_TILE
    return pl.pallas_call(
        _ada_kernel,
        grid=(DEPTH, n // tn),
        in_specs=[
            pl.BlockSpec((rows, D_MODEL), lambda l, j: (0, 0)),
            pl.BlockSpec((1, D_MODEL, tn), lambda l, j: (l, 0, j)),
            pl.BlockSpec((1, 1, tn), lambda l, j: (l, 0, j)),
        ],
        out_specs=pl.BlockSpec((1, rows, tn), lambda l, j: (l, 0, j)),
        out_shape=jax.ShapeDtypeStruct((DEPTH, rows, n), F32),
        compiler_params=_params("parallel", "parallel"),
        name="ada",
    )(c_all, w_ada, b_ada.reshape(DEPTH, 1, n))


def _prenorm_kernel(x_ref, g_ref, sh_ref, sc_ref, h_ref):
    y = _rms(x_ref[0], g_ref[...])
    h_ref[0] = (y * (1.0 + sc_ref[0]) + sh_ref[0]).astype(BF16)


def _prenorm(x, g, sh, sc, tm):
    B, T, D = x.shape
    row = pl.BlockSpec((1, tm, D), lambda b, i: (b, i, 0))
    vec = pl.BlockSpec((1, 1, D), lambda b, i: (b, 0, 0))
    return pl.pallas_call(
        _prenorm_kernel,
        grid=(B, T // tm),
        in_specs=[row, pl.BlockSpec((1, D), lambda b, i: (0, 0)), vec, vec],
        out_specs=row,
        out_shape=jax.ShapeDtypeStruct((B, T, D), BF16),
        compiler_params=_params("parallel", "parallel"),
        name="prenorm",
    )(x, g.reshape(1, D), sh, sc)


def _att_proj_kernel(h_ref, w_ref, *rest, rope):
    if rope:
        cos_ref, sa_ref, sb_ref, q_ref, k_ref, v_ref = rest
        cos, sa, sb = cos_ref[...], sa_ref[...], sb_ref[...]
    else:
        q_ref, k_ref, v_ref = rest
    h = h_ref[0]

    def rot(a):
        half = ATT_DH // 4
        return a * cos + pltpu.roll(a, LANES - half, 1) * sa + pltpu.roll(a, half, 1) * sb

    for j in range(ATT_QW // MXU_N):
        cols = slice(j * MXU_N, (j + 1) * MXU_N)
        aq = _dot(h, w_ref[:, j * MXU_N:(j + 1) * MXU_N])
        ak = _dot(h, w_ref[:, ATT_QW + j * MXU_N:ATT_QW + (j + 1) * MXU_N])
        av = _dot(h, w_ref[:, 2 * ATT_QW + j * MXU_N:2 * ATT_QW + (j + 1) * MXU_N])
        if rope:
            aq = jnp.concatenate([rot(aq[:, :LANES]), rot(aq[:, LANES:])], axis=1)
            ak = jnp.concatenate([rot(ak[:, :LANES]), rot(ak[:, LANES:])], axis=1)
        q_ref[0, :, cols] = (aq * ATT_Q_SCALE).astype(BF16)
        k_ref[0, :, cols] = ak.astype(BF16)
        v_ref[0, :, cols] = av.astype(BF16)


def _att_proj(h, w_qkv, tables, tm):
    B, T, D = h.shape
    rope = tables is not None
    row = pl.BlockSpec((1, tm, D), lambda b, i: (b, i, 0))
    out = pl.BlockSpec((1, tm, ATT_QW), lambda b, i: (b, i, 0))
    in_specs = [row, _resident(w_qkv.shape)]
    args = [h, w_qkv]
    if rope:
        in_specs += [pl.BlockSpec((tm, LANES), lambda b, i: (i, 0))] * 3
        args += list(tables)
    return pl.pallas_call(
        functools.partial(_att_proj_kernel, rope=rope),
        grid=(B, T // tm),
        in_specs=in_specs,
        out_specs=[out, out, out],
        out_shape=[jax.ShapeDtypeStruct((B, T, ATT_QW), BF16)] * 3,
        compiler_params=_params("parallel", "parallel"),
        name="att_proj",
    )(*args)


def _attn_kernel(lam_ref, q_ref, g_ref, *rest, n_kv, post_scale):
    kv = rest[:2 * n_kv]
    o_ref = rest[2 * n_kv]
    vext = rest[2 * n_kv + 1:]

    @pl.when(pl.program_id(2) == 0)
    def _():
        for j in range(n_kv):
            v = kv[2 * j + 1][0]
            vext[j][:, :LANES] = v
            vext[j][:, LANES:] = jnp.ones_like(v)

    q = q_ref[0]
    lane = lax.broadcasted_iota(jnp.int32, q.shape, 1)
    zero = jnp.zeros_like(q)
    outs = []
    for i in range(2):
        qi = jnp.where(lane < ATT_DH, q, zero) if i == 0 else jnp.where(lane >= ATT_DH, q, zero)
        m = acc = None
        for j in range(n_kv):
            n_keys = kv[2 * j].shape[1]
            kc = min(n_keys, ATT_KEY_CHUNK)
            for c0 in range(0, n_keys, kc):
                s = _dot_nt(qi, kv[2 * j][0, c0:c0 + kc, :])
                mc = jnp.max(s, axis=-1, keepdims=True)
                m_new = mc if m is None else jnp.maximum(m, mc)
                pv = _dot(jnp.exp2(s - m_new).astype(BF16), vext[j][c0:c0 + kc, :])
                acc = pv if m is None else acc * jnp.exp2(m - m_new) + pv
                m = m_new
        outs.append(acc[:, :LANES] / acc[:, LANES:])
    o = outs[0] - lam_ref[0] * outs[1]
    o_ref[0] = (_rms(o, g_ref[...]) * post_scale).astype(BF16)


def _attention(q, kvs, lam, g, post_scale, tq):
    B, T, _ = q.shape
    in_specs = [
        pl.BlockSpec(memory_space=pltpu.SMEM),
        pl.BlockSpec((1, tq, LANES), lambda b, h, i: (b, i, h)),
        pl.BlockSpec((1, LANES), lambda b, h, i: (0, 0)),
    ]
    args = [lam.reshape(1), q, g.reshape(1, ATT_DV)]
    scratch = []
    for k, v in kvs:
        spec = pl.BlockSpec((1, k.shape[1], LANES), lambda b, h, i: (b, 0, h))
        in_specs += [spec, spec]
        args += [k, v]
        scratch.append(pltpu.VMEM((k.shape[1], 2 * LANES), BF16))
    return pl.pallas_call(
        functools.partial(_attn_kernel, n_kv=len(kvs), post_scale=post_scale),
        grid=(B, ATT_HEADS, T // tq),
        in_specs=in_specs,
        out_specs=pl.BlockSpec((1, tq, LANES), lambda b, h, i: (b, i, h)),
        out_shape=jax.ShapeDtypeStruct((B, T, ATT_VW), BF16),
        scratch_shapes=scratch,
        compiler_params=_params("parallel", "parallel", "arbitrary"),
        name="attention",
    )(*args)


def _gmlp_kernel(h_ref, w_ref, lng_ref, lnb_ref, ws_ref, bias_ref, o_ref, *, tm):
    h = h_ref[0]
    u = _gelu_tanh(_dot(h, w_ref[:, :GM_WIDTH]))
    v = _gelu_tanh(_dot(h, w_ref[:, GM_WIDTH:]))
    mu = jnp.mean(v, axis=-1, keepdims=True)
    vc = v - mu
    var = jnp.mean(vc * vc, axis=-1, keepdims=True)
    vn = (vc * lax.rsqrt(var + EPS) * lng_ref[...] + lnb_ref[...]).astype(BF16)
    dg = GM_WIDTH // GM_GROUPS
    for r in range(tm // GM_CHUNK):
        rows = slice(r * GM_CHUNK, (r + 1) * GM_CHUNK)
        for g in range(GM_GROUPS):
            cols = slice(g * dg, (g + 1) * dg)
            s = _dot(ws_ref[g], vn[rows, cols]) + bias_ref[:, cols]
            o_ref[0, rows, cols] = (u[rows, cols] * s).astype(BF16)


def _gmlp(h, w_uv, ln_g, ln_b, ws, bias_full, tm):
    B, T, D = h.shape
    return pl.pallas_call(
        functools.partial(_gmlp_kernel, tm=tm),
        grid=(B, T // tm),
        in_specs=[
            pl.BlockSpec((1, tm, D), lambda b, i: (b, i, 0)),
            _resident(w_uv.shape),
            _resident((1, GM_WIDTH)),
            _resident((1, GM_WIDTH)),
            _resident(ws.shape),
            _resident(bias_full.shape),
        ],
        out_specs=pl.BlockSpec((1, tm, GM_WIDTH), lambda b, i: (b, i, 0)),
        out_shape=jax.ShapeDtypeStruct((B, T, GM_WIDTH), BF16),
        compiler_params=_params("parallel", "parallel"),
        name="gmlp",
    )(h, w_uv, ln_g.reshape(1, GM_WIDTH), ln_b.reshape(1, GM_WIDTH), ws, bias_full)


def _hgrn_proj_kernel(h_ref, w_ref, lb_ref, q_ref, kf_ref, lff_ref, kb_ref, lfb_ref, v_ref, g_ref):
    h = h_ref[0]

    def gates(z, d, cols):
        lb = lb_ref[3 * d + 0:3 * d + 1, cols]
        log_1m = lb_ref[3 * d + 1:3 * d + 2, cols]
        one_m = lb_ref[3 * d + 2:3 * d + 3, cols]
        e = jnp.exp(-jnp.abs(z))
        one_p = 1.0 + e
        r = 1.0 / one_p
        er = e * r
        pos = z >= 0.0
        b = log_1m + (jnp.minimum(z, 0.0) - jnp.log(one_p))
        logf = jnp.maximum(jnp.log(lb + one_m * jnp.where(pos, r, er)), b)
        return logf, one_m * jnp.where(pos, er, r)

    def proj(j):
        return [_dot(h, w_ref[:, seg * HG_KW + j * MXU_N:seg * HG_KW + (j + 1) * MXU_N]) for seg in range(5)]

    n_chunks = HG_KW // MXU_N
    z = proj(0)
    for j in range(n_chunks):
        z_next = proj(j + 1) if j + 1 < n_chunks else None
        cols = slice(j * MXU_N, (j + 1) * MXU_N)
        q_ref[0, :, cols] = _silu(z[0]).astype(BF16)
        lf, k = gates(z[1], 0, cols)
        lff_ref[0, :, cols] = lf
        kf_ref[0, :, cols] = k.astype(BF16)
        lf, k = gates(z[2], 1, cols)
        lfb_ref[0, :, cols] = lf
        kb_ref[0, :, cols] = k.astype(BF16)
        v_ref[0, :, cols] = z[3].astype(BF16)
        g_ref[0, :, cols] = _silu(z[4]).astype(BF16)
        z = z_next


def _hgrn_proj(h, w_hg, lb_tab, tm):
    B, T, D = h.shape
    out = pl.BlockSpec((1, tm, HG_KW), lambda b, i: (b, i, 0))
    bf = jax.ShapeDtypeStruct((B, T, HG_KW), BF16)
    f32 = jax.ShapeDtypeStruct((B, T, HG_KW), F32)
    return pl.pallas_call(
        _hgrn_proj_kernel,
        grid=(B, T // tm),
        in_specs=[pl.BlockSpec((1, tm, D), lambda b, i: (b, i, 0)), _resident(w_hg.shape),
                  _resident(lb_tab.shape)],
        out_specs=[out] * 7,
        out_shape=[bf, bf, f32, bf, f32, bf, bf],
        compiler_params=_params("parallel", "parallel"),
        name="hgrn_proj",
    )(h, w_hg, lb_tab)


def _split2(x):
    a = x.astype(BF16)
    return a, (x - a.astype(F32)).astype(BF16)


def _hgrn_decay_matrix():
    C = HG_CHUNK
    t = np.arange(C)[:, None]
    u = np.arange(C)[None, :]
    out = []
    for rev in (False, True):
        blocks = [u >= t] if rev else [u <= t]
        for l in range(1, HG_LEVELS):
            half = 1 << l
            mid = (t & -(2 * half)) + half
            upper = (t & half) != 0
            if rev:
                blocks.append(np.where(upper, (u >= mid) & (u < t), (u >= t) & (u < mid)))
            else:
                blocks.append(np.where(upper, (u >= mid) & (u <= t), (u > t) & (u < mid)))
        m = np.concatenate(blocks, axis=0).astype(np.float32)
        out += [m, m]
    return jnp.asarray(np.concatenate(out, axis=1), dtype=BF16)


def _hgrn_local(blocks, dmat, consts):
    C = HG_CHUNK
    G = range(len(blocks))
    level_of_pair, eye, r_in = consts
    qf = [blocks[g][0].astype(F32) for g in G]
    kf = [(blocks[g][1].astype(F32), blocks[g][2].astype(F32)) for g in G]
    z = jnp.zeros((C, HG_DK), BF16)

    def block_diag(lf_f, lf_b):
        f1, f2 = _split2(lf_f)
        b1, b2 = _split2(lf_b)
        return jnp.concatenate([jnp.concatenate([f1, z], axis=1), jnp.concatenate([f2, z], axis=1),
                                jnp.concatenate([z, b1], axis=1), jnp.concatenate([z, b2], axis=1)], axis=0)

    logs = [_dot(dmat, block_diag(blocks[g][3], blocks[g][4])) for g in G]
    logs = [(logs[g][:, :HG_DK], logs[g][:, HG_DK:]) for g in G]
    ex = [[jnp.exp(logs[g][d]) for d in range(2)] for g in G]
    q_in = [[(qf[g] * ex[g][d][0:C]).astype(BF16) for d in range(2)] for g in G]
    tot = [[logs[g][0][C - 1:C], logs[g][1][0:1]] for g in G]
    k_out = [[(kf[g][d] * jnp.exp(tot[g][d] - logs[g][d][0:C])).astype(BF16) for d in range(2)] for g in G]
    g_tot = [[ex[g][0][C - 1:C], ex[g][1][0:1]] for g in G]
    scores = [jnp.where(eye, jnp.sum(qf[g] * (kf[g][0] + kf[g][1]), axis=-1, keepdims=True), 0.0) for g in G]
    odd = (r_in & 1) != 0
    for g in G:
        k_prev = pltpu.roll(kf[g][0], 1, 0) * jnp.exp(blocks[g][3])
        k_next = pltpu.roll(kf[g][1], C - 1, 0) * jnp.exp(blocks[g][4])
        pair = jnp.sum(qf[g] * jnp.where(odd, k_prev, k_next), axis=-1, keepdims=True)
        scores[g] = jnp.where(level_of_pair == 1, pair, scores[g])
    for l in range(1, HG_LEVELS):
        upper = (r_in & (1 << l)) != 0
        rows = slice(l * C, (l + 1) * C)
        for g in G:
            qe = [qf[g] * ex[g][d][rows] for d in range(2)]
            ke = [kf[g][d] * ex[g][d][rows] for d in range(2)]
            qs = jnp.concatenate([jnp.where(upper, qe[0], 0.0), jnp.where(upper, 0.0, qe[1])], axis=1)
            ks = jnp.concatenate([jnp.where(upper, 0.0, ke[0]), jnp.where(upper, ke[1], 0.0)], axis=1)
            scores[g] = jnp.where(level_of_pair == l + 1, _dot_nt(qs.astype(BF16), ks.astype(BF16)), scores[g])
    o = [_dot(scores[g].astype(BF16), blocks[g][5]) for g in G]
    upd = [_dot(blocks[g][5].astype(F32).T.astype(BF16), jnp.concatenate(k_out[g], axis=1)) for g in G]
    return [jnp.concatenate(q_in[g], axis=1) for g in G], upd, g_tot, o


def _hgrn_scan_kernel(g_ref, d_ref, q_ref, kf_ref, lff_ref, kb_ref, lfb_ref, v_ref, og_ref,
                      cq_ref, ckf_ref, clff_ref, ckb_ref, clfb_ref, cv_ref, cog_ref,
                      y_ref, yc_ref, o_ref, qin_ref, upd_ref, hist_ref, gtot_ref, st_ref, *, T, L):
    C = HG_CHUNK
    row_id = lax.broadcasted_iota(jnp.int32, (C, C), 0)
    col_id = lax.broadcasted_iota(jnp.int32, (C, C), 1)
    r_in = lax.broadcasted_iota(jnp.int32, (C, HG_DK), 0)
    diff = row_id ^ col_id
    level_of_pair = functools.reduce(jnp.add, [(diff >= (1 << j)).astype(jnp.int32) for j in range(HG_LEVELS)])
    consts = (level_of_pair, row_id == col_id, r_in)

    st_ref[...] = jnp.zeros_like(st_ref)

    def block_rows(c):
        return pl.ds(pl.multiple_of(c * C, C), C)

    def scan(q_r, k_rs, lf_rs, v_r, og_r, y_r, n):
        def local(i, carry):
            cs = [i * group + g for g in range(group)]
            blocks = [(q_r[0, block_rows(c), :], k_rs[0][0, block_rows(c), :], k_rs[1][0, block_rows(c), :],
                       lf_rs[0][0, block_rows(c), :], lf_rs[1][0, block_rows(c), :], v_r[0, block_rows(c), :])
                      for c in cs]
            q_in, upd, g_tot, o = _hgrn_local(blocks, d_ref[...], consts)
            for g, c in enumerate(cs):
                qin_ref[block_rows(c), :] = q_in[g]
                upd_ref[c] = upd[g]
                for d in range(2):
                    gtot_ref[d, pl.ds(pl.multiple_of(c * 8, 8), 8), :] = jnp.broadcast_to(g_tot[g][d], (8, HG_DK))
                o_ref[block_rows(c), :] = o[g]
            return carry

        def carry_states(i, carry):
            for d in range(2):
                c = i if d == 0 else n - 1 - i
                lanes = slice(d * HG_DK, (d + 1) * HG_DK)
                st = st_ref[d]
                hist_ref[c, :, lanes] = st.astype(BF16)
                g_tot = gtot_ref[d, pl.ds(pl.multiple_of(c * 8, 8), 8), :][0:1]
                st_ref[d] = st * g_tot + upd_ref[c, :, lanes]
            return carry

        group = min(n, HG_GROUP)
        out_group = min(n, HG_OUT_GROUP)

        def outputs(i, carry):
            cs = [i * out_group + g for g in range(out_group)]
            o = [o_ref[block_rows(c), :] + _dot_nt(qin_ref[block_rows(c), :], hist_ref[c]) for c in cs]
            for c, o_c in zip(cs, o):
                y_r[0, block_rows(c), :] = (_rms(o_c, g_ref[...]) * og_r[0, block_rows(c), :].astype(F32)
                                            ).astype(BF16)
            return carry

        lax.fori_loop(0, n // group, local, 0)
        lax.fori_loop(0, n, carry_states, 0, unroll=4)
        lax.fori_loop(0, n // out_group, outputs, 0)

    scan(cq_ref, (ckf_ref, ckb_ref), (clff_ref, clfb_ref), cv_ref, cog_ref, yc_ref, L // C)
    scan(q_ref, (kf_ref, kb_ref), (lff_ref, lfb_ref), v_ref, og_ref, y_ref, T // C)


def _hgrn_scan(lat, ctx, g):
    B, T, _ = lat[0].shape
    L = ctx[0].shape[1]
    for n in (T // HG_CHUNK, L // HG_CHUNK):
        assert n * HG_CHUNK in (T, L) and n % min(n, HG_GROUP) == 0 and n % min(n, HG_OUT_GROUP) == 0
    dmat = _hgrn_decay_matrix()
    lat_spec = pl.BlockSpec((1, T, LANES), lambda b, h: (b, 0, h))
    ctx_spec = pl.BlockSpec((1, L, LANES), lambda b, h: (b, 0, h))
    return pl.pallas_call(
        functools.partial(_hgrn_scan_kernel, T=T, L=L),
        grid=(B, HG_HEADS),
        in_specs=[pl.BlockSpec((1, HG_DV), lambda b, h: (0, 0)), _resident(dmat.shape)]
        + [lat_spec] * 7 + [ctx_spec] * 7,
        out_specs=[lat_spec, ctx_spec],
        out_shape=[jax.ShapeDtypeStruct((B, T, HG_VW), BF16), jax.ShapeDtypeStruct((B, L, HG_VW), BF16)],
        scratch_shapes=[
            pltpu.VMEM((T, HG_DV), F32),
            pltpu.VMEM((T, 2 * HG_DK), BF16),
            pltpu.VMEM((T // HG_CHUNK, HG_DV, 2 * HG_DK), F32),
            pltpu.VMEM((T // HG_CHUNK, HG_DV, 2 * HG_DK), BF16),
            pltpu.VMEM((2, T // HG_CHUNK * 8, HG_DK), F32),
            pltpu.VMEM((2, HG_DV, HG_DK), F32),
        ],
        compiler_params=_params("parallel", "parallel"),
        name="hgrn_scan",
    )(g.reshape(1, HG_DV), dmat, *lat, *ctx)


def _merge_kernel(x_ref, h_ref, ya_ref, yg_ref, yh_ref, wgate_ref, wa_ref, wg_ref, wh_ref, wo_ref,
                  gpost_ref, gt_ref, gpre_ref, sh_ref, sc_ref, x1_ref, h2_ref):
    h = h_ref[0]
    D = D_MODEL
    y = _sigmoid(_dot(h, wgate_ref[:, :D])) * _dot(ya_ref[0], wa_ref[...])
    y = y + _sigmoid(_dot(h, wgate_ref[:, D:2 * D])) * _dot(yg_ref[0], wg_ref[...])
    y = y + _sigmoid(_dot(h, wgate_ref[:, 2 * D:])) * _dot(yh_ref[0], wh_ref[...])
    z = _dot(y.astype(BF16), wo_ref[...])
    x1 = x_ref[0] + gt_ref[0] * _rms(z, gpost_ref[...])
    x1_ref[0] = x1
    h2_ref[0] = (_rms(x1, gpre_ref[...]) * (1.0 + sc_ref[0]) + sh_ref[0]).astype(BF16)


def _merge(x, h, ya, yg, yh, wgate, wa, wg, wh, wo, g_post, gt, g_pre, sh, sc, tm):
    B, T, D = x.shape
    row = pl.BlockSpec((1, tm, D), lambda b, i: (b, i, 0))
    vec = pl.BlockSpec((1, 1, D), lambda b, i: (b, 0, 0))
    par = pl.BlockSpec((1, D), lambda b, i: (0, 0))
    return pl.pallas_call(
        _merge_kernel,
        grid=(B, T // tm),
        in_specs=[row] * 5 + [_resident(wgate.shape)] + [_resident(wa.shape)] * 4 + [par, vec, par, vec, vec],
        out_specs=[row, row],
        out_shape=[jax.ShapeDtypeStruct((B, T, D), F32), jax.ShapeDtypeStruct((B, T, D), BF16)],
        compiler_params=_params("parallel", "parallel"),
        name="merge",
    )(x, h, ya, yg, yh, wgate, wa, wg, wh, wo, g_post.reshape(1, D), gt, g_pre.reshape(1, D), sh, sc)


def _ffn_kernel(x_ref, h_ref, hp_ref, hn_ref, wup_ref, cw_ref, cb_ref, wdn_ref, gpost_ref, gt_ref,
                o_ref, act_ref, *, tm):
    i = pl.program_id(1)
    last = pl.num_programs(1) - 1
    halo = BF16_ROWS
    hp = jnp.where(i > 0, hp_ref[0], jnp.zeros_like(hp_ref[0]))
    hn = jnp.where(i < last, hn_ref[0], jnp.zeros_like(hn_ref[0]))
    h_ext = jnp.concatenate([hp, h_ref[0], hn], axis=0)
    rows = tm + 2 * halo

    def up_conv(cols):
        u = _dot(h_ext, wup_ref[:, cols])
        cw = cw_ref[:, cols]
        u_prev = pltpu.roll(u, 1, 0)
        u_next = pltpu.roll(u, rows - 1, 0)
        return (u_prev * cw[0:1] + u * cw[1:2] + u_next * cw[2:3] + cb_ref[:, cols])[halo:halo + tm]

    for n in range(D_FF // FF_CHUNK):
        a = up_conv(slice(n * FF_CHUNK, (n + 1) * FF_CHUNK))
        b = up_conv(slice(D_FF + n * FF_CHUNK, D_FF + (n + 1) * FF_CHUNK))
        act_ref[:, n * FF_CHUNK:(n + 1) * FF_CHUNK] = (_silu(a) * b).astype(BF16)
    y = _dot(act_ref[...], wdn_ref[...])
    o_ref[0] = x_ref[0] + gt_ref[0] * _rms(y, gpost_ref[...])


def _ffn(x, h, wup, cw, cb, wdn, g_post, gt, tm):
    B, T, D = x.shape
    nh = tm // BF16_ROWS
    row = pl.BlockSpec((1, tm, D), lambda b, i: (b, i, 0))
    prev = pl.BlockSpec((1, BF16_ROWS, D), lambda b, i: (b, jnp.maximum(i * nh - 1, 0), 0))
    nxt = pl.BlockSpec((1, BF16_ROWS, D), lambda b, i: (b, jnp.minimum((i + 1) * nh, T // BF16_ROWS - 1), 0))
    return pl.pallas_call(
        functools.partial(_ffn_kernel, tm=tm),
        grid=(B, T // tm),
        in_specs=[row, row, prev, nxt, _resident(wup.shape), _resident(cw.shape), _resident(cb.shape),
                  _resident(wdn.shape), pl.BlockSpec((1, D), lambda b, i: (0, 0)),
                  pl.BlockSpec((1, 1, D), lambda b, i: (b, 0, 0))],
        out_specs=row,
        out_shape=jax.ShapeDtypeStruct((B, T, D), F32),
        scratch_shapes=[pltpu.VMEM((tm, D_FF), BF16)],
        compiler_params=_params("parallel", "parallel"),
        name="ffn",
    )(x, h, h, h, wup, cw, cb, wdn, g_post.reshape(1, D), gt)


def _rope_tables(T):
    half = ATT_DH // 4
    inv = ROPE_BASE ** (-jnp.arange(half, dtype=F32) / half)
    n_rows = T // GRID_W
    ang_r = jnp.arange(n_rows, dtype=F32)[:, None] * inv[None, :]
    ang_c = jnp.arange(GRID_W, dtype=F32)[:, None] * inv[None, :]

    def by_row(a):
        return jnp.repeat(a, GRID_W, axis=0)

    def by_col(a):
        return jnp.tile(a, (n_rows, 1))

    cos_r, sin_r = by_row(jnp.cos(ang_r)), by_row(jnp.sin(ang_r))
    cos_c, sin_c = by_col(jnp.cos(ang_c)), by_col(jnp.sin(ang_c))
    z = jnp.zeros_like(cos_r)
    cos64 = jnp.concatenate([cos_r, cos_r, cos_c, cos_c], axis=1)
    sa64 = jnp.concatenate([-sin_r, z, -sin_c, z], axis=1)
    sb64 = jnp.concatenate([z, sin_r, z, sin_c], axis=1)
    return tuple(jnp.concatenate([a, a], axis=1) for a in (cos64, sa64, sb64))


def _col_offsets():
    offs, start = [], 0
    for w in IN_SPLITS:
        offs.append(start)
        start += w
    return offs


def kernel(x, c, ctx, c_ctx, w_ada, b_ada, g_pre_mix, g_post_mix, g_pre_ffn, g_post_ffn, w_in, lam_q1, lam_k1, lam_q2, lam_k2, att_subln_g, gm_ln_g, gm_ln_b, gm_ws, gm_bs, hg_lb, hg_norm_g, w_br_att, w_br_gm, w_br_hg, w_out, w_up, conv_w, conv_b, w_down):
    B, T, D = x.shape
    L = ctx.shape[1]
    TM, TQ = ROW_TILE, WIDE_ROW_TILE
    assert T % WIDE_ROW_TILE == 0 and (B * L) % WIDE_ROW_TILE == 0 and B + 1 <= ADA_ROWS

    c_all = jnp.concatenate([c, c_ctx[None, :], jnp.zeros((ADA_ROWS - B - 1, D), F32)], axis=0)
    mod = _ada(c_all, w_ada, b_ada)

    lb_all = jnp.cumsum(jax.nn.softmax(hg_lb.astype(F32), axis=0), axis=0)
    lb_all = lb_all - lb_all[0]
    tables = _rope_tables(T)
    offs = _col_offsets()

    xc = ctx
    for l in range(DEPTH):
        last = l == DEPTH - 1
        lam_init = 0.8 - 0.6 * math.exp(-0.3 * l)
        lam = (jnp.exp(jnp.sum(lam_q1[l] * lam_k1[l])) - jnp.exp(jnp.sum(lam_q2[l] * lam_k2[l])) + lam_init)

        def mods(k, ctx_row):
            m = mod[l, :, k * D:(k + 1) * D]
            if ctx_row:
                return jnp.broadcast_to(m[B:B + 1], (B, D)).reshape(B, 1, D)
            return m[:B].reshape(B, 1, D)

        w_l = w_in[l].astype(BF16)
        w_qkv = w_l[:, offs[0]:offs[3]]
        w_uv = w_l[:, offs[3]:offs[5]]
        w_hg = w_l[:, offs[5]:offs[10]]
        w_gate = w_l[:, offs[10]:]
        lb = lb_all[l]
        lb_tab = jnp.stack([lb[0], jnp.log1p(-lb[0]), 1.0 - lb[0],
                            lb[1], jnp.log1p(-lb[1]), 1.0 - lb[1],
                            jnp.zeros_like(lb[0]), jnp.zeros_like(lb[0])], axis=0)
        ws = gm_ws[l].astype(BF16)
        dg = GM_WIDTH // GM_GROUPS
        bias_full = jnp.broadcast_to(gm_bs[l].T[:, :, None], (GM_CHUNK, GM_GROUPS, dg)).reshape(GM_CHUNK, GM_WIDTH)
        wa, wg, wh, wo = (w[l].astype(BF16) for w in (w_br_att, w_br_gm, w_br_hg, w_out))
        wup = w_up[l].astype(BF16)
        cw = conv_w[l]
        cb = conv_b[l].reshape(1, 2 * D_FF)
        wdn = w_down[l].astype(BF16)

        def flat(a):
            return a.reshape(1, B * L, a.shape[-1])

        def unflat(a):
            return a.reshape(B, L, a.shape[-1])

        def cmod(k):
            return mod[l, B:B + 1, k * D:(k + 1) * D].reshape(1, 1, D)

        hc = _prenorm(flat(xc), g_pre_mix[l], cmod(0), cmod(1), TM)
        h = _prenorm(x, g_pre_mix[l], mods(0, False), mods(1, False), TM)
        cq, ck, cv = (unflat(a) for a in _att_proj(hc, w_qkv, None, TM))
        q, k, v = _att_proj(h, w_qkv, tables, WIDE_ROW_TILE)
        c_hg = [unflat(a) for a in _hgrn_proj(hc, w_hg, lb_tab, TM)]
        l_hg = _hgrn_proj(h, w_hg, lb_tab, WIDE_ROW_TILE)

        y_att = _attention(q, [(k, v), (ck, cv)], lam, att_subln_g[l], 1.0 - lam_init, TQ)
        y_gm = _gmlp(h, w_uv, gm_ln_g[l], gm_ln_b[l], ws, bias_full, WIDE_ROW_TILE)
        y_hg, yc_hg = _hgrn_scan(l_hg, c_hg, hg_norm_g[l])

        x, h2 = _merge(x, h, y_att, y_gm, y_hg, w_gate, wa, wg, wh, wo, g_post_mix[l], mods(2, False),
                       g_pre_ffn[l], mods(3, False), mods(4, False), TM)
        x = _ffn(x, h2, wup, cw, cb, wdn, g_post_ffn[l], mods(5, False), TM)

        if not last:
            yc_att = _attention(cq, [(ck, cv)], lam, att_subln_g[l], 1.0 - lam_init, L)
            yc_gm = _gmlp(hc, w_uv, gm_ln_g[l], gm_ln_b[l], ws, bias_full, TM)
            xc, hc2 = _merge(flat(xc), hc, flat(yc_att), yc_gm, flat(yc_hg), w_gate, wa, wg, wh, wo, g_post_mix[l],
                             cmod(2), g_pre_ffn[l], cmod(3), cmod(4), TM)
            xc = _ffn(unflat(xc), unflat(hc2), wup, cw, cb, wdn, g_post_ffn[l], mods(5, True), L)
    return x
```

```python
import functools
import math

import jax
import jax.numpy as jnp
import numpy as np
from jax import lax
from jax.experimental import pallas as pl
from jax.experimental.pallas import tpu as pltpu

F32 = jnp.float32
BF16 = jnp.bfloat16

D_MODEL = 1024
DEPTH = 2
GRID_W = 64
ATT_HEADS = 8
ATT_DH = 64
ATT_DV = 2 * ATT_DH
GM_WIDTH = 1024
GM_GROUPS = 8
GM_CHUNK = 128
HG_HEADS = 8
HG_DK = 128
HG_DV = 128
HG_CHUNK = 64
HG_LEVELS = 6
HG_GROUP = 8
HG_OUT_GROUP = 16
N_BRANCH = 3
D_FF = 2816
FF_CHUNK = 256
ROPE_BASE = 10000.0
EPS = 1e-6

ATT_QW = ATT_HEADS * 2 * ATT_DH
ATT_VW = ATT_HEADS * ATT_DV
HG_KW = HG_HEADS * HG_DK
HG_VW = HG_HEADS * HG_DV
IN_SPLITS = (ATT_QW, ATT_QW, ATT_VW, GM_WIDTH, GM_WIDTH, HG_KW, HG_KW, HG_KW, HG_VW, HG_VW,
             N_BRANCH * D_MODEL)

ROW_TILE = 512
WIDE_ROW_TILE = 1024
ADA_ROWS = 16
ADA_COL_TILE = 1536

LANES = 128
BF16_ROWS = 16
MXU_N = 256
VMEM_LIMIT = 56 * 1024 * 1024

ATT_Q_SCALE = ATT_DH ** -0.5 * math.log2(math.e)
ATT_KEY_CHUNK = 512


def _params(*sem):
    return pltpu.CompilerParams(dimension_semantics=sem, vmem_limit_bytes=VMEM_LIMIT)


def _resident(shape):
    nd = len(shape)
    return pl.BlockSpec(shape, lambda *_: (0,) * nd, pipeline_mode=pl.Buffered(1))


def _dot(a, b):
    return jnp.dot(a, b, preferred_element_type=F32)


def _dot_nt(a, b):
    return lax.dot_general(a, b, (((1,), (1,)), ((), ())), preferred_element_type=F32)


def _rms(x, g):
    return x * lax.rsqrt(jnp.mean(x * x, axis=-1, keepdims=True) + EPS) * g


def _sigmoid(x):
    return 0.5 * jnp.tanh(0.5 * x) + 0.5


def _silu(x):
    h = 0.5 * x
    return h + h * jnp.tanh(h)


def _gelu_tanh(x):
    return 0.5 * x * (1.0 + jnp.tanh(math.sqrt(2.0 / math.pi) * (x + 0.044715 * (x * x * x))))


def _ada_kernel(c_ref, w_ref, b_ref, o_ref):
    c = c_ref[...]
    o_ref[0] = _dot(_silu(c).astype(BF16), w_ref[0].astype(BF16)) + b_ref[0]


def _ada(c_all, w_ada, b_ada):
    rows = c_all.shape[0]
    n = w_ada.shape[-1]
    tn = ADA_COL_TILE
    return pl.pallas_call(
        _ada_kernel,
        grid=(DEPTH, n // tn),
        in_specs=[
            pl.BlockSpec((rows, D_MODEL), lambda l, j: (0, 0)),
            pl.BlockSpec((1, D_MODEL, tn), lambda l, j: (l, 0, j)),
            pl.BlockSpec((1, 1, tn), lambda l, j: (l, 0, j)),
        ],
        out_specs=pl.BlockSpec((1, rows, tn), lambda l, j: (l, 0, j)),
        out_shape=jax.ShapeDtypeStruct((DEPTH, rows, n), F32),
        compiler_params=_params("parallel", "parallel"),
        name="ada",
    )(c_all, w_ada, b_ada.reshape(DEPTH, 1, n))


def _prenorm_kernel(x_ref, g_ref, sh_ref, sc_ref, h_ref):
    y = _rms(x_ref[0], g_ref[...])
    h_ref[0] = (y * (1.0 + sc_ref[0]) + sh_ref[0]).astype(BF16)


def _prenorm(x, g, sh, sc, tm):
    B, T, D = x.shape
    row = pl.BlockSpec((1, tm, D), lambda b, i: (b, i, 0))
    vec = pl.BlockSpec((1, 1, D), lambda b, i: (b, 0, 0))
    return pl.pallas_call(
        _prenorm_kernel,
        grid=(B, T // tm),
        in_specs=[row, pl.BlockSpec((1, D), lambda b, i: (0, 0)), vec, vec],
        out_specs=row,
        out_shape=jax.ShapeDtypeStruct((B, T, D), BF16),
        compiler_params=_params("parallel", "parallel"),
        name="prenorm",
    )(x, g.reshape(1, D), sh, sc)


def _att_proj_kernel(h_ref, w_ref, *rest, rope):
    if rope:
        cos_ref, sa_ref, sb_ref, q_ref, k_ref, v_ref = rest
        cos, sa, sb = cos_ref[...], sa_ref[...], sb_ref[...]
    else:
        q_ref, k_ref, v_ref = rest
    h = h_ref[0]

    def rot(a):
        half = ATT_DH // 4
        return a * cos + pltpu.roll(a, LANES - half, 1) * sa + pltpu.roll(a, half, 1) * sb

    for j in range(ATT_QW // MXU_N):
        cols = slice(j * MXU_N, (j + 1) * MXU_N)
        aq = _dot(h, w_ref[:, j * MXU_N:(j + 1) * MXU_N])
        ak = _dot(h, w_ref[:, ATT_QW + j * MXU_N:ATT_QW + (j + 1) * MXU_N])
        av = _dot(h, w_ref[:, 2 * ATT_QW + j * MXU_N:2 * ATT_QW + (j + 1) * MXU_N])
        if rope:
            aq = jnp.concatenate([rot(aq[:, :LANES]), rot(aq[:, LANES:])], axis=1)
            ak = jnp.concatenate([rot(ak[:, :LANES]), rot(ak[:, LANES:])], axis=1)
        q_ref[0, :, cols] = (aq * ATT_Q_SCALE).astype(BF16)
        k_ref[0, :, cols] = ak.astype(BF16)
        v_ref[0, :, cols] = av.astype(BF16)


def _att_proj(h, w_qkv, tables, tm):
    B, T, D = h.shape
    rope = tables is not None
    row = pl.BlockSpec((1, tm, D), lambda b, i: (b, i, 0))
    out = pl.BlockSpec((1, tm, ATT_QW), lambda b, i: (b, i, 0))
    in_specs = [row, _resident(w_qkv.shape)]
    args = [h, w_qkv]
    if rope:
        in_specs += [pl.BlockSpec((tm, LANES), lambda b, i: (i, 0))] * 3
        args += list(tables)
    return pl.pallas_call(
        functools.partial(_att_proj_kernel, rope=rope),
        grid=(B, T // tm),
        in_specs=in_specs,
        out_specs=[out, out, out],
        out_shape=[jax.ShapeDtypeStruct((B, T, ATT_QW), BF16)] * 3,
        compiler_params=_params("parallel", "parallel"),
        name="att_proj",
    )(*args)


def _attn_kernel(lam_ref, q_ref, g_ref, *rest, n_kv, post_scale):
    kv = rest[:2 * n_kv]
    o_ref = rest[2 * n_kv]
    vext = rest[2 * n_kv + 1:]

    @pl.when(pl.program_id(2) == 0)
    def _():
        for j in range(n_kv):
            v = kv[2 * j + 1][0]
            vext[j][:, :LANES] = v
            vext[j][:, LANES:] = jnp.ones_like(v)

    q = q_ref[0]
    lane = lax.broadcasted_iota(jnp.int32, q.shape, 1)
    zero = jnp.zeros_like(q)
    outs = []
    for i in range(2):
        qi = jnp.where(lane < ATT_DH, q, zero) if i == 0 else jnp.where(lane >= ATT_DH, q, zero)
        m = acc = None
        for j in range(n_kv):
            n_keys = kv[2 * j].shape[1]
            kc = min(n_keys, ATT_KEY_CHUNK)
            for c0 in range(0, n_keys, kc):
                s = _dot_nt(qi, kv[2 * j][0, c0:c0 + kc, :])
                mc = jnp.max(s, axis=-1, keepdims=True)
                m_new = mc if m is None else jnp.maximum(m, mc)
                pv = _dot(jnp.exp2(s - m_new).astype(BF16), vext[j][c0:c0 + kc, :])
                acc = pv if m is None else acc * jnp.exp2(m - m_new) + pv
                m = m_new
        outs.append(acc[:, :LANES] / acc[:, LANES:])
    o = outs[0] - lam_ref[0] * outs[1]
    o_ref[0] = (_rms(o, g_ref[...]) * post_scale).astype(BF16)


def _attention(q, kvs, lam, g, post_scale, tq):
    B, T, _ = q.shape
    in_specs = [
        pl.BlockSpec(memory_space=pltpu.SMEM),
        pl.BlockSpec((1, tq, LANES), lambda b, h, i: (b, i, h)),
        pl.BlockSpec((1, LANES), lambda b, h, i: (0, 0)),
    ]
    args = [lam.reshape(1), q, g.reshape(1, ATT_DV)]
    scratch = []
    for k, v in kvs:
        spec = pl.BlockSpec((1, k.shape[1], LANES), lambda b, h, i: (b, 0, h))
        in_specs += [spec, spec]
        args += [k, v]
        scratch.append(pltpu.VMEM((k.shape[1], 2 * LANES), BF16))
    return pl.pallas_call(
        functools.partial(_attn_kernel, n_kv=len(kvs), post_scale=post_scale),
        grid=(B, ATT_HEADS, T // tq),
        in_specs=in_specs,
        out_specs=pl.BlockSpec((1, tq, LANES), lambda b, h, i: (b, i, h)),
        out_shape=jax.ShapeDtypeStruct((B, T, ATT_VW), BF16),
        scratch_shapes=scratch,
        compiler_params=_params("parallel", "parallel", "arbitrary"),
        name="attention",
    )(*args)


def _gmlp_kernel(h_ref, w_ref, lng_ref, lnb_ref, ws_ref, bias_ref, o_ref, *, tm):
    h = h_ref[0]
    u = _gelu_tanh(_dot(h, w_ref[:, :GM_WIDTH]))
    v = _gelu_tanh(_dot(h, w_ref[:, GM_WIDTH:]))
    mu = jnp.mean(v, axis=-1, keepdims=True)
    vc = v - mu
    var = jnp.mean(vc * vc, axis=-1, keepdims=True)
    vn = (vc * lax.rsqrt(var + EPS) * lng_ref[...] + lnb_ref[...]).astype(BF16)
    dg = GM_WIDTH // GM_GROUPS
    for r in range(tm // GM_CHUNK):
        rows = slice(r * GM_CHUNK, (r + 1) * GM_CHUNK)
        for g in range(GM_GROUPS):
            cols = slice(g * dg, (g + 1) * dg)
            s = _dot(ws_ref[g], vn[rows, cols]) + bias_ref[:, cols]
            o_ref[0, rows, cols] = (u[rows, cols] * s).astype(BF16)


def _gmlp(h, w_uv, ln_g, ln_b, ws, bias_full, tm):
    B, T, D = h.shape
    return pl.pallas_call(
        functools.partial(_gmlp_kernel, tm=tm),
        grid=(B, T // tm),
        in_specs=[
            pl.BlockSpec((1, tm, D), lambda b, i: (b, i, 0)),
            _resident(w_uv.shape),
            _resident((1, GM_WIDTH)),
            _resident((1, GM_WIDTH)),
            _resident(ws.shape),
            _resident(bias_full.shape),
        ],
        out_specs=pl.BlockSpec((1, tm, GM_WIDTH), lambda b, i: (b, i, 0)),
        out_shape=jax.ShapeDtypeStruct((B, T, GM_WIDTH), BF16),
        compiler_params=_params("parallel", "parallel"),
        name="gmlp",
    )(h, w_uv, ln_g.reshape(1, GM_WIDTH), ln_b.reshape(1, GM_WIDTH), ws, bias_full)


def _hgrn_proj_kernel(h_ref, w_ref, lb_ref, q_ref, kf_ref, lff_ref, kb_ref, lfb_ref, v_ref, g_ref):
    h = h_ref[0]

    def gates(z, d, cols):
        lb = lb_ref[3 * d + 0:3 * d + 1, cols]
        log_1m = lb_ref[3 * d + 1:3 * d + 2, cols]
        one_m = lb_ref[3 * d + 2:3 * d + 3, cols]
        e = jnp.exp(-jnp.abs(z))
        one_p = 1.0 + e
        r = 1.0 / one_p
        er = e * r
        pos = z >= 0.0
        b = log_1m + (jnp.minimum(z, 0.0) - jnp.log(one_p))
        logf = jnp.maximum(jnp.log(lb + one_m * jnp.where(pos, r, er)), b)
        return logf, one_m * jnp.where(pos, er, r)

    def proj(j):
        return [_dot(h, w_ref[:, seg * HG_KW + j * MXU_N:seg * HG_KW + (j + 1) * MXU_N]) for seg in range(5)]

    n_chunks = HG_KW // MXU_N
    z = proj(0)
    for j in range(n_chunks):
        z_next = proj(j + 1) if j + 1 < n_chunks else None
        cols = slice(j * MXU_N, (j + 1) * MXU_N)
        q_ref[0, :, cols] = _silu(z[0]).astype(BF16)
        lf, k = gates(z[1], 0, cols)
        lff_ref[0, :, cols] = lf
        kf_ref[0, :, cols] = k.astype(BF16)
        lf, k = gates(z[2], 1, cols)
        lfb_ref[0, :, cols] = lf
        kb_ref[0, :, cols] = k.astype(BF16)
        v_ref[0, :, cols] = z[3].astype(BF16)
        g_ref[0, :, cols] = _silu(z[4]).astype(BF16)
        z = z_next


def _hgrn_proj(h, w_hg, lb_tab, tm):
    B, T, D = h.shape
    out = pl.BlockSpec((1, tm, HG_KW), lambda b, i: (b, i, 0))
    bf = jax.ShapeDtypeStruct((B, T, HG_KW), BF16)
    f32 = jax.ShapeDtypeStruct((B, T, HG_KW), F32)
    return pl.pallas_call(
        _hgrn_proj_kernel,
        grid=(B, T // tm),
        in_specs=[pl.BlockSpec((1, tm, D), lambda b, i: (b, i, 0)), _resident(w_hg.shape),
                  _resident(lb_tab.shape)],
        out_specs=[out] * 7,
        out_shape=[bf, bf, f32, bf, f32, bf, bf],
        compiler_params=_params("parallel", "parallel"),
        name="hgrn_proj",
    )(h, w_hg, lb_tab)


def _split2(x):
    a = x.astype(BF16)
    return a, (x - a.astype(F32)).astype(BF16)


def _hgrn_decay_matrix():
    C = HG_CHUNK
    t = np.arange(C)[:, None]
    u = np.arange(C)[None, :]
    out = []
    for rev in (False, True):
        blocks = [u >= t] if rev else [u <= t]
        for l in range(1, HG_LEVELS):
            half = 1 << l
            mid = (t & -(2 * half)) + half
            upper = (t & half) != 0
            if rev:
                blocks.append(np.where(upper, (u >= mid) & (u < t), (u >= t) & (u < mid)))
            else:
                blocks.append(np.where(upper, (u >= mid) & (u <= t), (u > t) & (u < mid)))
        m = np.concatenate(blocks, axis=0).astype(np.float32)
        out += [m, m]
    return jnp.asarray(np.concatenate(out, axis=1), dtype=BF16)


def _hgrn_local(blocks, dmat, consts):
    C = HG_CHUNK
    G = range(len(blocks))
    level_of_pair, eye, r_in = consts
    qf = [blocks[g][0].astype(F32) for g in G]
    kf = [(blocks[g][1].astype(F32), blocks[g][2].astype(F32)) for g in G]
    z = jnp.zeros((C, HG_DK), BF16)

    def block_diag(lf_f, lf_b):
        f1, f2 = _split2(lf_f)
        b1, b2 = _split2(lf_b)
        return jnp.concatenate([jnp.concatenate([f1, z], axis=1), jnp.concatenate([f2, z], axis=1),
                                jnp.concatenate([z, b1], axis=1), jnp.concatenate([z, b2], axis=1)], axis=0)

    logs = [_dot(dmat, block_diag(blocks[g][3], blocks[g][4])) for g in G]
    logs = [(logs[g][:, :HG_DK], logs[g][:, HG_DK:]) for g in G]
    ex = [[jnp.exp(logs[g][d]) for d in range(2)] for g in G]
    q_in = [[(qf[g] * ex[g][d][0:C]).astype(BF16) for d in range(2)] for g in G]
    tot = [[logs[g][0][C - 1:C], logs[g][1][0:1]] for g in G]
    k_out = [[(kf[g][d] * jnp.exp(tot[g][d] - logs[g][d][0:C])).astype(BF16) for d in range(2)] for g in G]
    g_tot = [[ex[g][0][C - 1:C], ex[g][1][0:1]] for g in G]
    scores = [jnp.where(eye, jnp.sum(qf[g] * (kf[g][0] + kf[g][1]), axis=-1, keepdims=True), 0.0) for g in G]
    odd = (r_in & 1) != 0
    for g in G:
        k_prev = pltpu.roll(kf[g][0], 1, 0) * jnp.exp(blocks[g][3])
        k_next = pltpu.roll(kf[g][1], C - 1, 0) * jnp.exp(blocks[g][4])
        pair = jnp.sum(qf[g] * jnp.where(odd, k_prev, k_next), axis=-1, keepdims=True)
        scores[g] = jnp.where(level_of_pair == 1, pair, scores[g])
    for l in range(1, HG_LEVELS):
        upper = (r_in & (1 << l)) != 0
        rows = slice(l * C, (l + 1) * C)
        for g in G:
            qe = [qf[g] * ex[g][d][rows] for d in range(2)]
            ke = [kf[g][d] * ex[g][d][rows] for d in range(2)]
            qs = jnp.concatenate([jnp.where(upper, qe[0], 0.0), jnp.where(upper, 0.0, qe[1])], axis=1)
            ks = jnp.concatenate([jnp.where(upper, 0.0, ke[0]), jnp.where(upper, ke[1], 0.0)], axis=1)
            scores[g] = jnp.where(level_of_pair == l + 1, _dot_nt(qs.astype(BF16), ks.astype(BF16)), scores[g])
    o = [_dot(scores[g].astype(BF16), blocks[g][5]) for g in G]
    upd = [_dot(blocks[g][5].astype(F32).T.astype(BF16), jnp.concatenate(k_out[g], axis=1)) for g in G]
    return [jnp.concatenate(q_in[g], axis=1) for g in G], upd, g_tot, o


def _hgrn_scan_kernel(g_ref, d_ref, q_ref, kf_ref, lff_ref, kb_ref, lfb_ref, v_ref, og_ref,
                      cq_ref, ckf_ref, clff_ref, ckb_ref, clfb_ref, cv_ref, cog_ref,
                      y_ref, yc_ref, o_ref, qin_ref, upd_ref, hist_ref, gtot_ref, st_ref, *, T, L):
    C = HG_CHUNK
    row_id = lax.broadcasted_iota(jnp.int32, (C, C), 0)
    col_id = lax.broadcasted_iota(jnp.int32, (C, C), 1)
    r_in = lax.broadcasted_iota(jnp.int32, (C, HG_DK), 0)
    diff = row_id ^ col_id
    level_of_pair = functools.reduce(jnp.add, [(diff >= (1 << j)).astype(jnp.int32) for j in range(HG_LEVELS)])
    consts = (level_of_pair, row_id == col_id, r_in)

    st_ref[...] = jnp.zeros_like(st_ref)

    def block_rows(c):
        return pl.ds(pl.multiple_of(c * C, C), C)

    def scan(q_r, k_rs, lf_rs, v_r, og_r, y_r, n):
        def local(i, carry):
            cs = [i * group + g for g in range(group)]
            blocks = [(q_r[0, block_rows(c), :], k_rs[0][0, block_rows(c), :], k_rs[1][0, block_rows(c), :],
                       lf_rs[0][0, block_rows(c), :], lf_rs[1][0, block_rows(c), :], v_r[0, block_rows(c), :])
                      for c in cs]
            q_in, upd, g_tot, o = _hgrn_local(blocks, d_ref[...], consts)
            for g, c in enumerate(cs):
                qin_ref[block_rows(c), :] = q_in[g]
                upd_ref[c] = upd[g]
                for d in range(2):
                    gtot_ref[d, pl.ds(pl.multiple_of(c * 8, 8), 8), :] = jnp.broadcast_to(g_tot[g][d], (8, HG_DK))
                o_ref[block_rows(c), :] = o[g]
            return carry

        def carry_states(i, carry):
            for d in range(2):
                c = i if d == 0 else n - 1 - i
                lanes = slice(d * HG_DK, (d + 1) * HG_DK)
                st = st_ref[d]
                hist_ref[c, :, lanes] = st.astype(BF16)
                g_tot = gtot_ref[d, pl.ds(pl.multiple_of(c * 8, 8), 8), :][0:1]
                st_ref[d] = st * g_tot + upd_ref[c, :, lanes]
            return carry

        group = min(n, HG_GROUP)
        out_group = min(n, HG_OUT_GROUP)

        def outputs(i, carry):
            cs = [i * out_group + g for g in range(out_group)]
            o = [o_ref[block_rows(c), :] + _dot_nt(qin_ref[block_rows(c), :], hist_ref[c]) for c in cs]
            for c, o_c in zip(cs, o):
                y_r[0, block_rows(c), :] = (_rms(o_c, g_ref[...]) * og_r[0, block_rows(c), :].astype(F32)
                                            ).astype(BF16)
            return carry

        lax.fori_loop(0, n // group, local, 0)
        lax.fori_loop(0, n, carry_states, 0, unroll=4)
        lax.fori_loop(0, n // out_group, outputs, 0)

    scan(cq_ref, (ckf_ref, ckb_ref), (clff_ref, clfb_ref), cv_ref, cog_ref, yc_ref, L // C)
    scan(q_ref, (kf_ref, kb_ref), (lff_ref, lfb_ref), v_ref, og_ref, y_ref, T // C)


def _hgrn_scan(lat, ctx, g):
    B, T, _ = lat[0].shape
    L = ctx[0].shape[1]
    for n in (T // HG_CHUNK, L // HG_CHUNK):
        assert n * HG_CHUNK in (T, L) and n % min(n, HG_GROUP) == 0 and n % min(n, HG_OUT_GROUP) == 0
    dmat = _hgrn_decay_matrix()
    lat_spec = pl.BlockSpec((1, T, LANES), lambda b, h: (b, 0, h))
    ctx_spec = pl.BlockSpec((1, L, LANES), lambda b, h: (b, 0, h))
    return pl.pallas_call(
        functools.partial(_hgrn_scan_kernel, T=T, L=L),
        grid=(B, HG_HEADS),
        in_specs=[pl.BlockSpec((1, HG_DV), lambda b, h: (0, 0)), _resident(dmat.shape)]
        + [lat_spec] * 7 + [ctx_spec] * 7,
        out_specs=[lat_spec, ctx_spec],
        out_shape=[jax.ShapeDtypeStruct((B, T, HG_VW), BF16), jax.ShapeDtypeStruct((B, L, HG_VW), BF16)],
        scratch_shapes=[
            pltpu.VMEM((T, HG_DV), F32),
            pltpu.VMEM((T, 2 * HG_DK), BF16),
            pltpu.VMEM((T // HG_CHUNK, HG_DV, 2 * HG_DK), F32),
            pltpu.VMEM((T // HG_CHUNK, HG_DV, 2 * HG_DK), BF16),
            pltpu.VMEM((2, T // HG_CHUNK * 8, HG_DK), F32),
            pltpu.VMEM((2, HG_DV, HG_DK), F32),
        ],
        compiler_params=_params("parallel", "parallel"),
        name="hgrn_scan",
    )(g.reshape(1, HG_DV), dmat, *lat, *ctx)


def _merge_kernel(x_ref, h_ref, ya_ref, yg_ref, yh_ref, wgate_ref, wa_ref, wg_ref, wh_ref, wo_ref,
                  gpost_ref, gt_ref, gpre_ref, sh_ref, sc_ref, x1_ref, h2_ref):
    h = h_ref[0]
    D = D_MODEL
    y = _sigmoid(_dot(h, wgate_ref[:, :D])) * _dot(ya_ref[0], wa_ref[...])
    y = y + _sigmoid(_dot(h, wgate_ref[:, D:2 * D])) * _dot(yg_ref[0], wg_ref[...])
    y = y + _sigmoid(_dot(h, wgate_ref[:, 2 * D:])) * _dot(yh_ref[0], wh_ref[...])
    z = _dot(y.astype(BF16), wo_ref[...])
    x1 = x_ref[0] + gt_ref[0] * _rms(z, gpost_ref[...])
    x1_ref[0] = x1
    h2_ref[0] = (_rms(x1, gpre_ref[...]) * (1.0 + sc_ref[0]) + sh_ref[0]).astype(BF16)


def _merge(x, h, ya, yg, yh, wgate, wa, wg, wh, wo, g_post, gt, g_pre, sh, sc, tm):
    B, T, D = x.shape
    row = pl.BlockSpec((1, tm, D), lambda b, i: (b, i, 0))
    vec = pl.BlockSpec((1, 1, D), lambda b, i: (b, 0, 0))
    par = pl.BlockSpec((1, D), lambda b, i: (0, 0))
    return pl.pallas_call(
        _merge_kernel,
        grid=(B, T // tm),
        in_specs=[row] * 5 + [_resident(wgate.shape)] + [_resident(wa.shape)] * 4 + [par, vec, par, vec, vec],
        out_specs=[row, row],
        out_shape=[jax.ShapeDtypeStruct((B, T, D), F32), jax.ShapeDtypeStruct((B, T, D), BF16)],
        compiler_params=_params("parallel", "parallel"),
        name="merge",
    )(x, h, ya, yg, yh, wgate, wa, wg, wh, wo, g_post.reshape(1, D), gt, g_pre.reshape(1, D), sh, sc)


def _ffn_kernel(x_ref, h_ref, hp_ref, hn_ref, wup_ref, cw_ref, cb_ref, wdn_ref, gpost_ref, gt_ref, *rest,
                tm, emit_next):
    if emit_next:
        gnext_ref, shnext_ref, scnext_ref, o_ref, hnext_ref, act_ref = rest
    else:
        o_ref, act_ref = rest
    i = pl.program_id(1)
    last = pl.num_programs(1) - 1
    halo = BF16_ROWS
    hp = jnp.where(i > 0, hp_ref[0], jnp.zeros_like(hp_ref[0]))
    hn = jnp.where(i < last, hn_ref[0], jnp.zeros_like(hn_ref[0]))
    h_ext = jnp.concatenate([hp, h_ref[0], hn], axis=0)
    rows = tm + 2 * halo

    def up_conv(cols):
        u = _dot(h_ext, wup_ref[:, cols])
        cw = cw_ref[:, cols]
        u_prev = pltpu.roll(u, 1, 0)
        u_next = pltpu.roll(u, rows - 1, 0)
        return (u_prev * cw[0:1] + u * cw[1:2] + u_next * cw[2:3] + cb_ref[:, cols])[halo:halo + tm]

    for n in range(D_FF // FF_CHUNK):
        a = up_conv(slice(n * FF_CHUNK, (n + 1) * FF_CHUNK))
        b = up_conv(slice(D_FF + n * FF_CHUNK, D_FF + (n + 1) * FF_CHUNK))
        act_ref[:, n * FF_CHUNK:(n + 1) * FF_CHUNK] = (_silu(a) * b).astype(BF16)
    y = _dot(act_ref[...], wdn_ref[...])
    x_out = x_ref[0] + gt_ref[0] * _rms(y, gpost_ref[...])
    o_ref[0] = x_out
    if emit_next:
        hnext_ref[0] = (_rms(x_out, gnext_ref[...]) * (1.0 + scnext_ref[0]) + shnext_ref[0]).astype(BF16)


def _ffn(x, h, wup, cw, cb, wdn, g_post, gt, tm, next_norm=None):
    B, T, D = x.shape
    nh = tm // BF16_ROWS
    row = pl.BlockSpec((1, tm, D), lambda b, i: (b, i, 0))
    prev = pl.BlockSpec((1, BF16_ROWS, D), lambda b, i: (b, jnp.maximum(i * nh - 1, 0), 0))
    nxt = pl.BlockSpec((1, BF16_ROWS, D), lambda b, i: (b, jnp.minimum((i + 1) * nh, T // BF16_ROWS - 1), 0))
    par = pl.BlockSpec((1, D), lambda b, i: (0, 0))
    vec = pl.BlockSpec((1, 1, D), lambda b, i: (b, 0, 0))
    in_specs = [row, row, prev, nxt, _resident(wup.shape), _resident(cw.shape), _resident(cb.shape),
                _resident(wdn.shape), par, vec]
    args = [x, h, h, h, wup, cw, cb, wdn, g_post.reshape(1, D), gt]
    out_specs, out_shape = row, jax.ShapeDtypeStruct((B, T, D), F32)
    if next_norm is not None:
        g_next, sh_next, sc_next = next_norm
        in_specs += [par, vec, vec]
        args += [g_next.reshape(1, D), sh_next, sc_next]
        out_specs, out_shape = [row, row], [out_shape, jax.ShapeDtypeStruct((B, T, D), BF16)]
    return pl.pallas_call(
        functools.partial(_ffn_kernel, tm=tm, emit_next=next_norm is not None),
        grid=(B, T // tm),
        in_specs=in_specs,
        out_specs=out_specs,
        out_shape=out_shape,
        scratch_shapes=[pltpu.VMEM((tm, D_FF), BF16)],
        compiler_params=_params("parallel", "parallel"),
        name="ffn",
    )(*args)


def _rope_tables(T):
    half = ATT_DH // 4
    inv = ROPE_BASE ** (-jnp.arange(half, dtype=F32) / half)
    n_rows = T // GRID_W
    ang_r = jnp.arange(n_rows, dtype=F32)[:, None] * inv[None, :]
    ang_c = jnp.arange(GRID_W, dtype=F32)[:, None] * inv[None, :]

    def by_row(a):
        return jnp.repeat(a, GRID_W, axis=0)

    def by_col(a):
        return jnp.tile(a, (n_rows, 1))

    cos_r, sin_r = by_row(jnp.cos(ang_r)), by_row(jnp.sin(ang_r))
    cos_c, sin_c = by_col(jnp.cos(ang_c)), by_col(jnp.sin(ang_c))
    z = jnp.zeros_like(cos_r)
    cos64 = jnp.concatenate([cos_r, cos_r, cos_c, cos_c], axis=1)
    sa64 = jnp.concatenate([-sin_r, z, -sin_c, z], axis=1)
    sb64 = jnp.concatenate([z, sin_r, z, sin_c], axis=1)
    return tuple(jnp.concatenate([a, a], axis=1) for a in (cos64, sa64, sb64))


def _col_offsets():
    offs, start = [], 0
    for w in IN_SPLITS:
        offs.append(start)
        start += w
    return offs


def kernel(x, c, ctx, c_ctx, w_ada, b_ada, g_pre_mix, g_post_mix, g_pre_ffn, g_post_ffn, w_in, lam_q1, lam_k1, lam_q2, lam_k2, att_subln_g, gm_ln_g, gm_ln_b, gm_ws, gm_bs, hg_lb, hg_norm_g, w_br_att, w_br_gm, w_br_hg, w_out, w_up, conv_w, conv_b, w_down):
    B, T, D = x.shape
    L = ctx.shape[1]
    TM, TQ = ROW_TILE, WIDE_ROW_TILE
    assert T % WIDE_ROW_TILE == 0 and (B * L) % WIDE_ROW_TILE == 0 and B + 1 <= ADA_ROWS

    c_all = jnp.concatenate([c, c_ctx[None, :], jnp.zeros((ADA_ROWS - B - 1, D), F32)], axis=0)
    mod = _ada(c_all, w_ada, b_ada)

    lb_all = jnp.cumsum(jax.nn.softmax(hg_lb.astype(F32), axis=0), axis=0)
    lb_all = lb_all - lb_all[0]
    tables = _rope_tables(T)
    offs = _col_offsets()

    xc = ctx
    for l in range(DEPTH):
        last = l == DEPTH - 1
        lam_init = 0.8 - 0.6 * math.exp(-0.3 * l)
        lam = (jnp.exp(jnp.sum(lam_q1[l] * lam_k1[l])) - jnp.exp(jnp.sum(lam_q2[l] * lam_k2[l])) + lam_init)

        def mods(k, ctx_row, layer=l):
            m = mod[layer, :, k * D:(k + 1) * D]
            if ctx_row:
                return jnp.broadcast_to(m[B:B + 1], (B, D)).reshape(B, 1, D)
            return m[:B].reshape(B, 1, D)

        w_l = w_in[l].astype(BF16)
        w_qkv = w_l[:, offs[0]:offs[3]]
        w_uv = w_l[:, offs[3]:offs[5]]
        w_hg = w_l[:, offs[5]:offs[10]]
        w_gate = w_l[:, offs[10]:]
        lb = lb_all[l]
        lb_tab = jnp.stack([lb[0], jnp.log1p(-lb[0]), 1.0 - lb[0],
                            lb[1], jnp.log1p(-lb[1]), 1.0 - lb[1],
                            jnp.zeros_like(lb[0]), jnp.zeros_like(lb[0])], axis=0)
        ws = gm_ws[l].astype(BF16)
        dg = GM_WIDTH // GM_GROUPS
        bias_full = jnp.broadcast_to(gm_bs[l].T[:, :, None], (GM_CHUNK, GM_GROUPS, dg)).reshape(GM_CHUNK, GM_WIDTH)
        wa, wg, wh, wo = (w[l].astype(BF16) for w in (w_br_att, w_br_gm, w_br_hg, w_out))
        wup = w_up[l].astype(BF16)
        cw = conv_w[l]
        cb = conv_b[l].reshape(1, 2 * D_FF)
        wdn = w_down[l].astype(BF16)

        def flat(a):
            return a.reshape(1, B * L, a.shape[-1])

        def unflat(a):
            return a.reshape(B, L, a.shape[-1])

        def cmod(k):
            return mod[l, B:B + 1, k * D:(k + 1) * D].reshape(1, 1, D)

        if l == 0:
            hc = _prenorm(flat(xc), g_pre_mix[l], cmod(0), cmod(1), TM)
            h = _prenorm(x, g_pre_mix[l], mods(0, False), mods(1, False), TM)
        cq, ck, cv = (unflat(a) for a in _att_proj(hc, w_qkv, None, TM))
        q, k, v = _att_proj(h, w_qkv, tables, WIDE_ROW_TILE)
        c_hg = [unflat(a) for a in _hgrn_proj(hc, w_hg, lb_tab, TM)]
        l_hg = _hgrn_proj(h, w_hg, lb_tab, WIDE_ROW_TILE)

        y_att = _attention(q, [(k, v), (ck, cv)], lam, att_subln_g[l], 1.0 - lam_init, TQ)
        y_gm = _gmlp(h, w_uv, gm_ln_g[l], gm_ln_b[l], ws, bias_full, WIDE_ROW_TILE)
        y_hg, yc_hg = _hgrn_scan(l_hg, c_hg, hg_norm_g[l])

        x, h2 = _merge(x, h, y_att, y_gm, y_hg, w_gate, wa, wg, wh, wo, g_post_mix[l], mods(2, False),
                       g_pre_ffn[l], mods(3, False), mods(4, False), TM)
        if last:
            x = _ffn(x, h2, wup, cw, cb, wdn, g_post_ffn[l], mods(5, False), TM)
        else:
            x, h_next = _ffn(x, h2, wup, cw, cb, wdn, g_post_ffn[l], mods(5, False), TM,
                             (g_pre_mix[l + 1], mods(0, False, l + 1), mods(1, False, l + 1)))

        if not last:
            yc_att = _attention(cq, [(ck, cv)], lam, att_subln_g[l], 1.0 - lam_init, L)
            yc_gm = _gmlp(hc, w_uv, gm_ln_g[l], gm_ln_b[l], ws, bias_full, TM)
            xc, hc2 = _merge(flat(xc), hc, flat(yc_att), yc_gm, flat(yc_hg), w_gate, wa, wg, wh, wo, g_post_mix[l],
                             cmod(2), g_pre_ffn[l], cmod(3), cmod(4), TM)
            xc, hc_next = _ffn(unflat(xc), unflat(hc2), wup, cw, cb, wdn, g_post_ffn[l], mods(5, True), L,
                               (g_pre_mix[l + 1], mods(0, True, l + 1), mods(1, True, l + 1)))
            h, hc = h_next, flat(hc_next)
    return x
```

```python
import functools
import math

import jax
import jax.numpy as jnp
import numpy as np
from jax import lax
from jax.experimental import pallas as pl
from jax.experimental.pallas import tpu as pltpu

F32 = jnp.float32
BF16 = jnp.bfloat16

D_MODEL = 1024
DEPTH = 2
GRID_W = 64
ATT_HEADS = 8
ATT_DH = 64
ATT_DV = 2 * ATT_DH
GM_WIDTH = 1024
GM_GROUPS = 8
GM_CHUNK = 128
HG_HEADS = 8
HG_DK = 128
HG_DV = 128
HG_CHUNK = 64
HG_LEVELS = 6
HG_GROUP = 8
HG_OUT_GROUP = 32
N_BRANCH = 3
D_FF = 2816
FF_CHUNK = 256
ROPE_BASE = 10000.0
EPS = 1e-6

ATT_QW = ATT_HEADS * 2 * ATT_DH
ATT_VW = ATT_HEADS * ATT_DV
HG_KW = HG_HEADS * HG_DK
HG_VW = HG_HEADS * HG_DV
IN_SPLITS = (ATT_QW, ATT_QW, ATT_VW, GM_WIDTH, GM_WIDTH, HG_KW, HG_KW, HG_KW, HG_VW, HG_VW,
             N_BRANCH * D_MODEL)

ROW_TILE = 512
WIDE_ROW_TILE = 1024
ADA_ROWS = 16
ADA_COL_TILE = 1536

LANES = 128
BF16_ROWS = 16
MXU_N = 256
VMEM_LIMIT = 56 * 1024 * 1024

ATT_Q_SCALE = ATT_DH ** -0.5 * math.log2(math.e)
ATT_KEY_CHUNK = 512


def _params(*sem):
    return pltpu.CompilerParams(dimension_semantics=sem, vmem_limit_bytes=VMEM_LIMIT)


def _resident(shape):
    nd = len(shape)
    return pl.BlockSpec(shape, lambda *_: (0,) * nd, pipeline_mode=pl.Buffered(1))


def _dot(a, b):
    return jnp.dot(a, b, preferred_element_type=F32)


def _dot_nt(a, b):
    return lax.dot_general(a, b, (((1,), (1,)), ((), ())), preferred_element_type=F32)


def _rms(x, g):
    return x * lax.rsqrt(jnp.mean(x * x, axis=-1, keepdims=True) + EPS) * g


def _sigmoid(x):
    return 0.5 * jnp.tanh(0.5 * x) + 0.5


def _silu(x):
    h = 0.5 * x
    return h + h * jnp.tanh(h)


def _gelu_tanh(x):
    return 0.5 * x * (1.0 + jnp.tanh(math.sqrt(2.0 / math.pi) * (x + 0.044715 * (x * x * x))))


def _ada_kernel(c_ref, w_ref, b_ref, o_ref):
    c = c_ref[...]
    o_ref[0] = _dot(_silu(c).astype(BF16), w_ref[0].astype(BF16)) + b_ref[0]


def _ada(c_all, w_ada, b_ada):
    rows = c_all.shape[0]
    n = w_ada.shape[-1]
    tn = ADA_COL_TILE
    return pl.pallas_call(
        _ada_kernel,
        grid=(DEPTH, n // tn),
        in_specs=[
            pl.BlockSpec((rows, D_MODEL), lambda l, j: (0, 0)),
            pl.BlockSpec((1, D_MODEL, tn), lambda l, j: (l, 0, j)),
            pl.BlockSpec((1, 1, tn), lambda l, j: (l, 0, j)),
        ],
        out_specs=pl.BlockSpec((1, rows, tn), lambda l, j: (l, 0, j)),
        out_shape=jax.ShapeDtypeStruct((DEPTH, rows, n), F32),
        compiler_params=_params("parallel", "parallel"),
        name="ada",
    )(c_all, w_ada, b_ada.reshape(DEPTH, 1, n))


def _prenorm_kernel(x_ref, g_ref, sh_ref, sc_ref, h_ref):
    y = _rms(x_ref[0], g_ref[...])
    h_ref[0] = (y * (1.0 + sc_ref[0]) + sh_ref[0]).astype(BF16)


def _prenorm(x, g, sh, sc, tm):
    B, T, D = x.shape
    row = pl.BlockSpec((1, tm, D), lambda b, i: (b, i, 0))
    vec = pl.BlockSpec((1, 1, D), lambda b, i: (b, 0, 0))
    return pl.pallas_call(
        _prenorm_kernel,
        grid=(B, T // tm),
        in_specs=[row, pl.BlockSpec((1, D), lambda b, i: (0, 0)), vec, vec],
        out_specs=row,
        out_shape=jax.ShapeDtypeStruct((B, T, D), BF16),
        compiler_params=_params("parallel", "parallel"),
        name="prenorm",
    )(x, g.reshape(1, D), sh, sc)


def _att_proj_kernel(h_ref, w_ref, *rest, rope):
    if rope:
        cos_ref, sa_ref, sb_ref, q_ref, k_ref, v_ref = rest
        cos, sa, sb = cos_ref[...], sa_ref[...], sb_ref[...]
    else:
        q_ref, k_ref, v_ref = rest
    h = h_ref[0]

    def rot(a):
        half = ATT_DH // 4
        return a * cos + pltpu.roll(a, LANES - half, 1) * sa + pltpu.roll(a, half, 1) * sb

    for j in range(ATT_QW // MXU_N):
        cols = slice(j * MXU_N, (j + 1) * MXU_N)
        aq = _dot(h, w_ref[:, j * MXU_N:(j + 1) * MXU_N])
        ak = _dot(h, w_ref[:, ATT_QW + j * MXU_N:ATT_QW + (j + 1) * MXU_N])
        av = _dot(h, w_ref[:, 2 * ATT_QW + j * MXU_N:2 * ATT_QW + (j + 1) * MXU_N])
        if rope:
            aq = jnp.concatenate([rot(aq[:, :LANES]), rot(aq[:, LANES:])], axis=1)
            ak = jnp.concatenate([rot(ak[:, :LANES]), rot(ak[:, LANES:])], axis=1)
        q_ref[0, :, cols] = (aq * ATT_Q_SCALE).astype(BF16)
        k_ref[0, :, cols] = ak.astype(BF16)
        v_ref[0, :, cols] = av.astype(BF16)


def _att_proj(h, w_qkv, tables, tm):
    B, T, D = h.shape
    rope = tables is not None
    row = pl.BlockSpec((1, tm, D), lambda b, i: (b, i, 0))
    out = pl.BlockSpec((1, tm, ATT_QW), lambda b, i: (b, i, 0))
    in_specs = [row, _resident(w_qkv.shape)]
    args = [h, w_qkv]
    if rope:
        in_specs += [pl.BlockSpec((tm, LANES), lambda b, i: (i, 0))] * 3
        args += list(tables)
    return pl.pallas_call(
        functools.partial(_att_proj_kernel, rope=rope),
        grid=(B, T // tm),
        in_specs=in_specs,
        out_specs=[out, out, out],
        out_shape=[jax.ShapeDtypeStruct((B, T, ATT_QW), BF16)] * 3,
        compiler_params=_params("parallel", "parallel"),
        name="att_proj",
    )(*args)


def _attn_kernel(lam_ref, q_ref, g_ref, *rest, n_kv, post_scale):
    kv = rest[:2 * n_kv]
    o_ref = rest[2 * n_kv]
    vext = rest[2 * n_kv + 1:]

    @pl.when(pl.program_id(2) == 0)
    def _():
        for j in range(n_kv):
            v = kv[2 * j + 1][0]
            vext[j][:, :LANES] = v
            vext[j][:, LANES:] = jnp.ones_like(v)

    q = q_ref[0]
    lane = lax.broadcasted_iota(jnp.int32, q.shape, 1)
    zero = jnp.zeros_like(q)
    outs = []
    for i in range(2):
        qi = jnp.where(lane < ATT_DH, q, zero) if i == 0 else jnp.where(lane >= ATT_DH, q, zero)
        m = acc = None
        for j in range(n_kv):
            n_keys = kv[2 * j].shape[1]
            kc = min(n_keys, ATT_KEY_CHUNK)
            for c0 in range(0, n_keys, kc):
                s = _dot_nt(qi, kv[2 * j][0, c0:c0 + kc, :])
                mc = jnp.max(s, axis=-1, keepdims=True)
                m_new = mc if m is None else jnp.maximum(m, mc)
                pv = _dot(jnp.exp2(s - m_new).astype(BF16), vext[j][c0:c0 + kc, :])
                acc = pv if m is None else acc * jnp.exp2(m - m_new) + pv
                m = m_new
        outs.append(acc[:, :LANES] / acc[:, LANES:])
    o = outs[0] - lam_ref[0] * outs[1]
    o_ref[0] = (_rms(o, g_ref[...]) * post_scale).astype(BF16)


def _attention(q, kvs, lam, g, post_scale, tq):
    B, T, _ = q.shape
    in_specs = [
        pl.BlockSpec(memory_space=pltpu.SMEM),
        pl.BlockSpec((1, tq, LANES), lambda b, h, i: (b, i, h)),
        pl.BlockSpec((1, LANES), lambda b, h, i: (0, 0)),
    ]
    args = [lam.reshape(1), q, g.reshape(1, ATT_DV)]
    scratch = []
    for k, v in kvs:
        spec = pl.BlockSpec((1, k.shape[1], LANES), lambda b, h, i: (b, 0, h))
        in_specs += [spec, spec]
        args += [k, v]
        scratch.append(pltpu.VMEM((k.shape[1], 2 * LANES), BF16))
    return pl.pallas_call(
        functools.partial(_attn_kernel, n_kv=len(kvs), post_scale=post_scale),
        grid=(B, ATT_HEADS, T // tq),
        in_specs=in_specs,
        out_specs=pl.BlockSpec((1, tq, LANES), lambda b, h, i: (b, i, h)),
        out_shape=jax.ShapeDtypeStruct((B, T, ATT_VW), BF16),
        scratch_shapes=scratch,
        compiler_params=_params("parallel", "parallel", "arbitrary"),
        name="attention",
    )(*args)


def _gmlp_kernel(h_ref, w_ref, lng_ref, lnb_ref, ws_ref, bias_ref, o_ref, *, tm):
    h = h_ref[0]
    u = _gelu_tanh(_dot(h, w_ref[:, :GM_WIDTH]))
    v = _gelu_tanh(_dot(h, w_ref[:, GM_WIDTH:]))
    mu = jnp.mean(v, axis=-1, keepdims=True)
    vc = v - mu
    var = jnp.mean(vc * vc, axis=-1, keepdims=True)
    vn = (vc * lax.rsqrt(var + EPS) * lng_ref[...] + lnb_ref[...]).astype(BF16)
    dg = GM_WIDTH // GM_GROUPS
    for r in range(tm // GM_CHUNK):
        rows = slice(r * GM_CHUNK, (r + 1) * GM_CHUNK)
        for g in range(GM_GROUPS):
            cols = slice(g * dg, (g + 1) * dg)
            s = _dot(ws_ref[g], vn[rows, cols]) + bias_ref[:, cols]
            o_ref[0, rows, cols] = (u[rows, cols] * s).astype(BF16)


def _gmlp(h, w_uv, ln_g, ln_b, ws, bias_full, tm):
    B, T, D = h.shape
    return pl.pallas_call(
        functools.partial(_gmlp_kernel, tm=tm),
        grid=(B, T // tm),
        in_specs=[
            pl.BlockSpec((1, tm, D), lambda b, i: (b, i, 0)),
            _resident(w_uv.shape),
            _resident((1, GM_WIDTH)),
            _resident((1, GM_WIDTH)),
            _resident(ws.shape),
            _resident(bias_full.shape),
        ],
        out_specs=pl.BlockSpec((1, tm, GM_WIDTH), lambda b, i: (b, i, 0)),
        out_shape=jax.ShapeDtypeStruct((B, T, GM_WIDTH), BF16),
        compiler_params=_params("parallel", "parallel"),
        name="gmlp",
    )(h, w_uv, ln_g.reshape(1, GM_WIDTH), ln_b.reshape(1, GM_WIDTH), ws, bias_full)


def _hgrn_proj_kernel(h_ref, w_ref, lb_ref, q_ref, kf_ref, lff_ref, kb_ref, lfb_ref, v_ref, g_ref):
    h = h_ref[0]

    def gates(z, d, cols):
        lb = lb_ref[3 * d + 0:3 * d + 1, cols]
        log_1m = lb_ref[3 * d + 1:3 * d + 2, cols]
        one_m = lb_ref[3 * d + 2:3 * d + 3, cols]
        e = jnp.exp(-jnp.abs(z))
        one_p = 1.0 + e
        r = 1.0 / one_p
        er = e * r
        pos = z >= 0.0
        b = log_1m + (jnp.minimum(z, 0.0) - jnp.log(one_p))
        logf = jnp.maximum(jnp.log(lb + one_m * jnp.where(pos, r, er)), b)
        return logf, one_m * jnp.where(pos, er, r)

    def proj(j):
        return [_dot(h, w_ref[:, seg * HG_KW + j * MXU_N:seg * HG_KW + (j + 1) * MXU_N]) for seg in range(5)]

    n_chunks = HG_KW // MXU_N
    z = proj(0)
    for j in range(n_chunks):
        z_next = proj(j + 1) if j + 1 < n_chunks else None
        cols = slice(j * MXU_N, (j + 1) * MXU_N)
        q_ref[0, :, cols] = _silu(z[0]).astype(BF16)
        lf, k = gates(z[1], 0, cols)
        lff_ref[0, :, cols] = lf
        kf_ref[0, :, cols] = k.astype(BF16)
        lf, k = gates(z[2], 1, cols)
        lfb_ref[0, :, cols] = lf
        kb_ref[0, :, cols] = k.astype(BF16)
        v_ref[0, :, cols] = z[3].astype(BF16)
        g_ref[0, :, cols] = _silu(z[4]).astype(BF16)
        z = z_next


def _hgrn_proj(h, w_hg, lb_tab, tm):
    B, T, D = h.shape
    out = pl.BlockSpec((1, tm, HG_KW), lambda b, i: (b, i, 0))
    bf = jax.ShapeDtypeStruct((B, T, HG_KW), BF16)
    f32 = jax.ShapeDtypeStruct((B, T, HG_KW), F32)
    return pl.pallas_call(
        _hgrn_proj_kernel,
        grid=(B, T // tm),
        in_specs=[pl.BlockSpec((1, tm, D), lambda b, i: (b, i, 0)), _resident(w_hg.shape),
                  _resident(lb_tab.shape)],
        out_specs=[out] * 7,
        out_shape=[bf, bf, f32, bf, f32, bf, bf],
        compiler_params=_params("parallel", "parallel"),
        name="hgrn_proj",
    )(h, w_hg, lb_tab)


def _split2(x):
    a = x.astype(BF16)
    return a, (x - a.astype(F32)).astype(BF16)


def _hgrn_decay_matrix():
    C = HG_CHUNK
    t = np.arange(C)[:, None]
    u = np.arange(C)[None, :]
    out = []
    for rev in (False, True):
        blocks = [u >= t] if rev else [u <= t]
        for l in range(1, HG_LEVELS):
            half = 1 << l
            mid = (t & -(2 * half)) + half
            upper = (t & half) != 0
            if rev:
                blocks.append(np.where(upper, (u >= mid) & (u < t), (u >= t) & (u < mid)))
            else:
                blocks.append(np.where(upper, (u >= mid) & (u <= t), (u > t) & (u < mid)))
        m = np.concatenate(blocks, axis=0).astype(np.float32)
        out += [m, m]
    return jnp.asarray(np.concatenate(out, axis=1), dtype=BF16)


def _hgrn_local(blocks, dmat, consts):
    C = HG_CHUNK
    G = range(len(blocks))
    level_of_pair, eye, r_in = consts
    qf = [blocks[g][0].astype(F32) for g in G]
    kf = [(blocks[g][1].astype(F32), blocks[g][2].astype(F32)) for g in G]
    z = jnp.zeros((C, HG_DK), BF16)

    def block_diag(lf_f, lf_b):
        f1, f2 = _split2(lf_f)
        b1, b2 = _split2(lf_b)
        return jnp.concatenate([jnp.concatenate([f1, z], axis=1), jnp.concatenate([f2, z], axis=1),
                                jnp.concatenate([z, b1], axis=1), jnp.concatenate([z, b2], axis=1)], axis=0)

    logs = [_dot(dmat, block_diag(blocks[g][3], blocks[g][4])) for g in G]
    logs = [(logs[g][:, :HG_DK], logs[g][:, HG_DK:]) for g in G]
    ex = [[jnp.exp(logs[g][d]) for d in range(2)] for g in G]
    q_in = [[(qf[g] * ex[g][d][0:C]).astype(BF16) for d in range(2)] for g in G]
    tot = [[logs[g][0][C - 1:C], logs[g][1][0:1]] for g in G]
    k_out = [[(kf[g][d] * jnp.exp(tot[g][d] - logs[g][d][0:C])).astype(BF16) for d in range(2)] for g in G]
    g_tot = [[ex[g][0][C - 1:C], ex[g][1][0:1]] for g in G]
    scores = [jnp.where(eye, jnp.sum(qf[g] * (kf[g][0] + kf[g][1]), axis=-1, keepdims=True), 0.0) for g in G]
    odd = (r_in & 1) != 0
    for g in G:
        k_prev = pltpu.roll(kf[g][0], 1, 0) * jnp.exp(blocks[g][3])
        k_next = pltpu.roll(kf[g][1], C - 1, 0) * jnp.exp(blocks[g][4])
        pair = jnp.sum(qf[g] * jnp.where(odd, k_prev, k_next), axis=-1, keepdims=True)
        scores[g] = jnp.where(level_of_pair == 1, pair, scores[g])
    for l in range(1, HG_LEVELS):
        upper = (r_in & (1 << l)) != 0
        rows = slice(l * C, (l + 1) * C)
        for g in G:
            qe = [qf[g] * ex[g][d][rows] for d in range(2)]
            ke = [kf[g][d] * ex[g][d][rows] for d in range(2)]
            qs = jnp.concatenate([jnp.where(upper, qe[0], 0.0), jnp.where(upper, 0.0, qe[1])], axis=1)
            ks = jnp.concatenate([jnp.where(upper, 0.0, ke[0]), jnp.where(upper, ke[1], 0.0)], axis=1)
            scores[g] = jnp.where(level_of_pair == l + 1, _dot_nt(qs.astype(BF16), ks.astype(BF16)), scores[g])
    o = [_dot(scores[g].astype(BF16), blocks[g][5]) for g in G]
    upd = [_dot(blocks[g][5].astype(F32).T.astype(BF16), jnp.concatenate(k_out[g], axis=1)) for g in G]
    return [jnp.concatenate(q_in[g], axis=1) for g in G], upd, g_tot, o


def _hgrn_scan_kernel(g_ref, d_ref, q_ref, kf_ref, lff_ref, kb_ref, lfb_ref, v_ref, og_ref,
                      cq_ref, ckf_ref, clff_ref, ckb_ref, clfb_ref, cv_ref, cog_ref,
                      y_ref, yc_ref, o_ref, qin_ref, upd_ref, hist_ref, gtot_ref, st_ref, *, T, L):
    C = HG_CHUNK
    row_id = lax.broadcasted_iota(jnp.int32, (C, C), 0)
    col_id = lax.broadcasted_iota(jnp.int32, (C, C), 1)
    r_in = lax.broadcasted_iota(jnp.int32, (C, HG_DK), 0)
    diff = row_id ^ col_id
    level_of_pair = functools.reduce(jnp.add, [(diff >= (1 << j)).astype(jnp.int32) for j in range(HG_LEVELS)])
    consts = (level_of_pair, row_id == col_id, r_in)

    st_ref[...] = jnp.zeros_like(st_ref)

    def block_rows(c):
        return pl.ds(pl.multiple_of(c * C, C), C)

    def scan(q_r, k_rs, lf_rs, v_r, og_r, y_r, n):
        def local(i, carry):
            cs = [i * group + g for g in range(group)]
            blocks = [(q_r[0, block_rows(c), :], k_rs[0][0, block_rows(c), :], k_rs[1][0, block_rows(c), :],
                       lf_rs[0][0, block_rows(c), :], lf_rs[1][0, block_rows(c), :], v_r[0, block_rows(c), :])
                      for c in cs]
            q_in, upd, g_tot, o = _hgrn_local(blocks, d_ref[...], consts)
            for g, c in enumerate(cs):
                qin_ref[block_rows(c), :] = q_in[g]
                upd_ref[c] = upd[g]
                for d in range(2):
                    gtot_ref[d, pl.ds(pl.multiple_of(c * 8, 8), 8), :] = jnp.broadcast_to(g_tot[g][d], (8, HG_DK))
                o_ref[block_rows(c), :] = o[g]
            return carry

        def carry_states(i, carry):
            for d in range(2):
                c = i if d == 0 else n - 1 - i
                lanes = slice(d * HG_DK, (d + 1) * HG_DK)
                st = st_ref[d]
                hist_ref[c, :, lanes] = st.astype(BF16)
                g_tot = gtot_ref[d, pl.ds(pl.multiple_of(c * 8, 8), 8), :][0:1]
                st_ref[d] = st * g_tot + upd_ref[c, :, lanes]
            return carry

        group = min(n, HG_GROUP)
        out_group = min(n, HG_OUT_GROUP)

        def outputs(i, carry):
            cs = [i * out_group + g for g in range(out_group)]
            o = [o_ref[block_rows(c), :] + _dot_nt(qin_ref[block_rows(c), :], hist_ref[c]) for c in cs]
            for c, o_c in zip(cs, o):
                y_r[0, block_rows(c), :] = (_rms(o_c, g_ref[...]) * og_r[0, block_rows(c), :].astype(F32)
                                            ).astype(BF16)
            return carry

        lax.fori_loop(0, n // group, local, 0)
        lax.fori_loop(0, n, carry_states, 0, unroll=4)
        lax.fori_loop(0, n // out_group, outputs, 0)

    scan(cq_ref, (ckf_ref, ckb_ref), (clff_ref, clfb_ref), cv_ref, cog_ref, yc_ref, L // C)
    scan(q_ref, (kf_ref, kb_ref), (lff_ref, lfb_ref), v_ref, og_ref, y_ref, T // C)


def _hgrn_scan(lat, ctx, g):
    B, T, _ = lat[0].shape
    L = ctx[0].shape[1]
    for n in (T // HG_CHUNK, L // HG_CHUNK):
        assert n * HG_CHUNK in (T, L) and n % min(n, HG_GROUP) == 0 and n % min(n, HG_OUT_GROUP) == 0
    dmat = _hgrn_decay_matrix()
    lat_spec = pl.BlockSpec((1, T, LANES), lambda b, h: (b, 0, h))
    ctx_spec = pl.BlockSpec((1, L, LANES), lambda b, h: (b, 0, h))
    return pl.pallas_call(
        functools.partial(_hgrn_scan_kernel, T=T, L=L),
        grid=(B, HG_HEADS),
        in_specs=[pl.BlockSpec((1, HG_DV), lambda b, h: (0, 0)), _resident(dmat.shape)]
        + [lat_spec] * 7 + [ctx_spec] * 7,
        out_specs=[lat_spec, ctx_spec],
        out_shape=[jax.ShapeDtypeStruct((B, T, HG_VW), BF16), jax.ShapeDtypeStruct((B, L, HG_VW), BF16)],
        scratch_shapes=[
            pltpu.VMEM((T, HG_DV), F32),
            pltpu.VMEM((T, 2 * HG_DK), BF16),
            pltpu.VMEM((T // HG_CHUNK, HG_DV, 2 * HG_DK), F32),
            pltpu.VMEM((T // HG_CHUNK, HG_DV, 2 * HG_DK), BF16),
            pltpu.VMEM((2, T // HG_CHUNK * 8, HG_DK), F32),
            pltpu.VMEM((2, HG_DV, HG_DK), F32),
        ],
        compiler_params=_params("parallel", "parallel"),
        name="hgrn_scan",
    )(g.reshape(1, HG_DV), dmat, *lat, *ctx)


def _merge_kernel(x_ref, h_ref, ya_ref, yg_ref, yh_ref, wgate_ref, wa_ref, wg_ref, wh_ref, wo_ref,
                  gpost_ref, gt_ref, gpre_ref, sh_ref, sc_ref, x1_ref, h2_ref):
    h = h_ref[0]
    D = D_MODEL
    y = _sigmoid(_dot(h, wgate_ref[:, :D])) * _dot(ya_ref[0], wa_ref[...])
    y = y + _sigmoid(_dot(h, wgate_ref[:, D:2 * D])) * _dot(yg_ref[0], wg_ref[...])
    y = y + _sigmoid(_dot(h, wgate_ref[:, 2 * D:])) * _dot(yh_ref[0], wh_ref[...])
    z = _dot(y.astype(BF16), wo_ref[...])
    x1 = x_ref[0] + gt_ref[0] * _rms(z, gpost_ref[...])
    x1_ref[0] = x1
    h2_ref[0] = (_rms(x1, gpre_ref[...]) * (1.0 + sc_ref[0]) + sh_ref[0]).astype(BF16)


def _merge(x, h, ya, yg, yh, wgate, wa, wg, wh, wo, g_post, gt, g_pre, sh, sc, tm):
    B, T, D = x.shape
    row = pl.BlockSpec((1, tm, D), lambda b, i: (b, i, 0))
    vec = pl.BlockSpec((1, 1, D), lambda b, i: (b, 0, 0))
    par = pl.BlockSpec((1, D), lambda b, i: (0, 0))
    return pl.pallas_call(
        _merge_kernel,
        grid=(B, T // tm),
        in_specs=[row] * 5 + [_resident(wgate.shape)] + [_resident(wa.shape)] * 4 + [par, vec, par, vec, vec],
        out_specs=[row, row],
        out_shape=[jax.ShapeDtypeStruct((B, T, D), F32), jax.ShapeDtypeStruct((B, T, D), BF16)],
        compiler_params=_params("parallel", "parallel"),
        name="merge",
    )(x, h, ya, yg, yh, wgate, wa, wg, wh, wo, g_post.reshape(1, D), gt, g_pre.reshape(1, D), sh, sc)


def _ffn_kernel(x_ref, h_ref, hp_ref, hn_ref, wup_ref, cw_ref, cb_ref, wdn_ref, gpost_ref, gt_ref, *rest,
                tm, emit_next):
    if emit_next:
        gnext_ref, shnext_ref, scnext_ref, o_ref, hnext_ref, act_ref = rest
    else:
        o_ref, act_ref = rest
    i = pl.program_id(1)
    last = pl.num_programs(1) - 1
    halo = BF16_ROWS
    hp = jnp.where(i > 0, hp_ref[0], jnp.zeros_like(hp_ref[0]))
    hn = jnp.where(i < last, hn_ref[0], jnp.zeros_like(hn_ref[0]))
    h_ext = jnp.concatenate([hp, h_ref[0], hn], axis=0)
    rows = tm + 2 * halo

    def up_conv(cols):
        u = _dot(h_ext, wup_ref[:, cols])
        cw = cw_ref[:, cols]
        u_prev = pltpu.roll(u, 1, 0)
        u_next = pltpu.roll(u, rows - 1, 0)
        return (u_prev * cw[0:1] + u * cw[1:2] + u_next * cw[2:3] + cb_ref[:, cols])[halo:halo + tm]

    for n in range(D_FF // FF_CHUNK):
        a = up_conv(slice(n * FF_CHUNK, (n + 1) * FF_CHUNK))
        b = up_conv(slice(D_FF + n * FF_CHUNK, D_FF + (n + 1) * FF_CHUNK))
        act_ref[:, n * FF_CHUNK:(n + 1) * FF_CHUNK] = (_silu(a) * b).astype(BF16)
    y = _dot(act_ref[...], wdn_ref[...])
    x_out = x_ref[0] + gt_ref[0] * _rms(y, gpost_ref[...])
    o_ref[0] = x_out
    if emit_next:
        hnext_ref[0] = (_rms(x_out, gnext_ref[...]) * (1.0 + scnext_ref[0]) + shnext_ref[0]).astype(BF16)


def _ffn(x, h, wup, cw, cb, wdn, g_post, gt, tm, next_norm=None):
    B, T, D = x.shape
    nh = tm // BF16_ROWS
    row = pl.BlockSpec((1, tm, D), lambda b, i: (b, i, 0))
    prev = pl.BlockSpec((1, BF16_ROWS, D), lambda b, i: (b, jnp.maximum(i * nh - 1, 0), 0))
    nxt = pl.BlockSpec((1, BF16_ROWS, D), lambda b, i: (b, jnp.minimum((i + 1) * nh, T // BF16_ROWS - 1), 0))
    par = pl.BlockSpec((1, D), lambda b, i: (0, 0))
    vec = pl.BlockSpec((1, 1, D), lambda b, i: (b, 0, 0))
    in_specs = [row, row, prev, nxt, _resident(wup.shape), _resident(cw.shape), _resident(cb.shape),
                _resident(wdn.shape), par, vec]
    args = [x, h, h, h, wup, cw, cb, wdn, g_post.reshape(1, D), gt]
    out_specs, out_shape = row, jax.ShapeDtypeStruct((B, T, D), F32)
    if next_norm is not None:
        g_next, sh_next, sc_next = next_norm
        in_specs += [par, vec, vec]
        args += [g_next.reshape(1, D), sh_next, sc_next]
        out_specs, out_shape = [row, row], [out_shape, jax.ShapeDtypeStruct((B, T, D), BF16)]
    return pl.pallas_call(
        functools.partial(_ffn_kernel, tm=tm, emit_next=next_norm is not None),
        grid=(B, T // tm),
        in_specs=in_specs,
        out_specs=out_specs,
        out_shape=out_shape,
        scratch_shapes=[pltpu.VMEM((tm, D_FF), BF16)],
        compiler_params=_params("parallel", "parallel"),
        name="ffn",
    )(*args)


def _rope_tables(T):
    half = ATT_DH // 4
    inv = ROPE_BASE ** (-jnp.arange(half, dtype=F32) / half)
    n_rows = T // GRID_W
    ang_r = jnp.arange(n_rows, dtype=F32)[:, None] * inv[None, :]
    ang_c = jnp.arange(GRID_W, dtype=F32)[:, None] * inv[None, :]

    def by_row(a):
        return jnp.repeat(a, GRID_W, axis=0)

    def by_col(a):
        return jnp.tile(a, (n_rows, 1))

    cos_r, sin_r = by_row(jnp.cos(ang_r)), by_row(jnp.sin(ang_r))
    cos_c, sin_c = by_col(jnp.cos(ang_c)), by_col(jnp.sin(ang_c))
    z = jnp.zeros_like(cos_r)
    cos64 = jnp.concatenate([cos_r, cos_r, cos_c, cos_c], axis=1)
    sa64 = jnp.concatenate([-sin_r, z, -sin_c, z], axis=1)
    sb64 = jnp.concatenate([z, sin_r, z, sin_c], axis=1)
    return tuple(jnp.concatenate([a, a], axis=1) for a in (cos64, sa64, sb64))


def _col_offsets():
    offs, start = [], 0
    for w in IN_SPLITS:
        offs.append(start)
        start += w
    return offs


def kernel(x, c, ctx, c_ctx, w_ada, b_ada, g_pre_mix, g_post_mix, g_pre_ffn, g_post_ffn, w_in, lam_q1, lam_k1, lam_q2, lam_k2, att_subln_g, gm_ln_g, gm_ln_b, gm_ws, gm_bs, hg_lb, hg_norm_g, w_br_att, w_br_gm, w_br_hg, w_out, w_up, conv_w, conv_b, w_down):
    B, T, D = x.shape
    L = ctx.shape[1]
    TM, TQ = ROW_TILE, WIDE_ROW_TILE
    assert T % WIDE_ROW_TILE == 0 and (B * L) % WIDE_ROW_TILE == 0 and B + 1 <= ADA_ROWS

    c_all = jnp.concatenate([c, c_ctx[None, :], jnp.zeros((ADA_ROWS - B - 1, D), F32)], axis=0)
    mod = _ada(c_all, w_ada, b_ada)

    lb_all = jnp.cumsum(jax.nn.softmax(hg_lb.astype(F32), axis=0), axis=0)
    lb_all = lb_all - lb_all[0]
    tables = _rope_tables(T)
    offs = _col_offsets()

    xc = ctx
    for l in range(DEPTH):
        last = l == DEPTH - 1
        lam_init = 0.8 - 0.6 * math.exp(-0.3 * l)
        lam = (jnp.exp(jnp.sum(lam_q1[l] * lam_k1[l])) - jnp.exp(jnp.sum(lam_q2[l] * lam_k2[l])) + lam_init)

        def mods(k, ctx_row, layer=l):
            m = mod[layer, :, k * D:(k + 1) * D]
            if ctx_row:
                return jnp.broadcast_to(m[B:B + 1], (B, D)).reshape(B, 1, D)
            return m[:B].reshape(B, 1, D)

        w_qkv, w_uv, w_hg, w_gate = (w_in[l, :, a:b].astype(BF16) for a, b in
                                     ((offs[0], offs[3]), (offs[3], offs[5]), (offs[5], offs[10]), (offs[10], None)))
        lb = lb_all[l]
        lb_tab = jnp.stack([lb[0], jnp.log1p(-lb[0]), 1.0 - lb[0],
                            lb[1], jnp.log1p(-lb[1]), 1.0 - lb[1],
                            jnp.zeros_like(lb[0]), jnp.zeros_like(lb[0])], axis=0)
        ws = gm_ws[l].astype(BF16)
        dg = GM_WIDTH // GM_GROUPS
        bias_full = jnp.broadcast_to(gm_bs[l].T[:, :, None], (GM_CHUNK, GM_GROUPS, dg)).reshape(GM_CHUNK, GM_WIDTH)
        wa, wg, wh, wo = (w[l].astype(BF16) for w in (w_br_att, w_br_gm, w_br_hg, w_out))
        wup = w_up[l].astype(BF16)
        cw = conv_w[l]
        cb = conv_b[l].reshape(1, 2 * D_FF)
        wdn = w_down[l].astype(BF16)

        def flat(a):
            return a.reshape(1, B * L, a.shape[-1])

        def unflat(a):
            return a.reshape(B, L, a.shape[-1])

        def cmod(k):
            return mod[l, B:B + 1, k * D:(k + 1) * D].reshape(1, 1, D)

        if l == 0:
            hc = _prenorm(flat(xc), g_pre_mix[l], cmod(0), cmod(1), TM)
            h = _prenorm(x, g_pre_mix[l], mods(0, False), mods(1, False), TM)
        cq, ck, cv = (unflat(a) for a in _att_proj(hc, w_qkv, None, TM))
        q, k, v = _att_proj(h, w_qkv, tables, WIDE_ROW_TILE)
        c_hg = [unflat(a) for a in _hgrn_proj(hc, w_hg, lb_tab, TM)]
        l_hg = _hgrn_proj(h, w_hg, lb_tab, WIDE_ROW_TILE)

        y_att = _attention(q, [(k, v), (ck, cv)], lam, att_subln_g[l], 1.0 - lam_init, TQ)
        y_gm = _gmlp(h, w_uv, gm_ln_g[l], gm_ln_b[l], ws, bias_full, WIDE_ROW_TILE)
        y_hg, yc_hg = _hgrn_scan(l_hg, c_hg, hg_norm_g[l])

        x, h2 = _merge(x, h, y_att, y_gm, y_hg, w_gate, wa, wg, wh, wo, g_post_mix[l], mods(2, False),
                       g_pre_ffn[l], mods(3, False), mods(4, False), TM)
        if last:
            x = _ffn(x, h2, wup, cw, cb, wdn, g_post_ffn[l], mods(5, False), TM)
        else:
            x, h_next = _ffn(x, h2, wup, cw, cb, wdn, g_post_ffn[l], mods(5, False), TM,
                             (g_pre_mix[l + 1], mods(0, False, l + 1), mods(1, False, l + 1)))

        if not last:
            yc_att = _attention(cq, [(ck, cv)], lam, att_subln_g[l], 1.0 - lam_init, L)
            yc_gm = _gmlp(hc, w_uv, gm_ln_g[l], gm_ln_b[l], ws, bias_full, TM)
            xc, hc2 = _merge(flat(xc), hc, flat(yc_att), yc_gm, flat(yc_hg), w_gate, wa, wg, wh, wo, g_post_mix[l],
                             cmod(2), g_pre_ffn[l], cmod(3), cmod(4), TM)
            xc, hc_next = _ffn(unflat(xc), unflat(hc2), wup, cw, cb, wdn, g_post_ffn[l], mods(5, True), L,
                               (g_pre_mix[l + 1], mods(0, True, l + 1), mods(1, True, l + 1)))
            h, hc = h_next, flat(hc_next)
    return x
```

```python
import functools
import math

import jax
import jax.numpy as jnp
import numpy as np
from jax import lax
from jax.experimental import pallas as pl
from jax.experimental.pallas import tpu as pltpu

F32 = jnp.float32
BF16 = jnp.bfloat16

D_MODEL = 1024
DEPTH = 2
GRID_W = 64
ATT_HEADS = 8
ATT_DH = 64
ATT_DV = 2 * ATT_DH
GM_WIDTH = 1024
GM_GROUPS = 8
GM_CHUNK = 128
HG_HEADS = 8
HG_DK = 128
HG_DV = 128
HG_CHUNK = 64
HG_LEVELS = 6
HG_GROUP = 8
HG_OUT_GROUP = 32
N_BRANCH = 3
D_FF = 2816
FF_CHUNK = 256
ROPE_BASE = 10000.0
EPS = 1e-6

ATT_QW = ATT_HEADS * 2 * ATT_DH
ATT_VW = ATT_HEADS * ATT_DV
HG_KW = HG_HEADS * HG_DK
HG_VW = HG_HEADS * HG_DV
IN_SPLITS = (ATT_QW, ATT_QW, ATT_VW, GM_WIDTH, GM_WIDTH, HG_KW, HG_KW, HG_KW, HG_VW, HG_VW,
             N_BRANCH * D_MODEL)
IN_OFFSETS = tuple(sum(IN_SPLITS[:i]) for i in range(len(IN_SPLITS)))

ROW_TILE = 512
WIDE_ROW_TILE = 1024
ADA_ROWS = 16
ADA_COL_TILE = 1536

LANES = 128
BF16_ROWS = 16
MXU_N = 256
VMEM_LIMIT = 56 * 1024 * 1024

ATT_Q_SCALE = ATT_DH ** -0.5 * math.log2(math.e)
ATT_KEY_CHUNK = 512


def _params(*sem):
    return pltpu.CompilerParams(dimension_semantics=sem, vmem_limit_bytes=VMEM_LIMIT)


def _resident(shape):
    nd = len(shape)
    return pl.BlockSpec(shape, lambda *_: (0,) * nd, pipeline_mode=pl.Buffered(1))


def _w_cols(w_all, layer, start, width):
    assert start % width == 0
    return pl.BlockSpec((None, w_all.shape[1], width), lambda *_: (layer, 0, start // width),
                        pipeline_mode=pl.Buffered(1))


def _dot(a, b):
    return jnp.dot(a, b, preferred_element_type=F32)


def _dot_nt(a, b):
    return lax.dot_general(a, b, (((1,), (1,)), ((), ())), preferred_element_type=F32)


def _rms(x, g):
    return x * lax.rsqrt(jnp.mean(x * x, axis=-1, keepdims=True) + EPS) * g


def _sigmoid(x):
    return 0.5 * jnp.tanh(0.5 * x) + 0.5


def _silu(x):
    h = 0.5 * x
    return h + h * jnp.tanh(h)


def _gelu_tanh(x):
    return 0.5 * x * (1.0 + jnp.tanh(math.sqrt(2.0 / math.pi) * (x + 0.044715 * (x * x * x))))


def _ada_kernel(c_ref, w_ref, b_ref, o_ref):
    c = c_ref[...]
    o_ref[0] = _dot(_silu(c).astype(BF16), w_ref[0].astype(BF16)) + b_ref[0]


def _ada(c_all, w_ada, b_ada):
    rows = c_all.shape[0]
    n = w_ada.shape[-1]
    tn = ADA_COL_TILE
    return pl.pallas_call(
        _ada_kernel,
        grid=(DEPTH, n // tn),
        in_specs=[
            pl.BlockSpec((rows, D_MODEL), lambda l, j: (0, 0)),
            pl.BlockSpec((1, D_MODEL, tn), lambda l, j: (l, 0, j)),
            pl.BlockSpec((1, 1, tn), lambda l, j: (l, 0, j)),
        ],
        out_specs=pl.BlockSpec((1, rows, tn), lambda l, j: (l, 0, j)),
        out_shape=jax.ShapeDtypeStruct((DEPTH, rows, n), F32),
        compiler_params=_params("parallel", "parallel"),
        name="ada",
    )(c_all, w_ada, b_ada.reshape(DEPTH, 1, n))


def _prenorm_kernel(x_ref, g_ref, sh_ref, sc_ref, h_ref):
    y = _rms(x_ref[0], g_ref[...])
    h_ref[0] = (y * (1.0 + sc_ref[0]) + sh_ref[0]).astype(BF16)


def _prenorm(x, g, sh, sc, tm):
    B, T, D = x.shape
    row = pl.BlockSpec((1, tm, D), lambda b, i: (b, i, 0))
    vec = pl.BlockSpec((1, 1, D), lambda b, i: (b, 0, 0))
    return pl.pallas_call(
        _prenorm_kernel,
        grid=(B, T // tm),
        in_specs=[row, pl.BlockSpec((1, D), lambda b, i: (0, 0)), vec, vec],
        out_specs=row,
        out_shape=jax.ShapeDtypeStruct((B, T, D), BF16),
        compiler_params=_params("parallel", "parallel"),
        name="prenorm",
    )(x, g.reshape(1, D), sh, sc)


def _att_proj_kernel(h_ref, w_ref, *rest, rope):
    if rope:
        cos_ref, sa_ref, sb_ref, q_ref, k_ref, v_ref = rest
        cos, sa, sb = cos_ref[...], sa_ref[...], sb_ref[...]
    else:
        q_ref, k_ref, v_ref = rest
    h = h_ref[0]

    def rot(a):
        half = ATT_DH // 4
        return a * cos + pltpu.roll(a, LANES - half, 1) * sa + pltpu.roll(a, half, 1) * sb

    for j in range(ATT_QW // MXU_N):
        cols = slice(j * MXU_N, (j + 1) * MXU_N)
        aq = _dot(h, w_ref[:, j * MXU_N:(j + 1) * MXU_N])
        ak = _dot(h, w_ref[:, ATT_QW + j * MXU_N:ATT_QW + (j + 1) * MXU_N])
        av = _dot(h, w_ref[:, 2 * ATT_QW + j * MXU_N:2 * ATT_QW + (j + 1) * MXU_N])
        if rope:
            aq = jnp.concatenate([rot(aq[:, :LANES]), rot(aq[:, LANES:])], axis=1)
            ak = jnp.concatenate([rot(ak[:, :LANES]), rot(ak[:, LANES:])], axis=1)
        q_ref[0, :, cols] = (aq * ATT_Q_SCALE).astype(BF16)
        k_ref[0, :, cols] = ak.astype(BF16)
        v_ref[0, :, cols] = av.astype(BF16)


def _att_proj(h, w_all, layer, tables, tm):
    B, T, D = h.shape
    rope = tables is not None
    row = pl.BlockSpec((1, tm, D), lambda b, i: (b, i, 0))
    out = pl.BlockSpec((1, tm, ATT_QW), lambda b, i: (b, i, 0))
    in_specs = [row, _w_cols(w_all, layer, IN_OFFSETS[0], 3 * ATT_QW)]
    args = [h, w_all]
    if rope:
        in_specs += [pl.BlockSpec((tm, LANES), lambda b, i: (i, 0))] * 3
        args += list(tables)
    return pl.pallas_call(
        functools.partial(_att_proj_kernel, rope=rope),
        grid=(B, T // tm),
        in_specs=in_specs,
        out_specs=[out, out, out],
        out_shape=[jax.ShapeDtypeStruct((B, T, ATT_QW), BF16)] * 3,
        compiler_params=_params("parallel", "parallel"),
        name="att_proj",
    )(*args)


def _attn_kernel(lam_ref, q_ref, g_ref, *rest, n_kv, post_scale):
    kv = rest[:2 * n_kv]
    o_ref = rest[2 * n_kv]
    vext = rest[2 * n_kv + 1:]

    @pl.when(pl.program_id(2) == 0)
    def _():
        for j in range(n_kv):
            v = kv[2 * j + 1][0]
            vext[j][:, :LANES] = v
            vext[j][:, LANES:] = jnp.ones_like(v)

    q = q_ref[0]
    lane = lax.broadcasted_iota(jnp.int32, q.shape, 1)
    zero = jnp.zeros_like(q)
    outs = []
    for i in range(2):
        qi = jnp.where(lane < ATT_DH, q, zero) if i == 0 else jnp.where(lane >= ATT_DH, q, zero)
        m = acc = None
        for j in range(n_kv):
            n_keys = kv[2 * j].shape[1]
            kc = min(n_keys, ATT_KEY_CHUNK)
            for c0 in range(0, n_keys, kc):
                s = _dot_nt(qi, kv[2 * j][0, c0:c0 + kc, :])
                mc = jnp.max(s, axis=-1, keepdims=True)
                m_new = mc if m is None else jnp.maximum(m, mc)
                pv = _dot(jnp.exp2(s - m_new).astype(BF16), vext[j][c0:c0 + kc, :])
                acc = pv if m is None else acc * jnp.exp2(m - m_new) + pv
                m = m_new
        outs.append(acc[:, :LANES] / acc[:, LANES:])
    o = outs[0] - lam_ref[0] * outs[1]
    o_ref[0] = (_rms(o, g_ref[...]) * post_scale).astype(BF16)


def _attention(q, kvs, lam, g, post_scale, tq):
    B, T, _ = q.shape
    in_specs = [
        pl.BlockSpec(memory_space=pltpu.SMEM),
        pl.BlockSpec((1, tq, LANES), lambda b, h, i: (b, i, h)),
        pl.BlockSpec((1, LANES), lambda b, h, i: (0, 0)),
    ]
    args = [lam.reshape(1), q, g.reshape(1, ATT_DV)]
    scratch = []
    for k, v in kvs:
        spec = pl.BlockSpec((1, k.shape[1], LANES), lambda b, h, i: (b, 0, h))
        in_specs += [spec, spec]
        args += [k, v]
        scratch.append(pltpu.VMEM((k.shape[1], 2 * LANES), BF16))
    return pl.pallas_call(
        functools.partial(_attn_kernel, n_kv=len(kvs), post_scale=post_scale),
        grid=(B, ATT_HEADS, T // tq),
        in_specs=in_specs,
        out_specs=pl.BlockSpec((1, tq, LANES), lambda b, h, i: (b, i, h)),
        out_shape=jax.ShapeDtypeStruct((B, T, ATT_VW), BF16),
        scratch_shapes=scratch,
        compiler_params=_params("parallel", "parallel", "arbitrary"),
        name="attention",
    )(*args)


def _gmlp_kernel(h_ref, wu_ref, wv_ref, lng_ref, lnb_ref, ws_ref, bias_ref, o_ref, *, tm):
    h = h_ref[0]
    u = _gelu_tanh(_dot(h, wu_ref[...]))
    v = _gelu_tanh(_dot(h, wv_ref[...]))
    mu = jnp.mean(v, axis=-1, keepdims=True)
    vc = v - mu
    var = jnp.mean(vc * vc, axis=-1, keepdims=True)
    vn = (vc * lax.rsqrt(var + EPS) * lng_ref[...] + lnb_ref[...]).astype(BF16)
    dg = GM_WIDTH // GM_GROUPS
    for r in range(tm // GM_CHUNK):
        rows = slice(r * GM_CHUNK, (r + 1) * GM_CHUNK)
        for g in range(GM_GROUPS):
            cols = slice(g * dg, (g + 1) * dg)
            s = _dot(ws_ref[g], vn[rows, cols]) + bias_ref[:, cols]
            o_ref[0, rows, cols] = (u[rows, cols] * s).astype(BF16)


def _gmlp(h, w_all, layer, ln_g, ln_b, ws, bias_full, tm):
    B, T, D = h.shape
    return pl.pallas_call(
        functools.partial(_gmlp_kernel, tm=tm),
        grid=(B, T // tm),
        in_specs=[
            pl.BlockSpec((1, tm, D), lambda b, i: (b, i, 0)),
            _w_cols(w_all, layer, IN_OFFSETS[3], GM_WIDTH),
            _w_cols(w_all, layer, IN_OFFSETS[4], GM_WIDTH),
            _resident((1, GM_WIDTH)),
            _resident((1, GM_WIDTH)),
            _resident(ws.shape),
            _resident(bias_full.shape),
        ],
        out_specs=pl.BlockSpec((1, tm, GM_WIDTH), lambda b, i: (b, i, 0)),
        out_shape=jax.ShapeDtypeStruct((B, T, GM_WIDTH), BF16),
        compiler_params=_params("parallel", "parallel"),
        name="gmlp",
    )(h, w_all, w_all, ln_g.reshape(1, GM_WIDTH), ln_b.reshape(1, GM_WIDTH), ws, bias_full)


def _hgrn_proj_kernel(h_ref, w_ref, lb_ref, q_ref, kf_ref, lff_ref, kb_ref, lfb_ref, v_ref, g_ref):
    h = h_ref[0]

    def gates(z, d, cols):
        lb = lb_ref[3 * d + 0:3 * d + 1, cols]
        log_1m = lb_ref[3 * d + 1:3 * d + 2, cols]
        one_m = lb_ref[3 * d + 2:3 * d + 3, cols]
        e = jnp.exp(-jnp.abs(z))
        one_p = 1.0 + e
        r = 1.0 / one_p
        er = e * r
        pos = z >= 0.0
        b = log_1m + (jnp.minimum(z, 0.0) - jnp.log(one_p))
        logf = jnp.maximum(jnp.log(lb + one_m * jnp.where(pos, r, er)), b)
        return logf, one_m * jnp.where(pos, er, r)

    def proj(j):
        return [_dot(h, w_ref[:, seg * HG_KW + j * MXU_N:seg * HG_KW + (j + 1) * MXU_N]) for seg in range(5)]

    n_chunks = HG_KW // MXU_N
    z = proj(0)
    for j in range(n_chunks):
        z_next = proj(j + 1) if j + 1 < n_chunks else None
        cols = slice(j * MXU_N, (j + 1) * MXU_N)
        q_ref[0, :, cols] = _silu(z[0]).astype(BF16)
        lf, k = gates(z[1], 0, cols)
        lff_ref[0, :, cols] = lf
        kf_ref[0, :, cols] = k.astype(BF16)
        lf, k = gates(z[2], 1, cols)
        lfb_ref[0, :, cols] = lf
        kb_ref[0, :, cols] = k.astype(BF16)
        v_ref[0, :, cols] = z[3].astype(BF16)
        g_ref[0, :, cols] = _silu(z[4]).astype(BF16)
        z = z_next


def _hgrn_proj(h, w_all, layer, lb_tab, tm):
    B, T, D = h.shape
    out = pl.BlockSpec((1, tm, HG_KW), lambda b, i: (b, i, 0))
    bf = jax.ShapeDtypeStruct((B, T, HG_KW), BF16)
    f32 = jax.ShapeDtypeStruct((B, T, HG_KW), F32)
    return pl.pallas_call(
        _hgrn_proj_kernel,
        grid=(B, T // tm),
        in_specs=[pl.BlockSpec((1, tm, D), lambda b, i: (b, i, 0)), _w_cols(w_all, layer, IN_OFFSETS[5], 5 * HG_KW),
                  _resident(lb_tab.shape)],
        out_specs=[out] * 7,
        out_shape=[bf, bf, f32, bf, f32, bf, bf],
        compiler_params=_params("parallel", "parallel"),
        name="hgrn_proj",
    )(h, w_all, lb_tab)


def _split2(x):
    a = x.astype(BF16)
    return a, (x - a.astype(F32)).astype(BF16)


def _hgrn_decay_matrix():
    C = HG_CHUNK
    t = np.arange(C)[:, None]
    u = np.arange(C)[None, :]
    out = []
    for rev in (False, True):
        blocks = [u >= t] if rev else [u <= t]
        for l in range(1, HG_LEVELS):
            half = 1 << l
            mid = (t & -(2 * half)) + half
            upper = (t & half) != 0
            if rev:
                blocks.append(np.where(upper, (u >= mid) & (u < t), (u >= t) & (u < mid)))
            else:
                blocks.append(np.where(upper, (u >= mid) & (u <= t), (u > t) & (u < mid)))
        m = np.concatenate(blocks, axis=0).astype(np.float32)
        out += [m, m]
    return jnp.asarray(np.concatenate(out, axis=1), dtype=BF16)


def _hgrn_local(blocks, dmat, consts):
    C = HG_CHUNK
    G = range(len(blocks))
    level_of_pair, eye, r_in = consts
    qf = [blocks[g][0].astype(F32) for g in G]
    kf = [(blocks[g][1].astype(F32), blocks[g][2].astype(F32)) for g in G]
    z = jnp.zeros((C, HG_DK), BF16)

    def block_diag(lf_f, lf_b):
        f1, f2 = _split2(lf_f)
        b1, b2 = _split2(lf_b)
        return jnp.concatenate([jnp.concatenate([f1, z], axis=1), jnp.concatenate([f2, z], axis=1),
                                jnp.concatenate([z, b1], axis=1), jnp.concatenate([z, b2], axis=1)], axis=0)

    logs = [_dot(dmat, block_diag(blocks[g][3], blocks[g][4])) for g in G]
    logs = [(logs[g][:, :HG_DK], logs[g][:, HG_DK:]) for g in G]
    ex = [[jnp.exp(logs[g][d]) for d in range(2)] for g in G]
    q_in = [[(qf[g] * ex[g][d][0:C]).astype(BF16) for d in range(2)] for g in G]
    tot = [[logs[g][0][C - 1:C], logs[g][1][0:1]] for g in G]
    k_out = [[(kf[g][d] * jnp.exp(tot[g][d] - logs[g][d][0:C])).astype(BF16) for d in range(2)] for g in G]
    g_tot = [[ex[g][0][C - 1:C], ex[g][1][0:1]] for g in G]
    scores = [jnp.where(eye, jnp.sum(qf[g] * (kf[g][0] + kf[g][1]), axis=-1, keepdims=True), 0.0) for g in G]
    odd = (r_in & 1) != 0
    for g in G:
        k_prev = pltpu.roll(kf[g][0], 1, 0) * jnp.exp(blocks[g][3])
        k_next = pltpu.roll(kf[g][1], C - 1, 0) * jnp.exp(blocks[g][4])
        pair = jnp.sum(qf[g] * jnp.where(odd, k_prev, k_next), axis=-1, keepdims=True)
        scores[g] = jnp.where(level_of_pair == 1, pair, scores[g])
    for l in range(1, HG_LEVELS):
        upper = (r_in & (1 << l)) != 0
        rows = slice(l * C, (l + 1) * C)
        for g in G:
            qe = [qf[g] * ex[g][d][rows] for d in range(2)]
            ke = [kf[g][d] * ex[g][d][rows] for d in range(2)]
            qs = jnp.concatenate([jnp.where(upper, qe[0], 0.0), jnp.where(upper, 0.0, qe[1])], axis=1)
            ks = jnp.concatenate([jnp.where(upper, 0.0, ke[0]), jnp.where(upper, ke[1], 0.0)], axis=1)
            scores[g] = jnp.where(level_of_pair == l + 1, _dot_nt(qs.astype(BF16), ks.astype(BF16)), scores[g])
    o = [_dot(scores[g].astype(BF16), blocks[g][5]) for g in G]
    upd = [_dot(blocks[g][5].astype(F32).T.astype(BF16), jnp.concatenate(k_out[g], axis=1)) for g in G]
    return [jnp.concatenate(q_in[g], axis=1) for g in G], upd, g_tot, o


def _hgrn_scan_kernel(g_ref, d_ref, q_ref, kf_ref, lff_ref, kb_ref, lfb_ref, v_ref, og_ref,
                      cq_ref, ckf_ref, clff_ref, ckb_ref, clfb_ref, cv_ref, cog_ref,
                      y_ref, yc_ref, o_ref, qin_ref, upd_ref, hist_ref, gtot_ref, st_ref, *, T, L):
    C = HG_CHUNK
    row_id = lax.broadcasted_iota(jnp.int32, (C, C), 0)
    col_id = lax.broadcasted_iota(jnp.int32, (C, C), 1)
    r_in = lax.broadcasted_iota(jnp.int32, (C, HG_DK), 0)
    diff = row_id ^ col_id
    level_of_pair = functools.reduce(jnp.add, [(diff >= (1 << j)).astype(jnp.int32) for j in range(HG_LEVELS)])
    consts = (level_of_pair, row_id == col_id, r_in)

    st_ref[...] = jnp.zeros_like(st_ref)

    def block_rows(c):
        return pl.ds(pl.multiple_of(c * C, C), C)

    def scan(q_r, k_rs, lf_rs, v_r, og_r, y_r, n):
        def local(i, carry):
            cs = [i * group + g for g in range(group)]
            blocks = [(q_r[0, block_rows(c), :], k_rs[0][0, block_rows(c), :], k_rs[1][0, block_rows(c), :],
                       lf_rs[0][0, block_rows(c), :], lf_rs[1][0, block_rows(c), :], v_r[0, block_rows(c), :])
                      for c in cs]
            q_in, upd, g_tot, o = _hgrn_local(blocks, d_ref[...], consts)
            for g, c in enumerate(cs):
                qin_ref[block_rows(c), :] = q_in[g]
                upd_ref[c] = upd[g]
                for d in range(2):
                    gtot_ref[d, pl.ds(pl.multiple_of(c * 8, 8), 8), :] = jnp.broadcast_to(g_tot[g][d], (8, HG_DK))
                o_ref[block_rows(c), :] = o[g]
            return carry

        def carry_states(i, carry):
            for d in range(2):
                c = i if d == 0 else n - 1 - i
                lanes = slice(d * HG_DK, (d + 1) * HG_DK)
                st = st_ref[d]
                hist_ref[c, :, lanes] = st.astype(BF16)
                g_tot = gtot_ref[d, pl.ds(pl.multiple_of(c * 8, 8), 8), :][0:1]
                st_ref[d] = st * g_tot + upd_ref[c, :, lanes]
            return carry

        group = min(n, HG_GROUP)
        out_group = min(n, HG_OUT_GROUP)

        def outputs(i, carry):
            cs = [i * out_group + g for g in range(out_group)]
            o = [o_ref[block_rows(c), :] + _dot_nt(qin_ref[block_rows(c), :], hist_ref[c]) for c in cs]
            for c, o_c in zip(cs, o):
                y_r[0, block_rows(c), :] = (_rms(o_c, g_ref[...]) * og_r[0, block_rows(c), :].astype(F32)
                                            ).astype(BF16)
            return carry

        lax.fori_loop(0, n // group, local, 0)
        lax.fori_loop(0, n, carry_states, 0, unroll=4)
        lax.fori_loop(0, n // out_group, outputs, 0)

    scan(cq_ref, (ckf_ref, ckb_ref), (clff_ref, clfb_ref), cv_ref, cog_ref, yc_ref, L // C)
    scan(q_ref, (kf_ref, kb_ref), (lff_ref, lfb_ref), v_ref, og_ref, y_ref, T // C)


def _hgrn_scan(lat, ctx, g):
    B, T, _ = lat[0].shape
    L = ctx[0].shape[1]
    for n in (T // HG_CHUNK, L // HG_CHUNK):
        assert n * HG_CHUNK in (T, L) and n % min(n, HG_GROUP) == 0 and n % min(n, HG_OUT_GROUP) == 0
    dmat = _hgrn_decay_matrix()
    lat_spec = pl.BlockSpec((1, T, LANES), lambda b, h: (b, 0, h))
    ctx_spec = pl.BlockSpec((1, L, LANES), lambda b, h: (b, 0, h))
    return pl.pallas_call(
        functools.partial(_hgrn_scan_kernel, T=T, L=L),
        grid=(B, HG_HEADS),
        in_specs=[pl.BlockSpec((1, HG_DV), lambda b, h: (0, 0)), _resident(dmat.shape)]
        + [lat_spec] * 7 + [ctx_spec] * 7,
        out_specs=[lat_spec, ctx_spec],
        out_shape=[jax.ShapeDtypeStruct((B, T, HG_VW), BF16), jax.ShapeDtypeStruct((B, L, HG_VW), BF16)],
        scratch_shapes=[
            pltpu.VMEM((T, HG_DV), F32),
            pltpu.VMEM((T, 2 * HG_DK), BF16),
            pltpu.VMEM((T // HG_CHUNK, HG_DV, 2 * HG_DK), F32),
            pltpu.VMEM((T // HG_CHUNK, HG_DV, 2 * HG_DK), BF16),
            pltpu.VMEM((2, T // HG_CHUNK * 8, HG_DK), F32),
            pltpu.VMEM((2, HG_DV, HG_DK), F32),
        ],
        compiler_params=_params("parallel", "parallel"),
        name="hgrn_scan",
    )(g.reshape(1, HG_DV), dmat, *lat, *ctx)


def _merge_kernel(x_ref, h_ref, ya_ref, yg_ref, yh_ref, ga_ref, gg_ref, gh_ref, wa_ref, wg_ref, wh_ref, wo_ref,
                  gpost_ref, gt_ref, gpre_ref, sh_ref, sc_ref, x1_ref, h2_ref):
    h = h_ref[0]
    y = _sigmoid(_dot(h, ga_ref[...])) * _dot(ya_ref[0], wa_ref[...])
    y = y + _sigmoid(_dot(h, gg_ref[...])) * _dot(yg_ref[0], wg_ref[...])
    y = y + _sigmoid(_dot(h, gh_ref[...])) * _dot(yh_ref[0], wh_ref[...])
    z = _dot(y.astype(BF16), wo_ref[...])
    x1 = x_ref[0] + gt_ref[0] * _rms(z, gpost_ref[...])
    x1_ref[0] = x1
    h2_ref[0] = (_rms(x1, gpre_ref[...]) * (1.0 + sc_ref[0]) + sh_ref[0]).astype(BF16)


def _merge(x, h, ya, yg, yh, w_all, layer, wa, wg, wh, wo, g_post, gt, g_pre, sh, sc, tm):
    B, T, D = x.shape
    row = pl.BlockSpec((1, tm, D), lambda b, i: (b, i, 0))
    vec = pl.BlockSpec((1, 1, D), lambda b, i: (b, 0, 0))
    par = pl.BlockSpec((1, D), lambda b, i: (0, 0))
    return pl.pallas_call(
        _merge_kernel,
        grid=(B, T // tm),
        in_specs=[row] * 5 + [_w_cols(w_all, layer, IN_OFFSETS[10] + k * D, D) for k in range(N_BRANCH)]
        + [_resident(wa.shape)] * 4 + [par, vec, par, vec, vec],
        out_specs=[row, row],
        out_shape=[jax.ShapeDtypeStruct((B, T, D), F32), jax.ShapeDtypeStruct((B, T, D), BF16)],
        compiler_params=_params("parallel", "parallel"),
        name="merge",
    )(x, h, ya, yg, yh, w_all, w_all, w_all, wa, wg, wh, wo, g_post.reshape(1, D), gt, g_pre.reshape(1, D), sh, sc)


def _ffn_kernel(x_ref, h_ref, hp_ref, hn_ref, wup_ref, cw_ref, cb_ref, wdn_ref, gpost_ref, gt_ref, *rest,
                tm, emit_next):
    if emit_next:
        gnext_ref, shnext_ref, scnext_ref, o_ref, hnext_ref, act_ref = rest
    else:
        o_ref, act_ref = rest
    i = pl.program_id(1)
    last = pl.num_programs(1) - 1
    halo = BF16_ROWS
    hp = jnp.where(i > 0, hp_ref[0], jnp.zeros_like(hp_ref[0]))
    hn = jnp.where(i < last, hn_ref[0], jnp.zeros_like(hn_ref[0]))
    h_ext = jnp.concatenate([hp, h_ref[0], hn], axis=0)
    rows = tm + 2 * halo

    def up_conv(cols):
        u = _dot(h_ext, wup_ref[:, cols])
        cw = cw_ref[:, cols]
        u_prev = pltpu.roll(u, 1, 0)
        u_next = pltpu.roll(u, rows - 1, 0)
        return (u_prev * cw[0:1] + u * cw[1:2] + u_next * cw[2:3] + cb_ref[:, cols])[halo:halo + tm]

    for n in range(D_FF // FF_CHUNK):
        a = up_conv(slice(n * FF_CHUNK, (n + 1) * FF_CHUNK))
        b = up_conv(slice(D_FF + n * FF_CHUNK, D_FF + (n + 1) * FF_CHUNK))
        act_ref[:, n * FF_CHUNK:(n + 1) * FF_CHUNK] = (_silu(a) * b).astype(BF16)
    y = _dot(act_ref[...], wdn_ref[...])
    x_out = x_ref[0] + gt_ref[0] * _rms(y, gpost_ref[...])
    o_ref[0] = x_out
    if emit_next:
        hnext_ref[0] = (_rms(x_out, gnext_ref[...]) * (1.0 + scnext_ref[0]) + shnext_ref[0]).astype(BF16)


def _ffn(x, h, wup, cw, cb, wdn, g_post, gt, tm, next_norm=None):
    B, T, D = x.shape
    nh = tm // BF16_ROWS
    row = pl.BlockSpec((1, tm, D), lambda b, i: (b, i, 0))
    prev = pl.BlockSpec((1, BF16_ROWS, D), lambda b, i: (b, jnp.maximum(i * nh - 1, 0), 0))
    nxt = pl.BlockSpec((1, BF16_ROWS, D), lambda b, i: (b, jnp.minimum((i + 1) * nh, T // BF16_ROWS - 1), 0))
    par = pl.BlockSpec((1, D), lambda b, i: (0, 0))
    vec = pl.BlockSpec((1, 1, D), lambda b, i: (b, 0, 0))
    in_specs = [row, row, prev, nxt, _resident(wup.shape), _resident(cw.shape), _resident(cb.shape),
                _resident(wdn.shape), par, vec]
    args = [x, h, h, h, wup, cw, cb, wdn, g_post.reshape(1, D), gt]
    out_specs, out_shape = row, jax.ShapeDtypeStruct((B, T, D), F32)
    if next_norm is not None:
        g_next, sh_next, sc_next = next_norm
        in_specs += [par, vec, vec]
        args += [g_next.reshape(1, D), sh_next, sc_next]
        out_specs, out_shape = [row, row], [out_shape, jax.ShapeDtypeStruct((B, T, D), BF16)]
    return pl.pallas_call(
        functools.partial(_ffn_kernel, tm=tm, emit_next=next_norm is not None),
        grid=(B, T // tm),
        in_specs=in_specs,
        out_specs=out_specs,
        out_shape=out_shape,
        scratch_shapes=[pltpu.VMEM((tm, D_FF), BF16)],
        compiler_params=_params("parallel", "parallel"),
        name="ffn",
    )(*args)


def _rope_tables(T):
    half = ATT_DH // 4
    inv = ROPE_BASE ** (-jnp.arange(half, dtype=F32) / half)
    n_rows = T // GRID_W
    ang_r = jnp.arange(n_rows, dtype=F32)[:, None] * inv[None, :]
    ang_c = jnp.arange(GRID_W, dtype=F32)[:, None] * inv[None, :]

    def by_row(a):
        return jnp.repeat(a, GRID_W, axis=0)

    def by_col(a):
        return jnp.tile(a, (n_rows, 1))

    cos_r, sin_r = by_row(jnp.cos(ang_r)), by_row(jnp.sin(ang_r))
    cos_c, sin_c = by_col(jnp.cos(ang_c)), by_col(jnp.sin(ang_c))
    z = jnp.zeros_like(cos_r)
    cos64 = jnp.concatenate([cos_r, cos_r, cos_c, cos_c], axis=1)
    sa64 = jnp.concatenate([-sin_r, z, -sin_c, z], axis=1)
    sb64 = jnp.concatenate([z, sin_r, z, sin_c], axis=1)
    return tuple(jnp.concatenate([a, a], axis=1) for a in (cos64, sa64, sb64))


def kernel(x, c, ctx, c_ctx, w_ada, b_ada, g_pre_mix, g_post_mix, g_pre_ffn, g_post_ffn, w_in, lam_q1, lam_k1, lam_q2, lam_k2, att_subln_g, gm_ln_g, gm_ln_b, gm_ws, gm_bs, hg_lb, hg_norm_g, w_br_att, w_br_gm, w_br_hg, w_out, w_up, conv_w, conv_b, w_down):
    B, T, D = x.shape
    L = ctx.shape[1]
    TM, TQ = ROW_TILE, WIDE_ROW_TILE
    assert T % WIDE_ROW_TILE == 0 and (B * L) % WIDE_ROW_TILE == 0 and B + 1 <= ADA_ROWS

    c_all = jnp.concatenate([c, c_ctx[None, :], jnp.zeros((ADA_ROWS - B - 1, D), F32)], axis=0)
    mod = _ada(c_all, w_ada, b_ada)

    lb_all = jnp.cumsum(jax.nn.softmax(hg_lb.astype(F32), axis=0), axis=0)
    lb_all = lb_all - lb_all[0]
    tables = _rope_tables(T)
    w_in_bf = w_in.astype(BF16)

    xc = ctx
    for l in range(DEPTH):
        last = l == DEPTH - 1
        lam_init = 0.8 - 0.6 * math.exp(-0.3 * l)
        lam = (jnp.exp(jnp.sum(lam_q1[l] * lam_k1[l])) - jnp.exp(jnp.sum(lam_q2[l] * lam_k2[l])) + lam_init)

        def mods(k, ctx_row, layer=l):
            m = mod[layer, :, k * D:(k + 1) * D]
            if ctx_row:
                return jnp.broadcast_to(m[B:B + 1], (B, D)).reshape(B, 1, D)
            return m[:B].reshape(B, 1, D)

        lb = lb_all[l]
        lb_tab = jnp.stack([lb[0], jnp.log1p(-lb[0]), 1.0 - lb[0],
                            lb[1], jnp.log1p(-lb[1]), 1.0 - lb[1],
                            jnp.zeros_like(lb[0]), jnp.zeros_like(lb[0])], axis=0)
        ws = gm_ws[l].astype(BF16)
        dg = GM_WIDTH // GM_GROUPS
        bias_full = jnp.broadcast_to(gm_bs[l].T[:, :, None], (GM_CHUNK, GM_GROUPS, dg)).reshape(GM_CHUNK, GM_WIDTH)
        wa, wg, wh, wo = (w[l].astype(BF16) for w in (w_br_att, w_br_gm, w_br_hg, w_out))
        wup = w_up[l].astype(BF16)
        cw = conv_w[l]
        cb = conv_b[l].reshape(1, 2 * D_FF)
        wdn = w_down[l].astype(BF16)

        def flat(a):
            return a.reshape(1, B * L, a.shape[-1])

        def unflat(a):
            return a.reshape(B, L, a.shape[-1])

        def cmod(k):
            return mod[l, B:B + 1, k * D:(k + 1) * D].reshape(1, 1, D)

        if l == 0:
            hc = _prenorm(flat(xc), g_pre_mix[l], cmod(0), cmod(1), TM)
            h = _prenorm(x, g_pre_mix[l], mods(0, False), mods(1, False), TM)
        cq, ck, cv = (unflat(a) for a in _att_proj(hc, w_in_bf, l, None, TM))
        q, k, v = _att_proj(h, w_in_bf, l, tables, WIDE_ROW_TILE)
        c_hg = [unflat(a) for a in _hgrn_proj(hc, w_in_bf, l, lb_tab, TM)]
        l_hg = _hgrn_proj(h, w_in_bf, l, lb_tab, WIDE_ROW_TILE)

        y_att = _attention(q, [(k, v), (ck, cv)], lam, att_subln_g[l], 1.0 - lam_init, TQ)
        y_gm = _gmlp(h, w_in_bf, l, gm_ln_g[l], gm_ln_b[l], ws, bias_full, WIDE_ROW_TILE)
        y_hg, yc_hg = _hgrn_scan(l_hg, c_hg, hg_norm_g[l])

        x, h2 = _merge(x, h, y_att, y_gm, y_hg, w_in_bf, l, wa, wg, wh, wo, g_post_mix[l], mods(2, False),
                       g_pre_ffn[l], mods(3, False), mods(4, False), TM)
        if last:
            x = _ffn(x, h2, wup, cw, cb, wdn, g_post_ffn[l], mods(5, False), TM)
        else:
            x, h_next = _ffn(x, h2, wup, cw, cb, wdn, g_post_ffn[l], mods(5, False), TM,
                             (g_pre_mix[l + 1], mods(0, False, l + 1), mods(1, False, l + 1)))

        if not last:
            yc_att = _attention(cq, [(ck, cv)], lam, att_subln_g[l], 1.0 - lam_init, L)
            yc_gm = _gmlp(hc, w_in_bf, l, gm_ln_g[l], gm_ln_b[l], ws, bias_full, TM)
            xc, hc2 = _merge(flat(xc), hc, flat(yc_att), yc_gm, flat(yc_hg), w_in_bf, l, wa, wg, wh, wo, g_post_mix[l],
                             cmod(2), g_pre_ffn[l], cmod(3), cmod(4), TM)
            xc, hc_next = _ffn(unflat(xc), unflat(hc2), wup, cw, cb, wdn, g_post_ffn[l], mods(5, True), L,
                               (g_pre_mix[l + 1], mods(0, True, l + 1), mods(1, True, l + 1)))
            h, hc = h_next, flat(hc_next)
    return x
```
